```python
import math
import jax
import jax.numpy as jnp
from jax import lax
import numpy as np

D_MODEL = 1024
BATCH = 2
SEQ = 8192
DEPTH = 2
DEC_BATCH = 32
DEC_SEQ = 8
PAST_LEN = 16384
PAGE_SIZE = 128

HEAD_DIM = 64
N_MIXERS = 4
GROUP_WIDTH = D_MODEL // N_MIXERS
N_HEADS_A = GROUP_WIDTH // HEAD_DIM
C_CONV = GROUP_WIDTH
CONV_WIDTH = 31
N_HEADS_C = GROUP_WIDTH // HEAD_DIM
N_BRANCH = 3
CMP_BLOCK = 32
SEL_BLOCK = 64
N_SELECT = 16
WINDOW = 512
N_GROUPS_D = GROUP_WIDTH // HEAD_DIM
CHUNK = 128
D_FF = 4 * D_MODEL
Q_BLOCK = 128
ROPE_THETA = 10000.0
EPS = 1e-6
FORCED_SCORE = 1e4
FORGET_BIAS = 4.0

SPLIT_SIZES = (GROUP_WIDTH, GROUP_WIDTH, GROUP_WIDTH, N_HEADS_A,
               2 * C_CONV,
               GROUP_WIDTH, 2 * N_BRANCH * HEAD_DIM, N_BRANCH * N_HEADS_C,
               GROUP_WIDTH, GROUP_WIDTH)
IN_WIDTH = sum(SPLIT_SIZES)
SPLIT_IDX = tuple(int(i) for i in np.cumsum(SPLIT_SIZES)[:-1])

kernel_name = 'hybrid_fox_conformer_nsa_gmlp_step'


def rmsnorm(x, g):
    xf = x.astype(jnp.float32)
    xf = xf * lax.rsqrt(jnp.mean(xf * xf, axis=-1, keepdims=True) + EPS)
    return xf.astype(x.dtype) * g


def layernorm(x, g, b):
    xf = x.astype(jnp.float32)
    mu = jnp.mean(xf, axis=-1, keepdims=True)
    var = jnp.mean(jnp.square(xf - mu), axis=-1, keepdims=True)
    return ((xf - mu) * lax.rsqrt(var + EPS)).astype(x.dtype) * g + b


def rope(x, pos):
    half = HEAD_DIM // 2
    inv = ROPE_THETA ** (-jnp.arange(half, dtype=jnp.float32) / half)
    ang = pos.astype(jnp.float32)[:, None] * inv
    ang = ang.reshape(ang.shape[:1] + (1,) * (x.ndim - 3) + ang.shape[1:])
    cos = jnp.cos(ang).astype(x.dtype)
    sin = jnp.sin(ang).astype(x.dtype)
    x1, x2 = x[..., :half], x[..., half:]
    return jnp.concatenate([x1 * cos - x2 * sin, x2 * cos + x1 * sin], axis=-1)


def masked_softmax(s, mask):
    s = jnp.where(mask, s.astype(jnp.float32), -jnp.inf)
    m = jnp.max(s, axis=-1, keepdims=True)
    m = jnp.where(jnp.isfinite(m), m, 0.0)
    e = jnp.where(mask, jnp.exp(s - m), 0.0)
    return e / jnp.maximum(jnp.sum(e, axis=-1, keepdims=True), 1e-30)


def project(xn, pos, p):
    B, T = xn.shape[:2]
    z = xn @ p['w_in']
    qa, ka, va, fa, glu_in, qc, kvc, gc, ud, vd = jnp.split(z, SPLIT_IDX, axis=-1)
    hs = (B, T, N_HEADS_A, HEAD_DIM)
    qa = rmsnorm(qa.reshape(hs), p['fox_qn_g'])
    ka = rmsnorm(ka.reshape(hs), p['fox_kn_g'])
    va = va.reshape(hs)
    logf = jax.nn.log_sigmoid(fa.astype(jnp.float32) + p['fox_bf'].astype(jnp.float32))
    ga, gb = jnp.split(glu_in, 2, axis=-1)
    glu = ga * jax.nn.sigmoid(gb)
    qc = rope(rmsnorm(qc.reshape(B, T, N_HEADS_C, HEAD_DIM), p['nsa_qn_g']), pos)
    kvc = kvc.reshape(B, T, 2 * N_BRANCH, HEAD_DIM)
    keys = rope(rmsnorm(kvc[:, :, 0::2], p['nsa_kn_g']), pos)
    kvc = jnp.stack([keys, kvc[:, :, 1::2]], axis=3).reshape(B, T, 2 * N_BRANCH, HEAD_DIM)
    gc = jax.nn.sigmoid(gc).reshape(B, T, N_HEADS_C, N_BRANCH)
    return qa, ka, va, logf, glu, qc, kvc, gc, ud, vd


def fox_attend(q, fq, qpos, k, v, fk, kpos):
    s = jnp.einsum('bqhd,bkhd->bhqk', q, k).astype(jnp.float32) * (HEAD_DIM ** -0.5)
    s = s + jnp.transpose(fq, (0, 2, 1))[..., :, None] - jnp.transpose(fk, (0, 2, 1))[..., None, :]
    mask = (kpos[None, :] <= qpos[:, None])[None, None]
    pr = masked_softmax(s, mask)
    return jnp.einsum('bhqk,bkhd->bqhd', pr.astype(v.dtype), v)


def fox_prompt(q, k, v, logf):
    B, S = q.shape[:2]
    F = jnp.cumsum(logf, axis=1)
    kpos = jnp.arange(S)

    def blk(i):
        st = i * Q_BLOCK
        qb = lax.dynamic_slice_in_dim(q, st, Q_BLOCK, axis=1)
        fb = lax.dynamic_slice_in_dim(F, st, Q_BLOCK, axis=1)
        return fox_attend(qb, fb, st + jnp.arange(Q_BLOCK), k, v, F, kpos)

    o = lax.map(blk, jnp.arange(S // Q_BLOCK))
    return jnp.moveaxis(o, 0, 1).reshape(B, S, N_HEADS_A, HEAD_DIM)


def fox_sample(q, k, v, logf, past_k, past_v, past_logf):
    P, T = past_k.shape[1], q.shape[1]
    suf = jnp.cumsum(past_logf[:, ::-1], axis=1)[:, ::-1] - past_logf
    lnew = jnp.cumsum(logf, axis=1)
    fk = jnp.concatenate([-suf, lnew], axis=1)
    keys = jnp.concatenate([past_k, k], axis=1)
    vals = jnp.concatenate([past_v, v], axis=1)
    return fox_attend(q, lnew, P + jnp.arange(T), keys, vals, fk, jnp.arange(P + T))


def conformer_conv(glu, buf, p):
    xin = jnp.concatenate([buf, glu], axis=1)
    y = lax.conv_general_dilated(xin, p['conv_w'][:, None, :], window_strides=(1,), padding='VALID',
                                 dimension_numbers=('NWC', 'WIO', 'NWC'), feature_group_count=C_CONV)
    y = layernorm(y + p['conv_b'], p['conv_ln_g'], p['conv_ln_b'])
    return jax.nn.silu(y), xin[:, -(CONV_WIDTH - 1):]


def nsa_compress(rows, p):
    B, L = rows.shape[:2]
    n = L // CMP_BLOCK
    blocks = rows[:, :n * CMP_BLOCK, :2].reshape(B, n, CMP_BLOCK, 2, HEAD_DIM)
    kb = (blocks[:, :, :, 0] + p['nsa_pe'][0]).reshape(B, n, CMP_BLOCK * HEAD_DIM)
    vb = (blocks[:, :, :, 1] + p['nsa_pe'][1]).reshape(B, n, CMP_BLOCK * HEAD_DIM)
    return kb @ p['nsa_phi_k'], vb @ p['nsa_phi_v']


def nsa_attend(q, qpos, g, kc, vc, ksel, vsel, kw, vw, kwpos):
    scale = HEAD_DIM ** -0.5
    B, Tq = q.shape[:2]
    L, nc = ksel.shape[1], kc.shape[1]
    s = jnp.einsum('bqhd,bnd->bhqn', q, kc) * scale
    cmask = (((jnp.arange(nc) + 1) * CMP_BLOCK - 1)[None, :] <= qpos[:, None])[None, None]
    p_cmp = masked_softmax(s, cmask)
    o_cmp = jnp.einsum('bhqn,bnd->bqhd', p_cmp.astype(vc.dtype), vc)
    ns = -(-L // SEL_BLOCK)
    per = SEL_BLOCK // CMP_BLOCK
    imp = jnp.sum(p_cmp, axis=1)
    imp = jnp.pad(imp, ((0, 0), (0, 0), (0, ns * per - nc))).reshape(B, Tq, ns, per).sum(-1)
    j = jnp.arange(ns)[None, :]
    cur = (qpos // SEL_BLOCK)[:, None]
    forced = (j == 0) | (j == cur) | (j == cur - 1)
    imp = jnp.where(forced, FORCED_SCORE, imp)
    imp = jnp.where(j <= cur, imp, -1.0)
    n_sel = min(N_SELECT, ns)
    top, idx = lax.top_k(imp, n_sel)
    kpos = (idx[..., None] * SEL_BLOCK + jnp.arange(SEL_BLOCK)).reshape(B, Tq, n_sel * SEL_BLOCK)
    blk_ok = jnp.broadcast_to((top >= 0)[..., None], top.shape + (SEL_BLOCK,)).reshape(B, Tq, n_sel * SEL_BLOCK)
    smask = blk_ok & (kpos <= qpos[None, :, None])
    kpos = jnp.minimum(kpos, L - 1)
    bidx = jnp.arange(B)[:, None, None]
    kg, vg = ksel[bidx, kpos], vsel[bidx, kpos]
    s = jnp.einsum('bqhd,bqkd->bhqk', q, kg) * scale
    p_sel = masked_softmax(s, smask[:, None])
    o_sel = jnp.einsum('bhqk,bqkd->bqhd', p_sel.astype(vg.dtype), vg)
    s = jnp.einsum('bqhd,bkd->bhqk', q, kw) * scale
    wmask = ((kwpos[None, :] <= qpos[:, None]) & (qpos[:, None] - kwpos[None, :] < WINDOW)
             & (kwpos[None, :] >= 0))[None, None]
    p_win = masked_softmax(s, wmask)
    o_win = jnp.einsum('bhqk,bkd->bqhd', p_win.astype(vw.dtype), vw)
    return g[..., 0:1] * o_cmp + g[..., 1:2] * o_sel + g[..., 2:3] * o_win


def nsa_prompt(q, g, kvc, p):
    B, S = q.shape[:2]
    kc, vc = nsa_compress(kvc, p)
    ksel, vsel = kvc[:, :, 2], kvc[:, :, 3]
    kwp = jnp.pad(kvc[:, :, 4], ((0, 0), (WINDOW, 0), (0, 0)))
    vwp = jnp.pad(kvc[:, :, 5], ((0, 0), (WINDOW, 0), (0, 0)))

    def blk(i):
        st = i * Q_BLOCK
        qb = lax.dynamic_slice_in_dim(q, st, Q_BLOCK, axis=1)
        gb = lax.dynamic_slice_in_dim(g, st, Q_BLOCK, axis=1)
        kwb = lax.dynamic_slice_in_dim(kwp, st, WINDOW + Q_BLOCK, axis=1)
        vwb = lax.dynamic_slice_in_dim(vwp, st, WINDOW + Q_BLOCK, axis=1)
        kwpos = st - WINDOW + jnp.arange(WINDOW + Q_BLOCK)
        return nsa_attend(qb, st + jnp.arange(Q_BLOCK), gb, kc, vc, ksel, vsel, kwb, vwb, kwpos)

    o = lax.map(blk, jnp.arange(S // Q_BLOCK))
    return jnp.moveaxis(o, 0, 1).reshape(B, S, N_HEADS_C, HEAD_DIM)


def gmlp_gate(u, v, p):
    B, T = u.shape[:2]
    vn = layernorm(v, p['gmlp_ln_g'], p['gmlp_ln_b'])
    nch = -(-T // CHUNK)
    vp = jnp.pad(vn, ((0, 0), (0, nch * CHUNK - T), (0, 0))).reshape(B, nch, CHUNK, N_GROUPS_D, HEAD_DIM)
    wm = p['gmlp_ws'] * jnp.tril(jnp.ones((CHUNK, CHUNK), p['gmlp_ws'].dtype))
    mixed = jnp.einsum('gts,bcsgd->bctgd', wm, vp) + jnp.transpose(p['gmlp_bs'])[:, :, None]
    mixed = mixed.reshape(B, nch * CHUNK, GROUP_WIDTH)[:, :T]
    return u * mixed, vn


def merge_and_ffn(x, outs, p):
    B, T = x.shape[:2]
    o = jnp.concatenate([rmsnorm(o_i.reshape(B, T, GROUP_WIDTH), p['gnorm_g'][i])
                         for i, o_i in enumerate(outs)], axis=-1)
    h = x + o @ p['w_out']
    u = jax.nn.relu(rmsnorm(h, p['norm2_g']) @ p['w_ff1'])
    return h + jnp.square(u) @ p['w_ff2']


def layer_prompt(x, p):
    B, S = x.shape[:2]
    pos = jnp.arange(S)
    xn = rmsnorm(x, p['norm1_g'])
    qa, ka, va, logf, glu, qc, kvc, gc, ud, vd = project(xn, pos, p)
    o_a = fox_prompt(qa, ka, va, logf)
    o_b, conv_new = conformer_conv(glu, jnp.zeros((B, CONV_WIDTH - 1, C_CONV), glu.dtype), p)
    o_c = nsa_prompt(qc, gc, kvc, p)
    o_d, _ = gmlp_gate(ud, vd, p)
    y = merge_and_ffn(x, (o_a, o_b, o_c, o_d), p)
    wp = min(WINDOW, S)
    return y, (jnp.stack([ka, va], axis=2), logf, kvc[:, :, :4], kvc[:, -wp:, 4:], conv_new)


def layer_sample(x, p, past_k, past_v, past_logf, past_nsa, win_buf, conv_buf):
    B, T = x.shape[:2]
    P = past_k.shape[1]
    pos = P + jnp.arange(T)
    xn = rmsnorm(x, p['norm1_g'])
    qa, ka, va, logf, glu, qc, kvc, gc, ud, vd = project(xn, pos, p)
    o_a = fox_sample(qa, ka, va, logf, past_k.astype(ka.dtype), past_v.astype(va.dtype),
                     past_logf.astype(jnp.float32))
    o_b, conv_new = conformer_conv(glu, conv_buf.astype(glu.dtype), p)
    full = jnp.concatenate([past_nsa.astype(kvc.dtype), kvc[:, :, :4]], axis=1)
    win = jnp.concatenate([win_buf.astype(kvc.dtype), kvc[:, :, 4:]], axis=1)
    wb = win_buf.shape[1]
    kc, vc = nsa_compress(full, p)
    o_c = nsa_attend(qc, pos, gc, kc, vc, full[:, :, 2], full[:, :, 3], win[:, :, 0], win[:, :, 1],
                     P - wb + jnp.arange(wb + T))
    o_d, vn = gmlp_gate(ud, vd, p)
    y = merge_and_ffn(x, (o_a, o_b, o_c, o_d), p)
    new_win = win[:, -min(WINDOW, wb + T):]
    return y, (jnp.stack([ka, va], axis=2), logf, kvc[:, :, :4], new_win, conv_new, vn)


def setup_inputs(seed: int = 0) -> dict:
    key = jax.random.key(seed)
    ks = jax.random.split(key, 32)
    n_pages = PAST_LEN // PAGE_SIZE
    n_used = DEC_BATCH * n_pages
    n_pool = n_used + n_used // 4
    win_buf = min(WINDOW, PAST_LEN)
    nrm = jax.random.normal

    def gain(k, shape):
        return 1.0 + 0.02 * nrm(k, shape)

    page_table = jax.random.permutation(ks[7], n_pool)[:n_used].reshape(DEC_BATCH, n_pages).astype(jnp.int32)
    return {
        'x_prompt': nrm(ks[0], (BATCH, SEQ, D_MODEL)),
        'x_sample': nrm(ks[1], (DEC_BATCH, DEC_SEQ, D_MODEL)),
        'cache_fox_kv': nrm(ks[2], (DEPTH, n_pool, PAGE_SIZE, 2, N_HEADS_A, HEAD_DIM)),
        'cache_fox_logf': jax.nn.log_sigmoid(FORGET_BIAS + nrm(ks[3], (DEPTH, n_pool, PAGE_SIZE, N_HEADS_A))),
        'cache_nsa_kv': nrm(ks[4], (DEPTH, n_pool, PAGE_SIZE, 4, HEAD_DIM)),
        'state_nsa_win': nrm(ks[5], (DEPTH, DEC_BATCH, win_buf, 2, HEAD_DIM)),
        'state_conv': 0.5 * nrm(ks[6], (DEPTH, DEC_BATCH, CONV_WIDTH - 1, C_CONV)),
        'page_table': page_table,
        'norm1_g': gain(ks[8], (DEPTH, D_MODEL)),
        'w_in': nrm(ks[9], (DEPTH, D_MODEL, IN_WIDTH)) * D_MODEL ** -0.5,
        'fox_bf': 2.0 + 4.0 * jax.random.uniform(ks[10], (DEPTH, N_HEADS_A)),
        'fox_qn_g': gain(ks[11], (DEPTH, HEAD_DIM)),
        'fox_kn_g': gain(ks[12], (DEPTH, HEAD_DIM)),
        'conv_w': nrm(ks[13], (DEPTH, CONV_WIDTH, C_CONV)) * CONV_WIDTH ** -0.5,
        'conv_b': 0.02 * nrm(ks[14], (DEPTH, C_CONV)),
        'conv_ln_g': gain(ks[15], (DEPTH, C_CONV)),
        'conv_ln_b': 0.02 * nrm(ks[16], (DEPTH, C_CONV)),
        'nsa_qn_g': gain(ks[17], (DEPTH, HEAD_DIM)),
        'nsa_kn_g': gain(ks[18], (DEPTH, N_BRANCH, HEAD_DIM)),
        'nsa_pe': 0.1 * nrm(ks[19], (DEPTH, 2, CMP_BLOCK, HEAD_DIM)),
        'nsa_phi_k': nrm(ks[20], (DEPTH, CMP_BLOCK * HEAD_DIM, HEAD_DIM)) * (CMP_BLOCK * HEAD_DIM) ** -0.5,
        'nsa_phi_v': nrm(ks[21], (DEPTH, CMP_BLOCK * HEAD_DIM, HEAD_DIM)) * (CMP_BLOCK * HEAD_DIM) ** -0.5,
        'gmlp_ln_g': gain(ks[22], (DEPTH, GROUP_WIDTH)),
        'gmlp_ln_b': 0.02 * nrm(ks[23], (DEPTH, GROUP_WIDTH)),
        'gmlp_ws': nrm(ks[24], (DEPTH, N_GROUPS_D, CHUNK, CHUNK)) * CHUNK ** -0.5,
        'gmlp_bs': 1.0 + 0.1 * nrm(ks[25], (DEPTH, N_GROUPS_D, CHUNK)),
        'gnorm_g': gain(ks[26], (DEPTH, N_MIXERS, GROUP_WIDTH)),
        'w_out': nrm(ks[27], (DEPTH, D_MODEL, D_MODEL)) * D_MODEL ** -0.5,
        'norm2_g': gain(ks[28], (DEPTH, D_MODEL)),
        'w_ff1': nrm(ks[29], (DEPTH, D_MODEL, D_FF)) * D_MODEL ** -0.5,
        'w_ff2': nrm(ks[30], (DEPTH, D_FF, D_MODEL)) * D_FF ** -0.5,
    }


def reference(x_prompt, x_sample, cache_fox_kv, cache_fox_logf, cache_nsa_kv, state_nsa_win, state_conv,
              page_table, norm1_g, w_in, fox_bf, fox_qn_g, fox_kn_g, conv_w, conv_b, conv_ln_g, conv_ln_b,
              nsa_qn_g, nsa_kn_g, nsa_pe, nsa_phi_k, nsa_phi_v, gmlp_ln_g, gmlp_ln_b, gmlp_ws, gmlp_bs,
              gnorm_g, w_out, norm2_g, w_ff1, w_ff2):
    DB = x_sample.shape[0]
    P = page_table.shape[1] * cache_fox_kv.shape[2]
    xp, xs = x_prompt, x_sample
    states_p, states_s = [], []
    for l in range(DEPTH):
        p = dict(norm1_g=norm1_g[l], w_in=w_in[l], fox_bf=fox_bf[l], fox_qn_g=fox_qn_g[l],
                 fox_kn_g=fox_kn_g[l], conv_w=conv_w[l], conv_b=conv_b[l], conv_ln_g=conv_ln_g[l],
                 conv_ln_b=conv_ln_b[l], nsa_qn_g=nsa_qn_g[l], nsa_kn_g=nsa_kn_g[l], nsa_pe=nsa_pe[l],
                 nsa_phi_k=nsa_phi_k[l], nsa_phi_v=nsa_phi_v[l], gmlp_ln_g=gmlp_ln_g[l],
                 gmlp_ln_b=gmlp_ln_b[l], gmlp_ws=gmlp_ws[l], gmlp_bs=gmlp_bs[l], gnorm_g=gnorm_g[l],
                 w_out=w_out[l], norm2_g=norm2_g[l], w_ff1=w_ff1[l], w_ff2=w_ff2[l])
        xp, st_p = layer_prompt(xp, p)
        states_p.append(st_p)
        past_k = cache_fox_kv[l, page_table, :, 0].reshape(DB, P, N_HEADS_A, HEAD_DIM)
        past_v = cache_fox_kv[l, page_table, :, 1].reshape(DB, P, N_HEADS_A, HEAD_DIM)
        past_logf = cache_fox_logf[l, page_table].reshape(DB, P, N_HEADS_A)
        past_nsa = cache_nsa_kv[l, page_table].reshape(DB, P, 4, HEAD_DIM)
        xs, st_s = layer_sample(xs, p, past_k, past_v, past_logf, past_nsa, state_nsa_win[l], state_conv[l])
        states_s.append(st_s)
    fox_kv_p, fox_logf_p, nsa_kv_p, nsa_win_p, conv_p = [jnp.stack(a) for a in zip(*states_p)]
    fox_kv_s, fox_logf_s, nsa_kv_s, nsa_win_s, conv_s, gmlp_v_s = [jnp.stack(a) for a in zip(*states_s)]
    return (xp, xs, fox_kv_p, fox_kv_s, fox_logf_p, fox_logf_s, nsa_kv_p, nsa_kv_s,
            nsa_win_p, nsa_win_s, conv_p, conv_s, gmlp_v_s)
```

```python
import functools

import jax
import jax.numpy as jnp
import numpy as np
from jax import lax
from jax.experimental import pallas as pl
from jax.experimental.pallas import tpu as pltpu

F32 = jnp.float32
BF16 = jnp.bfloat16

HEAD_DIM = 64
HALF = HEAD_DIM // 2
GROUP_WIDTH = 256
N_HEADS = GROUP_WIDTH // HEAD_DIM
CONV_WIDTH = 31
CMP_BLOCK = 32
SEL_BLOCK = 64
N_SELECT = 16
WINDOW = 512
CHUNK = 128
ROPE_THETA = 10000.0
EPS = 1e-6
FORCED_SCORE = 1e4
Q_SCALE = HEAD_DIM ** -0.5
NEG_BIG = -1e30
SEL_NEG = -32768.0
LANES = 128
VMEM_LIMIT = 56 * 1024 * 1024


def _cparams(sem):
    return pltpu.CompilerParams(dimension_semantics=sem, vmem_limit_bytes=VMEM_LIMIT)


def _nt(a, b):
    return lax.dot_general(a, b, (((1,), (1,)), ((), ())), preferred_element_type=F32)


def _dot(a, b):
    return jnp.dot(a, b, preferred_element_type=F32)


def _split3(x):
    h = x.astype(BF16).astype(F32)
    r = x - h
    m = r.astype(BF16).astype(F32)
    l = (r - m).astype(BF16).astype(F32)
    return h, m, l


def _log_sigmoid(x):
    return jnp.minimum(x, 0.0) - jnp.log1p(jnp.exp(-jnp.abs(x)))


def _group_mean_sq(x, gsum):
    x2 = x * x
    hi = x2.astype(BF16)
    lo = (x2 - hi.astype(F32)).astype(BF16)
    return (_dot(hi, gsum) + _dot(lo, gsum)) * (1.0 / HEAD_DIM)


def _rope_rows(x, cos, sin_signed):
    lane = lax.broadcasted_iota(jnp.int32, x.shape, 1)
    first_half = (lane % HEAD_DIM) < HALF
    swapped = jnp.where(first_half, pltpu.roll(x, LANES - HALF, 1), pltpu.roll(x, HALF, 1))
    return x * cos + swapped * sin_signed


_R_QA, _R_GLU, _R_QC, _R_UD, _R_VD, _R_SMALL, _R_END = 0, 256, 768, 1024, 1280, 1536, 1664
_C_KA, _C_VA, _C_KVC, _C_FA, _C_END = 0, 256, 512, 896, 912


def _proj_body(x_ref, g1_ref, wrow_ref, wcol_ref, cosr_ref, sinr_ref, cost_ref, sint_ref,
               gsum_ref, gqa_ref, gka_ref, bf_ref, gqc_ref, gkc_ref, lng_ref, lnb_ref,
               wm_ref, bstab_ref, utri_ref,
               qa_ref, foxkv_ref, kaug_ref, vbf_ref, logf_ref, glu_ref, qc_ref, nsakv_ref,
               nsawin_ref, nsabf_ref, cmprow_ref, gc_ref, od_ref, vn_ref,
               carry_ref, *, chunk):
    tm = x_ref.shape[1]

    @pl.when(pl.program_id(1) == 0)
    def _():
        carry_ref[...] = jnp.zeros_like(carry_ref)

    x = x_ref[0]
    ms = jnp.mean(x * x, axis=-1, keepdims=True)
    xn = ((x * lax.rsqrt(ms + EPS)) * g1_ref[...]).astype(BF16)
    zr = _dot(xn, wrow_ref[...])
    zc = _nt(wcol_ref[...], xn)
    gsum = gsum_ref[...]

    qa = zr[:, _R_QA:_R_QA + 256]
    qa = qa * lax.rsqrt(_group_mean_sq(qa, gsum) + EPS) * gqa_ref[...] * Q_SCALE
    lane = lax.broadcasted_iota(jnp.int32, (tm, LANES), 1)
    for h in range(N_HEADS):
        src = qa[:, (h // 2) * LANES:(h // 2 + 1) * LANES]
        if h % 2 == 0:
            aug = jnp.where(lane < HEAD_DIM, src, jnp.where(lane < HEAD_DIM + 3, 1.0, 0.0))
        else:
            aug = jnp.where(lane >= HEAD_DIM, src, jnp.where(lane < 3, 1.0, 0.0))
        qa_ref[0, :, h * LANES:(h + 1) * LANES] = aug.astype(BF16)

    logf = _log_sigmoid(zc[_C_FA:_C_FA + 8] + bf_ref[...])
    logf_ref[0] = logf[0:N_HEADS]
    parts = _split3(logf)
    l3 = jnp.concatenate(parts, axis=0).astype(BF16)
    cs = _dot(l3, utri_ref[...])
    fcum = cs[0:8] + cs[8:16] + cs[16:24] + carry_ref[:, 0:1]
    carry_ref[...] = jnp.broadcast_to(fcum[:, tm - 1:tm], carry_ref.shape)
    nfh, nfm, nfl = _split3(-fcum)
    row8 = lax.broadcasted_iota(jnp.int32, (8, tm), 0)
    zeros56 = jnp.zeros((HEAD_DIM - 8, tm), F32)
    gka = gka_ref[:, 0:1]
    for h in range(N_HEADS):
        k = zc[_C_KA + h * HEAD_DIM:_C_KA + (h + 1) * HEAD_DIM]
        k = k * lax.rsqrt(jnp.mean(k * k, axis=0, keepdims=True) + EPS) * gka
        v = zc[_C_VA + h * HEAD_DIM:_C_VA + (h + 1) * HEAD_DIM]
        foxkv_ref[0, h * HEAD_DIM:(h + 1) * HEAD_DIM, :] = k
        foxkv_ref[0, GROUP_WIDTH + h * HEAD_DIM:GROUP_WIDTH + (h + 1) * HEAD_DIM, :] = v
        vbf_ref[0, h * HEAD_DIM:(h + 1) * HEAD_DIM, :] = v.astype(BF16)
        extra8 = jnp.where(row8 == 0, nfh[h:h + 1],
                           jnp.where(row8 == 1, nfm[h:h + 1],
                                     jnp.where(row8 == 2, nfl[h:h + 1], 0.0)))
        extra = jnp.concatenate([extra8, zeros56], axis=0)
        pieces = [k, extra] if h % 2 == 0 else [extra, k]
        kaug_ref[0, h] = jnp.concatenate(pieces, axis=0).astype(BF16)

    glu_in = zr[:, _R_GLU:_R_GLU + 512]
    glu_ref[0] = glu_in[:, :256] * jax.nn.sigmoid(glu_in[:, 256:])

    qc = zr[:, _R_QC:_R_QC + 256]
    qc = qc * lax.rsqrt(_group_mean_sq(qc, gsum) + EPS) * gqc_ref[...]
    cosr, sinr = cosr_ref[...], sinr_ref[...]
    for p in range(2):
        qh = _rope_rows(qc[:, p * LANES:(p + 1) * LANES], cosr, sinr) * Q_SCALE
        qc_ref[0, :, p * LANES:(p + 1) * LANES] = qh.astype(BF16)
    gc_ref[0] = jax.nn.sigmoid(zr[:, _R_SMALL:_R_SMALL + LANES])
    cost, sint = cost_ref[...], sint_ref[...]
    keys = []
    for b in range(3):
        kb = zc[_C_KVC + 2 * b * HEAD_DIM:_C_KVC + (2 * b + 1) * HEAD_DIM]
        kb = kb * lax.rsqrt(jnp.mean(kb * kb, axis=0, keepdims=True) + EPS) * gkc_ref[:, b:b + 1]
        x1, x2 = kb[:HALF], kb[HALF:]
        keys.append(jnp.concatenate([x1 * cost - x2 * sint, x2 * cost + x1 * sint], axis=0))
    vals = [zc[_C_KVC + (2 * b + 1) * HEAD_DIM:_C_KVC + (2 * b + 2) * HEAD_DIM] for b in range(3)]
    nsakv_ref[0, 0:64, :] = keys[0]
    nsakv_ref[0, 64:128, :] = vals[0]
    nsakv_ref[0, 128:192, :] = keys[1]
    nsakv_ref[0, 192:256, :] = vals[1]
    nsawin_ref[0, 0:64, :] = keys[2]
    nsawin_ref[0, 64:128, :] = vals[2]
    nsabf_ref[0, 0:64, :] = keys[1].astype(BF16)
    nsabf_ref[0, 64:128, :] = vals[1].astype(BF16)
    nsabf_ref[0, 128:192, :] = keys[2].astype(BF16)
    nsabf_ref[0, 192:256, :] = vals[2].astype(BF16)
    cmprow_ref[0] = jnp.transpose(jnp.concatenate([keys[0], vals[0]], axis=0))

    ud = zr[:, _R_UD:_R_UD + 256]
    vd = zr[:, _R_VD:_R_VD + 256]
    mu = jnp.mean(vd, axis=-1, keepdims=True)
    var = jnp.mean(jnp.square(vd - mu), axis=-1, keepdims=True)
    vn = (vd - mu) * lax.rsqrt(var + EPS) * lng_ref[...] + lnb_ref[...]
    vn_ref[0] = vn
    grp = lax.broadcasted_iota(jnp.int32, (chunk, GROUP_WIDTH), 1) // HEAD_DIM
    wm = wm_ref[...]
    for c in range(tm // chunk):
        r = _dot(wm, vn[c * chunk:(c + 1) * chunk].astype(BF16))
        mixed = bstab_ref[...]
        for g in range(N_HEADS):
            mixed = mixed + jnp.where(grp == g, r[g * chunk:(g + 1) * chunk], 0.0)
        od_ref[0, c * chunk:(c + 1) * chunk, :] = ud[c * chunk:(c + 1) * chunk] * mixed


def _proj(x, w, tabs, *, tm, chunk):
    B, S, D = x.shape
    ns = S // tm
    row = lambda width: pl.BlockSpec((1, tm, width), lambda b, i: (b, i, 0))
    col = lambda height: pl.BlockSpec((1, height, tm), lambda b, i: (b, 0, i))
    full = lambda a: pl.BlockSpec(a.shape, lambda b, i: (0,) * a.ndim)
    ins = [x, w['norm1_g'], w['w_row'], w['w_col'], tabs['cos_r'], tabs['sin_r'], tabs['cos_t'], tabs['sin_t'],
           tabs['gsum'], w['gqa'], w['gka'], w['bf'], w['gqc'], w['gkc'], w['gmlp_ln_g'], w['gmlp_ln_b'],
           tabs['wm'], tabs['bs_tab'], tabs['utri']]
    in_specs = [row(D), full(ins[1]), full(ins[2]), full(ins[3]),
                pl.BlockSpec((tm, LANES), lambda b, i: (i, 0)), pl.BlockSpec((tm, LANES), lambda b, i: (i, 0)),
                pl.BlockSpec((HALF, tm), lambda b, i: (0, i)), pl.BlockSpec((HALF, tm), lambda b, i: (0, i))]
    in_specs += [full(a) for a in ins[8:]]
    outs = dict(
        qa=(jax.ShapeDtypeStruct((B, S, 4 * LANES), BF16), row(4 * LANES)),
        foxkv=(jax.ShapeDtypeStruct((B, 2 * GROUP_WIDTH, S), F32), col(2 * GROUP_WIDTH)),
        kaug=(jax.ShapeDtypeStruct((B, N_HEADS, LANES, S), BF16),
              pl.BlockSpec((1, N_HEADS, LANES, tm), lambda b, i: (b, 0, 0, i))),
        vbf=(jax.ShapeDtypeStruct((B, GROUP_WIDTH, S), BF16), col(GROUP_WIDTH)),
        logf=(jax.ShapeDtypeStruct((B, N_HEADS, S), F32), col(N_HEADS)),
        glu=(jax.ShapeDtypeStruct((B, S, GROUP_WIDTH), F32), row(GROUP_WIDTH)),
        qc=(jax.ShapeDtypeStruct((B, S, GROUP_WIDTH), BF16), row(GROUP_WIDTH)),
        nsakv=(jax.ShapeDtypeStruct((B, 256, S), F32), col(256)),
        nsawin=(jax.ShapeDtypeStruct((B, 128, S), F32), col(128)),
        nsabf=(jax.ShapeDtypeStruct((B, 256, S), BF16), col(256)),
        cmprow=(jax.ShapeDtypeStruct((B, S, LANES), F32), row(LANES)),
        gc=(jax.ShapeDtypeStruct((B, S, LANES), F32), row(LANES)),
        od=(jax.ShapeDtypeStruct((B, S, GROUP_WIDTH), F32), row(GROUP_WIDTH)),
        vn=(jax.ShapeDtypeStruct((B, S, GROUP_WIDTH), F32), row(GROUP_WIDTH)),
    )
    names = list(outs)
    res = pl.pallas_call(
        functools.partial(_proj_body, chunk=chunk),
        grid=(B, ns),
        in_specs=in_specs,
        out_specs=[outs[n][1] for n in names],
        out_shape=[outs[n][0] for n in names],
        scratch_shapes=[pltpu.VMEM((8, LANES), F32)],
        compiler_params=_cparams(("arbitrary", "arbitrary")),
        name="proj",
    )(*ins)
    return dict(zip(names, res))


def _online_softmax_step(s, v, carry):
    m, l, acc = carry
    m_new = jnp.maximum(m, jnp.max(s, axis=-1, keepdims=True))
    alpha = jnp.exp(m - m_new)
    p = jnp.exp(s - m_new)
    l = alpha * l + jnp.sum(p, axis=-1, keepdims=True)
    acc = alpha * acc + _nt(p.astype(BF16), v)
    return m_new, l, acc


def _softmax_init(rows, dv):
    return (jnp.full((rows, 1), NEG_BIG, F32), jnp.zeros((rows, 1), F32), jnp.zeros((rows, dv), F32))


def _fox_attn_body(q_ref, k_ref, v_ref, o_ref):
    t = q_ref.shape[1]
    i = pl.program_id(1)
    row = lax.broadcasted_iota(jnp.int32, (t, t), 0)
    colm = lax.broadcasted_iota(jnp.int32, (t, t), 1)
    outs = []
    for h in range(N_HEADS):
        q = q_ref[0, :, h * LANES:(h + 1) * LANES]

        def tile(kt, carry, masked):
            start = pl.multiple_of(kt * t, t)
            s = _dot(q, k_ref[0, h, :, pl.ds(start, t)])
            if masked:
                s = jnp.where(colm <= row, s, NEG_BIG)
            v = v_ref[0, h * HEAD_DIM:(h + 1) * HEAD_DIM, pl.ds(start, t)]
            return _online_softmax_step(s, v, carry)

        carry = lax.fori_loop(0, i, functools.partial(tile, masked=False), _softmax_init(t, HEAD_DIM))
        m, l, acc = tile(i, carry, True)
        outs.append(acc / l)
    o_ref[0] = jnp.concatenate(outs, axis=-1)


def _fox_attn(qa, kaug, vbf, *, t):
    B, S, _ = qa.shape
    return pl.pallas_call(
        _fox_attn_body,
        grid=(B, S // t),
        in_specs=[pl.BlockSpec((1, t, 4 * LANES), lambda b, i: (b, i, 0)),
                  pl.BlockSpec((1, N_HEADS, LANES, S), lambda b, i: (b, 0, 0, 0)),
                  pl.BlockSpec((1, GROUP_WIDTH, S), lambda b, i: (b, 0, 0))],
        out_specs=pl.BlockSpec((1, t, GROUP_WIDTH), lambda b, i: (b, i, 0)),
        out_shape=jax.ShapeDtypeStruct((B, S, GROUP_WIDTH), F32),
        compiler_params=_cparams(("arbitrary", "arbitrary")),
        name="fox_attn",
    )(qa, kaug, vbf)


_HALO = 32


def _ln_silu(y, g, b):
    mu = jnp.mean(y, axis=-1, keepdims=True)
    var = jnp.mean(jnp.square(y - mu), axis=-1, keepdims=True)
    y = (y - mu) * lax.rsqrt(var + EPS) * g + b
    return y * jax.nn.sigmoid(y)


def _conv_prompt_body(cur_ref, halo_ref, w_ref, cb_ref, lng_ref, lnb_ref, o_ref, xin_ref):
    tm = cur_ref.shape[1]
    first = pl.program_id(1) == 0
    xin_ref[0:_HALO, :] = jnp.where(first, 0.0, halo_ref[0])
    xin_ref[_HALO:, :] = cur_ref[0]
    off = _HALO - (CONV_WIDTH - 1)
    acc = jnp.zeros((tm, GROUP_WIDTH), F32)
    for k in range(CONV_WIDTH):
        acc = acc + xin_ref[pl.ds(off + k, tm), :] * w_ref[k:k + 1, :]
    o_ref[0] = _ln_silu(acc + cb_ref[...], lng_ref[...], lnb_ref[...])


def _conv_prompt(glu, w, *, tm):
    B, S, C = glu.shape
    r = tm // _HALO
    full = lambda a: pl.BlockSpec(a.shape, lambda b, i: (0,) * a.ndim)
    ins = [glu, glu, w['conv_w'], w['conv_b'], w['conv_ln_g'], w['conv_ln_b']]
    return pl.pallas_call(
        _conv_prompt_body,
        grid=(B, S // tm),
        in_specs=[pl.BlockSpec((1, tm, C), lambda b, i: (b, i, 0)),
                  pl.BlockSpec((1, _HALO, C), lambda b, i: (b, jnp.maximum(i * r - 1, 0), 0))]
                 + [full(a) for a in ins[2:]],
        out_specs=pl.BlockSpec((1, tm, C), lambda b, i: (b, i, 0)),
        out_shape=jax.ShapeDtypeStruct((B, S, C), F32),
        scratch_shapes=[pltpu.VMEM((tm + _HALO, C), F32)],
        compiler_params=_cparams(("arbitrary", "arbitrary")),
        name="conv_prompt",
    )(*ins)


def _compress_body(x_ref, pe_ref, phi_ref, o_ref):
    o_ref[0] = _dot((x_ref[0] + pe_ref[...]).astype(BF16), phi_ref[...])


def _compress(blocks, pe_flat, phi):
    B, n, width = blocks.shape
    return pl.pallas_call(
        _compress_body,
        grid=(B,),
        in_specs=[pl.BlockSpec((1, n, width), lambda b: (b, 0, 0)),
                  pl.BlockSpec(pe_flat.shape, lambda b: (0, 0)),
                  pl.BlockSpec(phi.shape, lambda b: (0, 0))],
        out_specs=pl.BlockSpec((1, n, LANES), lambda b: (b, 0, 0)),
        out_shape=jax.ShapeDtypeStruct((B, n, LANES), F32),
        compiler_params=_cparams(("arbitrary",)),
        name="nsa_compress",
    )(blocks, pe_flat, phi)


def _masked_softmax(s, mask):
    s = jnp.where(mask, s, NEG_BIG)
    m = jnp.max(s, axis=-1, keepdims=True)
    e = jnp.where(mask, jnp.exp(s - m), 0.0)
    return e / jnp.maximum(jnp.sum(e, axis=-1, keepdims=True), 1e-30)


def _stack_heads(q):
    return jnp.concatenate([q[:, h * HEAD_DIM:(h + 1) * HEAD_DIM] for h in range(N_HEADS)], axis=0)


def _select_blocks(imp, cur, n_select):
    j = lax.broadcasted_iota(jnp.int32, imp.shape, 1)
    forced = (j == 0) | (j == cur) | (j == cur - 1)
    v = jnp.where(forced, FORCED_SCORE, imp)
    v = jnp.where(j <= cur, v, -1.0)
    sel = jnp.zeros(imp.shape, jnp.bool_)
    for _ in range(n_select):
        m = jnp.max(v, axis=-1, keepdims=True)
        idx = jnp.min(jnp.where(v == m, j, imp.shape[1]), axis=-1, keepdims=True)
        pick = j == idx
        sel = sel | (pick & (m >= 0.0))
        v = jnp.where(pick, -2.0, v)
    return sel


def _nsa_local_body(q_ref, g_ref, kc_ref, vc_ref, kv_ref, o_ref, sb_ref):
    t = q_ref.shape[1]
    nc = kc_ref.shape[2]
    half = nc // 2
    st = pl.program_id(1) * t
    qs = _stack_heads(q_ref[0])
    qpos = st + lax.broadcasted_iota(jnp.int32, (N_HEADS * t, 1), 0) % t

    c = lax.broadcasted_iota(jnp.int32, (1, nc), 1)
    blk = jnp.where(c < half, 2 * c, 2 * (c - half) + 1)
    p_cmp = _masked_softmax(_dot(qs, kc_ref[0]), (blk + 1) * CMP_BLOCK - 1 <= qpos)
    o_cmp = _dot(p_cmp.astype(BF16), vc_ref[0])
    imp = p_cmp[0:t] + p_cmp[t:2 * t] + p_cmp[2 * t:3 * t] + p_cmp[3 * t:4 * t]
    imp = imp[:, :half] + imp[:, half:]
    sel = _select_blocks(imp, qpos[0:t] // SEL_BLOCK, min(N_SELECT, half))
    sb_ref[0] = jnp.where(sel, 0.0, SEL_NEG).astype(BF16)

    span = WINDOW + t
    start = pl.multiple_of(jnp.maximum(st - WINDOW, 0), LANES)
    kwpos = start + lax.broadcasted_iota(jnp.int32, (1, span), 1)
    wmask = (kwpos <= qpos) & (qpos - kwpos < WINDOW)
    p_win = _masked_softmax(_dot(qs, kv_ref[0, 128:192, pl.ds(start, span)]), wmask)
    o_win = _nt(p_win.astype(BF16), kv_ref[0, 192:256, pl.ds(start, span)])

    g = g_ref[0]
    outs = [g[:, 3 * h:3 * h + 1] * o_cmp[h * t:(h + 1) * t] + g[:, 3 * h + 2:3 * h + 3] * o_win[h * t:(h + 1) * t]
            for h in range(N_HEADS)]
    o_ref[0] = jnp.concatenate(outs, axis=-1)


def _nsa_local(qc, gc, kc_t, vc, nsabf, *, t):
    B, S, _ = qc.shape
    nc = kc_t.shape[2]
    tile = lambda width: pl.BlockSpec((1, t, width), lambda b, i: (b, i, 0))
    return pl.pallas_call(
        _nsa_local_body,
        grid=(B, S // t),
        in_specs=[tile(GROUP_WIDTH), tile(LANES),
                  pl.BlockSpec((1, HEAD_DIM, nc), lambda b, i: (b, 0, 0)),
                  pl.BlockSpec((1, nc, HEAD_DIM), lambda b, i: (b, 0, 0)),
                  pl.BlockSpec((1, 256, S), lambda b, i: (b, 0, 0))],
        out_specs=[tile(GROUP_WIDTH), tile(nc // 2)],
        out_shape=[jax.ShapeDtypeStruct((B, S, GROUP_WIDTH), F32),
                   jax.ShapeDtypeStruct((B, S, nc // 2), BF16)],
        compiler_params=_cparams(("arbitrary", "arbitrary")),
        name="nsa_local",
    )(qc, gc, kc_t, vc, nsabf)


def _nsa_sel_body(q_ref, sb_ref, g_ref, part_ref, kv_ref, e_ref, o_ref, *, tk):
    t = q_ref.shape[1]
    i = pl.program_id(1)
    st = i * t
    q2 = jnp.concatenate([_stack_heads(q_ref[0]), jnp.concatenate([sb_ref[0]] * N_HEADS, axis=0)], axis=-1)
    qpos = st + lax.broadcasted_iota(jnp.int32, (N_HEADS * t, 1), 0) % t

    def tile(kt, carry, masked):
        start = pl.multiple_of(kt * tk, tk)
        k2 = jnp.concatenate([kv_ref[0, 0:64, pl.ds(start, tk)], e_ref[:, pl.ds(start, tk)]], axis=0)
        s = _dot(q2, k2)
        if masked:
            kpos = start + lax.broadcasted_iota(jnp.int32, (1, tk), 1)
            s = jnp.where(kpos <= qpos, s, NEG_BIG)
        return _online_softmax_step(s, kv_ref[0, 64:128, pl.ds(start, tk)], carry)

    last = (st + t - 1) // tk
    carry = lax.fori_loop(0, last, functools.partial(tile, masked=False), _softmax_init(N_HEADS * t, HEAD_DIM))
    m, l, acc = tile(last, carry, True)
    o_sel = acc / l
    g = g_ref[0]
    outs = [g[:, 3 * h + 1:3 * h + 2] * o_sel[h * t:(h + 1) * t] for h in range(N_HEADS)]
    o_ref[0] = part_ref[0] + jnp.concatenate(outs, axis=-1)


def _nsa_sel(qc, selbias, gc, part, nsabf, onehot, *, t, tk):
    B, S, _ = qc.shape
    ns = selbias.shape[2]
    tile = lambda width: pl.BlockSpec((1, t, width), lambda b, i: (b, i, 0))
    return pl.pallas_call(
        functools.partial(_nsa_sel_body, tk=tk),
        grid=(B, S // t),
        in_specs=[tile(GROUP_WIDTH), tile(ns), tile(LANES), tile(GROUP_WIDTH),
                  pl.BlockSpec((1, 256, S), lambda b, i: (b, 0, 0)),
                  pl.BlockSpec((ns, S), lambda b, i: (0, 0))],
        out_specs=tile(GROUP_WIDTH),
        out_shape=jax.ShapeDtypeStruct((B, S, GROUP_WIDTH), F32),
        compiler_params=_cparams(("arbitrary", "arbitrary")),
        name="nsa_sel",
    )(qc, selbias, gc, part, nsabf, onehot)


def _nsa_prompt(o, w, *, t, tk):
    B, S, _ = o['qc'].shape
    n = S // CMP_BLOCK
    cmp = _compress(o['cmprow'].reshape(B, n, CMP_BLOCK * LANES), w['pe_flat'], w['phi'])
    kc_t, vc = _even_odd(cmp)
    part, selbias = _nsa_local(o['qc'], o['gc'], kc_t, vc, o['nsabf'], t=t)
    ns = n // 2
    onehot = jnp.asarray(np.arange(ns)[:, None] == (np.arange(S) // SEL_BLOCK)[None, :], BF16)
    return _nsa_sel(o['qc'], selbias, o['gc'], part, o['nsabf'], onehot, t=t, tk=tk)


def _merge_ffn_body(x_ref, oa_ref, ob_ref, oc_ref, od_ref, gn_ref, wout_ref, g2_ref, w1_ref, w2_ref,
                    y_ref, hn_ref, acc_ref):
    j = pl.program_id(1)

    @pl.when(j == 0)
    def _():
        h = x_ref[...]
        for i, o_ref in enumerate((oa_ref, ob_ref, oc_ref, od_ref)):
            o = o_ref[...]
            o = o * lax.rsqrt(jnp.mean(o * o, axis=-1, keepdims=True) + EPS) * gn_ref[i:i + 1, :]
            h = h + _dot(o.astype(BF16), wout_ref[i * GROUP_WIDTH:(i + 1) * GROUP_WIDTH, :])
        acc_ref[...] = h
        hn = h * lax.rsqrt(jnp.mean(h * h, axis=-1, keepdims=True) + EPS) * g2_ref[...]
        hn_ref[...] = hn.astype(BF16)

    u = jnp.maximum(_dot(hn_ref[...], w1_ref[...]), 0.0)
    acc_ref[...] += _dot((u * u).astype(BF16), w2_ref[...])

    @pl.when(j == pl.num_programs(1) - 1)
    def _():
        y_ref[...] = acc_ref[...]


def _merge_ffn(x, outs, w, *, tm, tf):
    R, D = x.shape
    F = w['w_ff1'].shape[1]
    rows = lambda width: pl.BlockSpec((tm, width), lambda i, j: (i, 0))
    full = lambda a: pl.BlockSpec(a.shape, lambda i, j: (0,) * a.ndim)
    return pl.pallas_call(
        _merge_ffn_body,
        grid=(R // tm, F // tf),
        in_specs=[rows(D)] + [rows(GROUP_WIDTH)] * 4 + [full(w['gnorm_g']), full(w['w_out']), full(w['norm2_g']),
                  pl.BlockSpec((D, tf), lambda i, j: (0, j)), pl.BlockSpec((tf, D), lambda i, j: (j, 0))],
        out_specs=rows(D),
        out_shape=jax.ShapeDtypeStruct((R, D), F32),
        scratch_shapes=[pltpu.VMEM((tm, D), BF16), pltpu.VMEM((tm, D), F32)],
        compiler_params=_cparams(("arbitrary", "arbitrary")),
        name="merge_ffn",
    )(x, *outs, w['gnorm_g'], w['w_out'], w['norm2_g'], w['w_ff1'], w['w_ff2'])


def _page_copies(layer, pt_ref, b, first_page, n, slot, srcs, bufs, sems):
    out = []
    for g in range(n):
        pid = pt_ref[b, first_page + g]
        for k, (src, buf) in enumerate(zip(srcs, bufs)):
            out.append(pltpu.make_async_copy(src.at[layer, pid], buf.at[slot, g], sems.at[k, slot]))
    return out


def _suffix_sums(x):
    lane = lax.broadcasted_iota(jnp.int32, x.shape, 1)
    y = x
    sh = 1
    while sh < LANES:
        y = y + jnp.where(lane + sh < LANES, pltpu.roll(y, LANES - sh, 1), 0.0)
        sh *= 2
    return y


def _same_seq_causal(b, t_new, n_cols, rows):
    col = lax.broadcasted_iota(jnp.int32, (rows, n_cols), 1)
    t = lax.broadcasted_iota(jnp.int32, (rows, n_cols), 0) % t_new
    return (col // t_new == b) & (col % t_new <= t)


def _fox_sample_body(pt_ref, q_ref, knew_ref, vnew_ref, kv_hbm, lf_hbm, o_ref,
                     kvbuf, lfbuf, sems, m_ref, l_ref, acc_ref, carry_ref, *, layer, pg):
    b, j = pl.program_id(0), pl.program_id(1)
    nb, nch = pl.num_programs(0), pl.num_programs(1)
    step = b * nch + j
    slot = step % 2
    t_new = q_ref.shape[0]
    rows = N_HEADS * t_new

    def copies(bb, jj, sl):
        return _page_copies(layer, pt_ref, bb, (nch - 1 - jj) * pg, pg, sl, (kv_hbm, lf_hbm), (kvbuf, lfbuf), sems)

    @pl.when(step == 0)
    def _():
        for c in copies(b, j, slot):
            c.start()

    @pl.when(step + 1 < nb * nch)
    def _():
        nxt = step + 1
        for c in copies(nxt // nch, nxt % nch, 1 - slot):
            c.start()

    @pl.when(j == 0)
    def _():
        m_ref[...] = jnp.full_like(m_ref, NEG_BIG)
        l_ref[...] = jnp.zeros_like(l_ref)
        acc_ref[...] = jnp.zeros_like(acc_ref)
        carry_ref[...] = jnp.zeros_like(carry_ref)

    qv = q_ref[...]
    lane = lax.broadcasted_iota(jnp.int32, (t_new, LANES), 1)
    q4 = jnp.concatenate([jnp.where(lane < HEAD_DIM, qv[:, 0:128], qv[:, 128:256]),
                          jnp.where(lane < HEAD_DIM, qv[:, 256:384], qv[:, 384:512])], axis=-1)
    grp = lax.broadcasted_iota(jnp.int32, (t_new, GROUP_WIDTH), 1) // HEAD_DIM
    wq = jnp.concatenate([jnp.where(grp == h, q4, 0.0) for h in range(N_HEADS)], axis=0).astype(BF16)

    for c in copies(b, j, slot):
        c.wait()

    def page(i, carry):
        m, l, acc, tail = carry
        g = pg - 1 - i
        lf = lfbuf[slot, g]
        incl = _suffix_sums(lf)
        suf = incl - lf + tail
        tail = tail + incl[:, 0:1]
        bias = jnp.concatenate([jnp.broadcast_to(suf[h:h + 1], (t_new, LANES)) for h in range(N_HEADS)], axis=0)
        s = _dot(wq, kvbuf[slot, g, 0:GROUP_WIDTH, :].astype(BF16)) + bias
        m, l, acc = _online_softmax_step(s, kvbuf[slot, g, GROUP_WIDTH:, :].astype(BF16), (m, l, acc))
        return m, l, acc, tail

    m, l, acc, tail = lax.fori_loop(0, pg, page, (m_ref[...], l_ref[...], acc_ref[...], carry_ref[:, 0:1]))
    m_ref[...], l_ref[...], acc_ref[...] = m, l, acc
    carry_ref[...] = jnp.broadcast_to(tail, carry_ref.shape)

    @pl.when(j == nch - 1)
    def _():
        n_cols = knew_ref.shape[2]
        s = jnp.concatenate([_dot(qv[:, h * LANES:(h + 1) * LANES].astype(BF16), knew_ref[h])
                             for h in range(N_HEADS)], axis=0)
        s = jnp.where(_same_seq_causal(b, t_new, n_cols, rows), s, NEG_BIG)
        m2, l2, acc2 = _online_softmax_step(s, vnew_ref[...], (m, l, acc))
        o = acc2 / l2
        o_ref[...] = jnp.concatenate(
            [o[h * t_new:(h + 1) * t_new, h * HEAD_DIM:(h + 1) * HEAD_DIM] for h in range(N_HEADS)], axis=-1)


def _fox_sample(layer, page_table, q_aug, kaug_new, v_new, cache_kv, cache_lf, *, t_new, pg):
    db, n_pages = page_table.shape
    rows = N_HEADS * t_new
    grid_spec = pltpu.PrefetchScalarGridSpec(
        num_scalar_prefetch=1,
        grid=(db, n_pages // pg),
        in_specs=[pl.BlockSpec((t_new, 4 * LANES), lambda b, j, pt: (b, 0)),
                  pl.BlockSpec(kaug_new.shape, lambda b, j, pt: (0, 0, 0)),
                  pl.BlockSpec(v_new.shape, lambda b, j, pt: (0, 0)),
                  pl.BlockSpec(memory_space=pl.ANY), pl.BlockSpec(memory_space=pl.ANY)],
        out_specs=pl.BlockSpec((t_new, GROUP_WIDTH), lambda b, j, pt: (b, 0)),
        scratch_shapes=[pltpu.VMEM((2, pg) + cache_kv.shape[2:], F32), pltpu.VMEM((2, pg) + cache_lf.shape[2:], F32),
                        pltpu.SemaphoreType.DMA((2, 2)),
                        pltpu.VMEM((rows, 1), F32), pltpu.VMEM((rows, 1), F32), pltpu.VMEM((rows, GROUP_WIDTH), F32),
                        pltpu.VMEM((N_HEADS, LANES), F32)])
    return pl.pallas_call(
        functools.partial(_fox_sample_body, layer=layer, pg=pg),
        grid_spec=grid_spec,
        out_shape=jax.ShapeDtypeStruct((db * t_new, GROUP_WIDTH), F32),
        compiler_params=_cparams(("arbitrary", "arbitrary")),
        name="fox_sample",
    )(page_table, q_aug, kaug_new, v_new, cache_kv, cache_lf)


def _conv_sample_body(state_ref, glu_ref, w_ref, cb_ref, lng_ref, lnb_ref, o_ref, new_ref):
    n_state, t_new = state_ref.shape[1], glu_ref.shape[0]
    x = lambda i: state_ref[0, i] if i < n_state else glu_ref[i - n_state]
    for t in range(t_new):
        acc = x(t) * w_ref[0:1, :]
        for k in range(1, CONV_WIDTH):
            acc = acc + x(t + k) * w_ref[k:k + 1, :]
        o_ref[t] = _ln_silu(acc + cb_ref[...], lng_ref[...], lnb_ref[...])
    for i in range(n_state):
        new_ref[i] = x(i + t_new)


def _conv_sample(layer, state_t, glu_t, w):
    _, n_state, db, c = state_t.shape
    t_new = glu_t.shape[0]
    full = lambda a: pl.BlockSpec(a.shape, lambda i: (0,) * a.ndim)
    ins = [state_t, glu_t, w['conv_w'], w['conv_b'], w['conv_ln_g'], w['conv_ln_b']]
    return pl.pallas_call(
        _conv_sample_body,
        grid=(1,),
        in_specs=[pl.BlockSpec((1, n_state, db, c), lambda i: (layer, 0, 0, 0))] + [full(a) for a in ins[1:]],
        out_specs=[pl.BlockSpec((t_new, db, c), lambda i: (0, 0, 0)), pl.BlockSpec((n_state, db, c), lambda i: (0, 0, 0))],
        out_shape=[jax.ShapeDtypeStruct((t_new, db, c), F32), jax.ShapeDtypeStruct((n_state, db, c), F32)],
        compiler_params=_cparams(("arbitrary",)),
        name="conv_sample",
    )(*ins)


_CMP_PER_PAGE = 4


def _compress_sample_body(pt_ref, *refs, n_group):
    x_refs, (pe_ref, phi_ref, o_ref, x_scr) = refs[:n_group], refs[n_group:]
    for g, x_ref in enumerate(x_refs):
        x_scr[g * _CMP_PER_PAGE:(g + 1) * _CMP_PER_PAGE, :] = x_ref[0, 0] + pe_ref[...]
    o_ref[0] = _dot(x_scr[...].astype(BF16), phi_ref[...])


def _compress_sample(layer, page_table, blocks, pe_flat, phi, *, n_group):
    db, n_pages = page_table.shape
    width = blocks.shape[3]
    page_spec = lambda g: pl.BlockSpec((1, 1, _CMP_PER_PAGE, width),
                                       lambda b, j, pt: (layer, pt[b, j * n_group + g], 0, 0))
    rows = n_group * _CMP_PER_PAGE
    grid_spec = pltpu.PrefetchScalarGridSpec(
        num_scalar_prefetch=1,
        grid=(db, n_pages // n_group),
        in_specs=[page_spec(g) for g in range(n_group)]
                 + [pl.BlockSpec(pe_flat.shape, lambda b, j, pt: (0, 0)), pl.BlockSpec(phi.shape, lambda b, j, pt: (0, 0))],
        out_specs=pl.BlockSpec((1, rows, LANES), lambda b, j, pt: (b, j, 0)),
        scratch_shapes=[pltpu.VMEM((rows, width), F32)])
    return pl.pallas_call(
        functools.partial(_compress_sample_body, n_group=n_group),
        grid_spec=grid_spec,
        out_shape=jax.ShapeDtypeStruct((db, n_pages * _CMP_PER_PAGE, LANES), F32),
        compiler_params=_cparams(("arbitrary", "arbitrary")),
        name="nsa_compress_sample",
    )(page_table, *([blocks] * n_group), pe_flat, phi)


def _nsa_local_sample_body(q_ref, g_ref, kc_ref, vc_ref, win_ref, new_ref, o_ref, flag_ref, *, past_len):
    b = pl.program_id(0)
    t_new = q_ref.shape[0]
    rows = N_HEADS * t_new
    nc = kc_ref.shape[2]
    half = nc // 2
    qs = _stack_heads(q_ref[...]).astype(BF16)
    tq = lax.broadcasted_iota(jnp.int32, (rows, 1), 0) % t_new
    qpos = past_len + tq

    c = lax.broadcasted_iota(jnp.int32, (1, nc), 1)
    blk = jnp.where(c < half, 2 * c, 2 * (c - half) + 1)
    p_cmp = _masked_softmax(_dot(qs, kc_ref[0]), (blk + 1) * CMP_BLOCK - 1 <= qpos)
    o_cmp = _dot(p_cmp.astype(BF16), vc_ref[0])
    imp = p_cmp[0:t_new]
    for h in range(1, N_HEADS):
        imp = imp + p_cmp[h * t_new:(h + 1) * t_new]
    imp = imp[:, :half] + imp[:, half:]
    cur = jnp.full((t_new, 1), past_len // SEL_BLOCK, jnp.int32)
    sel = _select_blocks(imp, cur, min(N_SELECT, half + 1) - 1)
    j = lax.broadcasted_iota(jnp.int32, sel.shape, 1)
    t = lax.broadcasted_iota(jnp.int32, sel.shape, 0)
    weight = jnp.left_shift(1, 2 * t + j % 2).astype(F32)
    colsum = jnp.sum(jnp.where(sel, weight, 0.0), axis=0, keepdims=True)
    flag_ref[0] = colsum + pltpu.roll(colsum, half - 1, 1)

    w = win_ref.shape[3]
    n_cols = new_ref.shape[1]
    s_old = _dot(qs, win_ref[0, 0, 0:HEAD_DIM, :].astype(BF16))
    s_new = _dot(qs, new_ref[128:192, :])
    i_old = lax.broadcasted_iota(jnp.int32, (1, w), 1)
    mask = jnp.concatenate([jnp.broadcast_to(i_old + (WINDOW - w) > tq, (rows, w)),
                            _same_seq_causal(b, t_new, n_cols, rows)], axis=-1)
    p_win = _masked_softmax(jnp.concatenate([s_old, s_new], axis=-1), mask).astype(BF16)
    o_win = _nt(p_win[:, :w], win_ref[0, 0, HEAD_DIM:, :].astype(BF16)) + _nt(p_win[:, w:], new_ref[192:256, :])

    g = g_ref[...]
    outs = [g[:, 3 * h:3 * h + 1] * o_cmp[h * t_new:(h + 1) * t_new]
            + g[:, 3 * h + 2:3 * h + 3] * o_win[h * t_new:(h + 1) * t_new] for h in range(N_HEADS)]
    o_ref[...] = jnp.concatenate(outs, axis=-1)


def _nsa_local_sample(layer, qc, gc, kc_t, vc, win_t, new_bf, *, t_new, past_len):
    db = kc_t.shape[0]
    nc = kc_t.shape[2]
    w = win_t.shape[3]
    rows = lambda width: pl.BlockSpec((t_new, width), lambda b: (b, 0))
    return pl.pallas_call(
        functools.partial(_nsa_local_sample_body, past_len=past_len),
        grid=(db,),
        in_specs=[rows(GROUP_WIDTH), rows(LANES),
                  pl.BlockSpec((1, HEAD_DIM, nc), lambda b: (b, 0, 0)),
                  pl.BlockSpec((1, nc, HEAD_DIM), lambda b: (b, 0, 0)),
                  pl.BlockSpec((1, 1, 2 * HEAD_DIM, w), lambda b: (layer, b, 0, 0)),
                  pl.BlockSpec(new_bf.shape, lambda b: (0, 0))],
        out_specs=[rows(GROUP_WIDTH), pl.BlockSpec((1, 1, nc // 2), lambda b: (b, 0, 0))],
        out_shape=[jax.ShapeDtypeStruct((db * t_new, GROUP_WIDTH), F32),
                   jax.ShapeDtypeStruct((db, 1, nc // 2), F32)],
        compiler_params=_cparams(("arbitrary",)),
        name="nsa_local_sample",
    )(qc, gc, kc_t, vc, win_t, new_bf)


def _nsa_sel_sample_body(pt_ref, fl_ref, q_ref, g_ref, part_ref, new_ref, kv_hbm, o_ref,
                         buf, sems, m_ref, l_ref, acc_ref, *, layer):
    b = pl.program_id(0)
    nb = pl.num_programs(0)
    n_pages = fl_ref.shape[1]
    slot = b % 2
    t_new = q_ref.shape[0]
    rows = N_HEADS * t_new

    def copy(bb, p, sl):
        src = kv_hbm.at[layer, pt_ref[bb, p], pl.ds(2 * HEAD_DIM, 2 * HEAD_DIM)]
        return pltpu.make_async_copy(src, buf.at[sl, p], sems.at[sl])

    def start_all(bb, sl):
        def body(p, c):
            @pl.when(fl_ref[bb, p] != 0)
            def _():
                copy(bb, p, sl).start()
            return c
        lax.fori_loop(0, n_pages, body, 0)

    @pl.when(b == 0)
    def _():
        start_all(b, slot)

    @pl.when(b + 1 < nb)
    def _():
        start_all(b + 1, 1 - slot)

    m_ref[...] = jnp.full_like(m_ref, NEG_BIG)
    l_ref[...] = jnp.zeros_like(l_ref)
    acc_ref[...] = jnp.zeros_like(acc_ref)
    qs = _stack_heads(q_ref[...]).astype(BF16)
    shamt = (2 * (lax.broadcasted_iota(jnp.int32, (rows, LANES), 0) % t_new)
             + lax.broadcasted_iota(jnp.int32, (rows, LANES), 1) // SEL_BLOCK)

    def wait_body(p, c):
        @pl.when(fl_ref[b, p] != 0)
        def _():
            copy(b, p, slot).wait()
        return c
    lax.fori_loop(0, n_pages, wait_body, 0)

    def page(p, c):
        flag = fl_ref[b, p]

        @pl.when(flag != 0)
        def _():
            picked = (jnp.right_shift(jnp.full((rows, LANES), flag, jnp.int32), shamt) & 1) == 1
            s = jnp.where(picked, _dot(qs, buf[slot, p, 0:HEAD_DIM, :].astype(BF16)), NEG_BIG)
            m, l, acc = _online_softmax_step(s, buf[slot, p, HEAD_DIM:, :].astype(BF16),
                                             (m_ref[...], l_ref[...], acc_ref[...]))
            m_ref[...], l_ref[...], acc_ref[...] = m, l, acc
        return c
    lax.fori_loop(0, n_pages, page, 0)

    s = jnp.where(_same_seq_causal(b, t_new, new_ref.shape[1], rows), _dot(qs, new_ref[0:64, :]), NEG_BIG)
    m, l, acc = _online_softmax_step(s, new_ref[64:128, :], (m_ref[...], l_ref[...], acc_ref[...]))
    o_sel = acc / l
    g = g_ref[...]
    outs = [g[:, 3 * h + 1:3 * h + 2] * o_sel[h * t_new:(h + 1) * t_new] for h in range(N_HEADS)]
    o_ref[...] = part_ref[...] + jnp.concatenate(outs, axis=-1)


def _nsa_sel_sample(layer, page_table, flags, qc, gc, part, new_bf, cache_nsa, *, t_new):
    db, n_pages = page_table.shape
    rows = N_HEADS * t_new
    tile = lambda width: pl.BlockSpec((t_new, width), lambda b, pt, fl: (b, 0))
    grid_spec = pltpu.PrefetchScalarGridSpec(
        num_scalar_prefetch=2,
        grid=(db,),
        in_specs=[tile(GROUP_WIDTH), tile(LANES), tile(GROUP_WIDTH),
                  pl.BlockSpec(new_bf.shape, lambda b, pt, fl: (0, 0)),
                  pl.BlockSpec(memory_space=pl.ANY)],
        out_specs=tile(GROUP_WIDTH),
        scratch_shapes=[pltpu.VMEM((2, n_pages, 2 * HEAD_DIM, LANES), F32), pltpu.SemaphoreType.DMA((2,)),
                        pltpu.VMEM((rows, 1), F32), pltpu.VMEM((rows, 1), F32), pltpu.VMEM((rows, HEAD_DIM), F32)])
    return pl.pallas_call(
        functools.partial(_nsa_sel_sample_body, layer=layer),
        grid_spec=grid_spec,
        out_shape=jax.ShapeDtypeStruct((db * t_new, GROUP_WIDTH), F32),
        compiler_params=_cparams(("arbitrary",)),
        name="nsa_sel_sample",
    )(page_table, flags, qc, gc, part, new_bf, cache_nsa)


_SPLITS = np.cumsum([0, 256, 256, 256, 4, 512, 256, 384, 12, 256, 256])


def _prep_layer(l, P):
    w_in = P['w_in'][l]
    sec = [w_in[:, _SPLITS[i]:_SPLITS[i + 1]] for i in range(10)]
    qa, ka, va, fa, glu, qc, kvc, gc, ud, vd = sec
    w_row = jnp.concatenate([qa, glu, qc, ud, vd, jnp.pad(gc, ((0, 0), (0, LANES - 12)))], axis=1).astype(BF16)
    w_col = jnp.concatenate([ka, va, kvc, jnp.pad(fa, ((0, 0), (0, 12)))], axis=1).T.astype(BF16)
    row = lambda v: v.reshape(1, -1)
    phi_k = P['nsa_phi_k'][l].reshape(CMP_BLOCK, 1, HEAD_DIM, HEAD_DIM)
    phi_v = P['nsa_phi_v'][l].reshape(CMP_BLOCK, 1, HEAD_DIM, HEAD_DIM)
    zero = jnp.zeros_like(phi_k)
    phi = jnp.concatenate([jnp.concatenate([phi_k, zero], axis=-1), jnp.concatenate([zero, phi_v], axis=-1)], axis=1)
    return dict(
        pe_flat=jnp.transpose(P['nsa_pe'][l], (1, 0, 2)).reshape(1, CMP_BLOCK * LANES),
        phi=phi.reshape(CMP_BLOCK * LANES, LANES).astype(BF16),
        norm1_g=row(P['norm1_g'][l]), w_row=w_row, w_col=w_col,
        gqa=row(jnp.tile(P['fox_qn_g'][l], N_HEADS)), gka=P['fox_kn_g'][l].reshape(HEAD_DIM, 1),
        bf=jnp.pad(P['fox_bf'][l], (0, 4)).reshape(8, 1),
        gqc=row(jnp.tile(P['nsa_qn_g'][l], N_HEADS)), gkc=P['nsa_kn_g'][l].T,
        gmlp_ln_g=row(P['gmlp_ln_g'][l]), gmlp_ln_b=row(P['gmlp_ln_b'][l]),
        gmlp_ws=P['gmlp_ws'][l], gmlp_bs=P['gmlp_bs'][l],
        gnorm_g=P['gnorm_g'][l], w_out=P['w_out'][l].astype(BF16), norm2_g=row(P['norm2_g'][l]),
        w_ff1=P['w_ff1'][l].astype(BF16), w_ff2=P['w_ff2'][l].astype(BF16),
        conv_w=P['conv_w'][l], conv_b=row(P['conv_b'][l]),
        conv_ln_g=row(P['conv_ln_g'][l]), conv_ln_b=row(P['conv_ln_b'][l]),
    )


def _rope_tables(pos):
    inv = ROPE_THETA ** (-jnp.arange(HALF, dtype=F32) / HALF)
    ang = pos.astype(F32)[:, None] * inv
    cos, sin = jnp.cos(ang), jnp.sin(ang)
    return dict(cos_r=jnp.tile(cos, (1, 4)), sin_r=jnp.tile(jnp.concatenate([-sin, sin], axis=1), (1, 2)),
                cos_t=cos.T, sin_t=sin.T)


def _const_tables():
    g = np.arange(GROUP_WIDTH) // HEAD_DIM
    return dict(gsum=jnp.asarray(g[:, None] == g[None, :], BF16))


def _gmlp_tables(w, seq_len, n_seq):
    t = min(seq_len, CHUNK)
    wm = (w['gmlp_ws'] * jnp.tril(jnp.ones((CHUNK, CHUNK), F32)))[:, :t, :t]
    bs = w['gmlp_bs'][:, :t]
    if seq_len < CHUNK:
        eye = jnp.eye(n_seq, dtype=F32)
        wm = jnp.einsum('ab,gts->gatbs', eye, wm).reshape(N_HEADS, n_seq * t, n_seq * t)
        bs = jnp.tile(bs, (1, n_seq))
    c = wm.shape[1]
    return dict(wm=wm.reshape(N_HEADS * c, c).astype(BF16), bs_tab=jnp.repeat(bs.T, HEAD_DIM, axis=1))


def _layer_prompt(x, w, consts, *, tm, ta, tk, tf):
    B, S, D = x.shape
    tabs = dict(consts, **_gmlp_tables(w, S, B))
    o = _proj(x, w, tabs, tm=tm, chunk=CHUNK)
    o_a = _fox_attn(o['qa'], o['kaug'], o['vbf'], t=ta)
    o_b = _conv_prompt(o['glu'], w, tm=tm)
    o_c = _nsa_prompt(o, w, t=LANES, tk=tk)
    flat = lambda a: a.reshape(B * S, a.shape[-1])
    y = _merge_ffn(flat(x), [flat(o_a), flat(o_b), flat(o_c), flat(o['od'])], w, tm=tm, tf=tf)
    wp = min(WINDOW, S)
    states = dict(fox_kv=o['foxkv'], fox_logf=o['logf'], nsa_kv=o['nsakv'],
                  nsa_win=o['nsawin'][:, :, S - wp:], conv=o['glu'][:, S - (CONV_WIDTH - 1):])
    return y.reshape(B, S, D), states


def _even_odd(cmp):
    n = cmp.shape[1]
    order = np.concatenate([np.arange(0, n, 2), np.arange(1, n, 2)])
    cmp = cmp[:, order].astype(BF16)
    return jnp.swapaxes(cmp[:, :, :HEAD_DIM], 1, 2), cmp[:, :, HEAD_DIM:]


def _prep_caches(cache_fox_kv, cache_fox_logf, cache_nsa_kv, state_nsa_win, state_conv):
    L, pool, page = cache_fox_kv.shape[:3]
    db = state_nsa_win.shape[1]
    return dict(
        fox_kv=jnp.transpose(cache_fox_kv, (0, 1, 3, 4, 5, 2)).reshape(L, pool, 2 * GROUP_WIDTH, page),
        fox_lf=jnp.transpose(cache_fox_logf, (0, 1, 3, 2)),
        nsa_kv=jnp.transpose(cache_nsa_kv, (0, 1, 3, 4, 2)).reshape(L, pool, 4 * HEAD_DIM, page),
        nsa_blocks=cache_nsa_kv[:, :, :, 0:2, :].reshape(L, pool, page // CMP_BLOCK, CMP_BLOCK * LANES),
        win=jnp.transpose(state_nsa_win, (0, 1, 3, 4, 2)).reshape(L, db, 2 * HEAD_DIM, -1),
        conv=jnp.transpose(state_conv, (0, 2, 1, 3)),
    )


def _layer_sample(l, xs, w, caches, consts, page_table, *, past_len, pg, tf):
    db, t_new, D = xs.shape
    R = db * t_new
    tabs = dict(consts, **_gmlp_tables(w, t_new, db))
    o = {k: v[0] for k, v in _proj(xs.reshape(1, R, D), w, tabs, tm=R, chunk=R).items()}
    o_a = _fox_sample(l, page_table, o['qa'].astype(F32), o['kaug'], o['vbf'], caches['fox_kv'], caches['fox_lf'],
                      t_new=t_new, pg=pg)
    glu_t = jnp.swapaxes(o['glu'].reshape(db, t_new, GROUP_WIDTH), 0, 1)
    o_b_t, conv_new = _conv_sample(l, caches['conv'], glu_t, w)
    o_b = jnp.swapaxes(o_b_t, 0, 1).reshape(R, GROUP_WIDTH)
    cmp = _compress_sample(l, page_table, caches['nsa_blocks'], w['pe_flat'], w['phi'], n_group=pg)
    kc_t, vc = _even_odd(cmp)
    qc = o['qc'].astype(F32)
    part, flags = _nsa_local_sample(l, qc, o['gc'], kc_t, vc, caches['win'], o['nsabf'],
                                    t_new=t_new, past_len=past_len)
    flags = flags[:, 0, ::2].astype(jnp.int32)
    o_c = _nsa_sel_sample(l, page_table, flags, qc, o['gc'], part, o['nsabf'], caches['nsa_kv'], t_new=t_new)
    y = _merge_ffn(xs.reshape(R, D), [o_a, o_b, o_c, o['od']], w, tm=R, tf=tf)
    rows = lambda a, *shape: a.T.reshape(db, t_new, *shape)
    win_new = jnp.swapaxes(o['nsawin'].reshape(2 * HEAD_DIM, db, t_new), 0, 1)
    win = jnp.concatenate([caches['win'][l][:, :, t_new:], win_new], axis=-1)
    states = dict(fox_kv=rows(o['foxkv'], 2, N_HEADS, HEAD_DIM), fox_logf=rows(o['logf'], N_HEADS),
                  nsa_kv=rows(o['nsakv'], 4, HEAD_DIM),
                  nsa_win=jnp.transpose(win.reshape(db, 2, HEAD_DIM, -1), (0, 3, 1, 2)),
                  conv=jnp.swapaxes(conv_new, 0, 1), gmlp_v=o['vn'].reshape(db, t_new, GROUP_WIDTH))
    return y.reshape(db, t_new, D), states


def _scan_matrix(n, seg):
    i = np.arange(n)
    return jnp.asarray((i[:, None] <= i[None, :]) & (i[:, None] // seg == i[None, :] // seg), BF16)


_PARAM_NAMES = ('norm1_g', 'w_in', 'fox_bf', 'fox_qn_g', 'fox_kn_g', 'conv_w', 'conv_b', 'conv_ln_g', 'conv_ln_b',
                'nsa_qn_g', 'nsa_kn_g', 'nsa_pe', 'nsa_phi_k', 'nsa_phi_v', 'gmlp_ln_g', 'gmlp_ln_b', 'gmlp_ws',
                'gmlp_bs', 'gnorm_g', 'w_out', 'norm2_g', 'w_ff1', 'w_ff2')


def kernel(x_prompt, x_sample, cache_fox_kv, cache_fox_logf, cache_nsa_kv, state_nsa_win, state_conv, page_table,
           *params):
    P = dict(zip(_PARAM_NAMES, params))
    depth = P['w_in'].shape[0]
    B, S, D = x_prompt.shape
    DB, T, _ = x_sample.shape
    n_pages, page = page_table.shape[1], cache_fox_kv.shape[2]
    past_len = n_pages * page
    assert past_len % SEL_BLOCK == 0 and T <= SEL_BLOCK and past_len >= WINDOW and page == LANES
    tm = 512
    consts_p = dict(_rope_tables(jnp.arange(S)), **_const_tables(), utri=_scan_matrix(tm, tm))
    consts_s = dict(_rope_tables(jnp.tile(past_len + jnp.arange(T), DB)), **_const_tables(),
                    utri=_scan_matrix(DB * T, T))
    caches = _prep_caches(cache_fox_kv, cache_fox_logf, cache_nsa_kv, state_nsa_win, state_conv)
    xp, xs = x_prompt, x_sample
    st_p, st_s = [], []
    for l in range(depth):
        w = _prep_layer(l, P)
        xp, sp = _layer_prompt(xp, w, consts_p, tm=tm, ta=512, tk=512, tf=1024)
        xs, ss = _layer_sample(l, xs, w, caches, consts_s, page_table, past_len=past_len, pg=32, tf=1024)
        st_p.append(sp)
        st_s.append(ss)
    stack_p = lambda k: jnp.stack([s[k] for s in st_p])
    stack_s = lambda k: jnp.stack([s[k] for s in st_s])
    fox_kv_p = jnp.transpose(stack_p('fox_kv').reshape(depth, B, 2, N_HEADS, HEAD_DIM, S), (0, 1, 5, 2, 3, 4))
    fox_logf_p = jnp.transpose(stack_p('fox_logf'), (0, 1, 3, 2))
    nsa_kv_p = jnp.transpose(stack_p('nsa_kv').reshape(depth, B, 4, HEAD_DIM, S), (0, 1, 4, 2, 3))
    nsa_win_p = jnp.transpose(stack_p('nsa_win').reshape(depth, B, 2, HEAD_DIM, -1), (0, 1, 4, 2, 3))
    return (xp, xs, fox_kv_p, stack_s('fox_kv'), fox_logf_p, stack_s('fox_logf'),
            nsa_kv_p, stack_s('nsa_kv'), nsa_win_p, stack_s('nsa_win'),
            stack_p('conv'), stack_s('conv'), stack_s('gmlp_v'))
```

```python
import functools

import jax
import jax.numpy as jnp
import numpy as np
from jax import lax
from jax.experimental import pallas as pl
from jax.experimental.pallas import tpu as pltpu

F32 = jnp.float32
BF16 = jnp.bfloat16

HEAD_DIM = 64
HALF = HEAD_DIM // 2
GROUP_WIDTH = 256
N_HEADS = GROUP_WIDTH // HEAD_DIM
CONV_WIDTH = 31
CMP_BLOCK = 32
SEL_BLOCK = 64
N_SELECT = 16
WINDOW = 512
CHUNK = 128
ROPE_THETA = 10000.0
EPS = 1e-6
FORCED_SCORE = 1e4
Q_SCALE = HEAD_DIM ** -0.5
NEG_BIG = -1e30
SEL_NEG = -32768.0
LANES = 128
VMEM_LIMIT = 56 * 1024 * 1024


def _cparams(sem):
    return pltpu.CompilerParams(dimension_semantics=sem, vmem_limit_bytes=VMEM_LIMIT)


def _nt(a, b):
    return lax.dot_general(a, b, (((1,), (1,)), ((), ())), preferred_element_type=F32)


def _dot(a, b):
    return jnp.dot(a, b, preferred_element_type=F32)


def _split3(x):
    h = x.astype(BF16).astype(F32)
    r = x - h
    m = r.astype(BF16).astype(F32)
    l = (r - m).astype(BF16).astype(F32)
    return h, m, l


def _log_sigmoid(x):
    return jnp.minimum(x, 0.0) - jnp.log1p(jnp.exp(-jnp.abs(x)))


def _group_mean_sq(x, gsum):
    x2 = x * x
    hi = x2.astype(BF16)
    lo = (x2 - hi.astype(F32)).astype(BF16)
    return (_dot(hi, gsum) + _dot(lo, gsum)) * (1.0 / HEAD_DIM)


def _rope_rows(x, cos, sin_signed):
    lane = lax.broadcasted_iota(jnp.int32, x.shape, 1)
    first_half = (lane % HEAD_DIM) < HALF
    swapped = jnp.where(first_half, pltpu.roll(x, LANES - HALF, 1), pltpu.roll(x, HALF, 1))
    return x * cos + swapped * sin_signed


_R_QA, _R_GLU, _R_QC, _R_UD, _R_VD, _R_SMALL, _R_END = 0, 256, 768, 1024, 1280, 1536, 1664
_C_KA, _C_VA, _C_KVC, _C_FA, _C_END = 0, 256, 512, 896, 912


def _proj_body(x_ref, g1_ref, wrow_ref, wcol_ref, cosr_ref, sinr_ref, cost_ref, sint_ref,
               gsum_ref, gqa_ref, gka_ref, bf_ref, gqc_ref, gkc_ref, lng_ref, lnb_ref,
               wm_ref, bstab_ref, utri_ref,
               qa_ref, foxkv_ref, kaug_ref, vbf_ref, logf_ref, glu_ref, qc_ref, nsakv_ref,
               nsawin_ref, nsabf_ref, cmprow_ref, gc_ref, od_ref, vn_ref,
               carry_ref, *, chunk):
    tm = x_ref.shape[1]

    @pl.when(pl.program_id(1) == 0)
    def _():
        carry_ref[...] = jnp.zeros_like(carry_ref)

    x = x_ref[0]
    ms = jnp.mean(x * x, axis=-1, keepdims=True)
    xn = ((x * lax.rsqrt(ms + EPS)) * g1_ref[...]).astype(BF16)
    zr = _dot(xn, wrow_ref[...])
    zc = _nt(wcol_ref[...], xn)
    gsum = gsum_ref[...]

    qa = zr[:, _R_QA:_R_QA + 256]
    qa = qa * lax.rsqrt(_group_mean_sq(qa, gsum) + EPS) * gqa_ref[...] * Q_SCALE
    lane = lax.broadcasted_iota(jnp.int32, (tm, LANES), 1)
    for h in range(N_HEADS):
        src = qa[:, (h // 2) * LANES:(h // 2 + 1) * LANES]
        if h % 2 == 0:
            aug = jnp.where(lane < HEAD_DIM, src, jnp.where(lane < HEAD_DIM + 3, 1.0, 0.0))
        else:
            aug = jnp.where(lane >= HEAD_DIM, src, jnp.where(lane < 3, 1.0, 0.0))
        qa_ref[0, :, h * LANES:(h + 1) * LANES] = aug.astype(BF16)

    logf = _log_sigmoid(zc[_C_FA:_C_FA + 8] + bf_ref[...])
    logf_ref[0] = logf[0:N_HEADS]
    parts = _split3(logf)
    l3 = jnp.concatenate(parts, axis=0).astype(BF16)
    cs = _dot(l3, utri_ref[...])
    fcum = cs[0:8] + cs[8:16] + cs[16:24] + carry_ref[:, 0:1]
    carry_ref[...] = jnp.broadcast_to(fcum[:, tm - 1:tm], carry_ref.shape)
    nfh, nfm, nfl = _split3(-fcum)
    row8 = lax.broadcasted_iota(jnp.int32, (8, tm), 0)
    zeros56 = jnp.zeros((HEAD_DIM - 8, tm), F32)
    gka = gka_ref[:, 0:1]
    for h in range(N_HEADS):
        k = zc[_C_KA + h * HEAD_DIM:_C_KA + (h + 1) * HEAD_DIM]
        k = k * lax.rsqrt(jnp.mean(k * k, axis=0, keepdims=True) + EPS) * gka
        v = zc[_C_VA + h * HEAD_DIM:_C_VA + (h + 1) * HEAD_DIM]
        foxkv_ref[0, h * HEAD_DIM:(h + 1) * HEAD_DIM, :] = k
        foxkv_ref[0, GROUP_WIDTH + h * HEAD_DIM:GROUP_WIDTH + (h + 1) * HEAD_DIM, :] = v
        vbf_ref[0, h * HEAD_DIM:(h + 1) * HEAD_DIM, :] = v.astype(BF16)
        extra8 = jnp.where(row8 == 0, nfh[h:h + 1],
                           jnp.where(row8 == 1, nfm[h:h + 1],
                                     jnp.where(row8 == 2, nfl[h:h + 1], 0.0)))
        extra = jnp.concatenate([extra8, zeros56], axis=0)
        pieces = [k, extra] if h % 2 == 0 else [extra, k]
        kaug_ref[0, h] = jnp.concatenate(pieces, axis=0).astype(BF16)

    glu_in = zr[:, _R_GLU:_R_GLU + 512]
    glu_ref[0] = glu_in[:, :256] * jax.nn.sigmoid(glu_in[:, 256:])

    qc = zr[:, _R_QC:_R_QC + 256]
    qc = qc * lax.rsqrt(_group_mean_sq(qc, gsum) + EPS) * gqc_ref[...]
    cosr, sinr = cosr_ref[...], sinr_ref[...]
    for p in range(2):
        qh = _rope_rows(qc[:, p * LANES:(p + 1) * LANES], cosr, sinr) * Q_SCALE
        qc_ref[0, :, p * LANES:(p + 1) * LANES] = qh.astype(BF16)
    gc_ref[0] = jax.nn.sigmoid(zr[:, _R_SMALL:_R_SMALL + LANES])
    cost, sint = cost_ref[...], sint_ref[...]
    keys = []
    for b in range(3):
        kb = zc[_C_KVC + 2 * b * HEAD_DIM:_C_KVC + (2 * b + 1) * HEAD_DIM]
        kb = kb * lax.rsqrt(jnp.mean(kb * kb, axis=0, keepdims=True) + EPS) * gkc_ref[:, b:b + 1]
        x1, x2 = kb[:HALF], kb[HALF:]
        keys.append(jnp.concatenate([x1 * cost - x2 * sint, x2 * cost + x1 * sint], axis=0))
    vals = [zc[_C_KVC + (2 * b + 1) * HEAD_DIM:_C_KVC + (2 * b + 2) * HEAD_DIM] for b in range(3)]
    nsakv_ref[0, 0:64, :] = keys[0]
    nsakv_ref[0, 64:128, :] = vals[0]
    nsakv_ref[0, 128:192, :] = keys[1]
    nsakv_ref[0, 192:256, :] = vals[1]
    nsawin_ref[0, 0:64, :] = keys[2]
    nsawin_ref[0, 64:128, :] = vals[2]
    nsabf_ref[0, 0:64, :] = keys[1].astype(BF16)
    nsabf_ref[0, 64:128, :] = vals[1].astype(BF16)
    nsabf_ref[0, 128:192, :] = keys[2].astype(BF16)
    nsabf_ref[0, 192:256, :] = vals[2].astype(BF16)
    cmprow_ref[0] = jnp.transpose(jnp.concatenate([keys[0], vals[0]], axis=0))

    ud = zr[:, _R_UD:_R_UD + 256]
    vd = zr[:, _R_VD:_R_VD + 256]
    mu = jnp.mean(vd, axis=-1, keepdims=True)
    var = jnp.mean(jnp.square(vd - mu), axis=-1, keepdims=True)
    vn = (vd - mu) * lax.rsqrt(var + EPS) * lng_ref[...] + lnb_ref[...]
    vn_ref[0] = vn
    grp = lax.broadcasted_iota(jnp.int32, (chunk, GROUP_WIDTH), 1) // HEAD_DIM
    wm = wm_ref[...]
    for c in range(tm // chunk):
        r = _dot(wm, vn[c * chunk:(c + 1) * chunk].astype(BF16))
        mixed = bstab_ref[...]
        for g in range(N_HEADS):
            mixed = mixed + jnp.where(grp == g, r[g * chunk:(g + 1) * chunk], 0.0)
        od_ref[0, c * chunk:(c + 1) * chunk, :] = ud[c * chunk:(c + 1) * chunk] * mixed


def _proj(x, w, tabs, *, tm, chunk):
    B, S, D = x.shape
    ns = S // tm
    row = lambda width: pl.BlockSpec((1, tm, width), lambda b, i: (b, i, 0))
    col = lambda height: pl.BlockSpec((1, height, tm), lambda b, i: (b, 0, i))
    full = lambda a: pl.BlockSpec(a.shape, lambda b, i: (0,) * a.ndim)
    ins = [x, w['norm1_g'], w['w_row'], w['w_col'], tabs['cos_r'], tabs['sin_r'], tabs['cos_t'], tabs['sin_t'],
           tabs['gsum'], w['gqa'], w['gka'], w['bf'], w['gqc'], w['gkc'], w['gmlp_ln_g'], w['gmlp_ln_b'],
           tabs['wm'], tabs['bs_tab'], tabs['utri']]
    in_specs = [row(D), full(ins[1]), full(ins[2]), full(ins[3]),
                pl.BlockSpec((tm, LANES), lambda b, i: (i, 0)), pl.BlockSpec((tm, LANES), lambda b, i: (i, 0)),
                pl.BlockSpec((HALF, tm), lambda b, i: (0, i)), pl.BlockSpec((HALF, tm), lambda b, i: (0, i))]
    in_specs += [full(a) for a in ins[8:]]
    outs = dict(
        qa=(jax.ShapeDtypeStruct((B, S, 4 * LANES), BF16), row(4 * LANES)),
        foxkv=(jax.ShapeDtypeStruct((B, 2 * GROUP_WIDTH, S), F32), col(2 * GROUP_WIDTH)),
        kaug=(jax.ShapeDtypeStruct((B, N_HEADS, LANES, S), BF16),
              pl.BlockSpec((1, N_HEADS, LANES, tm), lambda b, i: (b, 0, 0, i))),
        vbf=(jax.ShapeDtypeStruct((B, GROUP_WIDTH, S), BF16), col(GROUP_WIDTH)),
        logf=(jax.ShapeDtypeStruct((B, N_HEADS, S), F32), col(N_HEADS)),
        glu=(jax.ShapeDtypeStruct((B, S, GROUP_WIDTH), F32), row(GROUP_WIDTH)),
        qc=(jax.ShapeDtypeStruct((B, S, GROUP_WIDTH), BF16), row(GROUP_WIDTH)),
        nsakv=(jax.ShapeDtypeStruct((B, 256, S), F32), col(256)),
        nsawin=(jax.ShapeDtypeStruct((B, 128, S), F32), col(128)),
        nsabf=(jax.ShapeDtypeStruct((B, 256, S), BF16), col(256)),
        cmprow=(jax.ShapeDtypeStruct((B, S, LANES), F32), row(LANES)),
        gc=(jax.ShapeDtypeStruct((B, S, LANES), F32), row(LANES)),
        od=(jax.ShapeDtypeStruct((B, S, GROUP_WIDTH), F32), row(GROUP_WIDTH)),
        vn=(jax.ShapeDtypeStruct((B, S, GROUP_WIDTH), F32), row(GROUP_WIDTH)),
    )
    names = list(outs)
    res = pl.pallas_call(
        functools.partial(_proj_body, chunk=chunk),
        grid=(B, ns),
        in_specs=in_specs,
        out_specs=[outs[n][1] for n in names],
        out_shape=[outs[n][0] for n in names],
        scratch_shapes=[pltpu.VMEM((8, LANES), F32)],
        compiler_params=_cparams(("arbitrary", "arbitrary")),
        name="proj",
    )(*ins)
    return dict(zip(names, res))


def _online_softmax_step(s, v, carry):
    m, l, acc = carry
    m_new = jnp.maximum(m, jnp.max(s, axis=-1, keepdims=True))
    alpha = jnp.exp(m - m_new)
    p = jnp.exp(s - m_new)
    l = alpha * l + jnp.sum(p, axis=-1, keepdims=True)
    acc = alpha * acc + _nt(p.astype(BF16), v)
    return m_new, l, acc


def _softmax_init(rows, dv):
    return (jnp.full((rows, 1), NEG_BIG, F32), jnp.zeros((rows, 1), F32), jnp.zeros((rows, dv), F32))


def _fox_attn_body(q_ref, k_ref, v_ref, o_ref):
    t = q_ref.shape[1]
    i = pl.program_id(1)
    row = lax.broadcasted_iota(jnp.int32, (t, t), 0)
    colm = lax.broadcasted_iota(jnp.int32, (t, t), 1)

    def tile(kt, carries, masked):
        start = pl.multiple_of(kt * t, t)
        out = []
        for h in range(N_HEADS):
            s = _dot(q_ref[0, :, h * LANES:(h + 1) * LANES], k_ref[0, h, :, pl.ds(start, t)])
            if masked:
                s = jnp.where(colm <= row, s, NEG_BIG)
            v = v_ref[0, h * HEAD_DIM:(h + 1) * HEAD_DIM, pl.ds(start, t)]
            out.append(_online_softmax_step(s, v, carries[h]))
        return tuple(out)

    init = tuple(_softmax_init(t, HEAD_DIM) for _ in range(N_HEADS))
    carries = tile(i, lax.fori_loop(0, i, functools.partial(tile, masked=False), init), True)
    o_ref[0] = jnp.concatenate([acc / l for _, l, acc in carries], axis=-1)


def _fox_attn(qa, kaug, vbf, *, t):
    B, S, _ = qa.shape
    return pl.pallas_call(
        _fox_attn_body,
        grid=(B, S // t),
        in_specs=[pl.BlockSpec((1, t, 4 * LANES), lambda b, i: (b, i, 0)),
                  pl.BlockSpec((1, N_HEADS, LANES, S), lambda b, i: (b, 0, 0, 0)),
                  pl.BlockSpec((1, GROUP_WIDTH, S), lambda b, i: (b, 0, 0))],
        out_specs=pl.BlockSpec((1, t, GROUP_WIDTH), lambda b, i: (b, i, 0)),
        out_shape=jax.ShapeDtypeStruct((B, S, GROUP_WIDTH), F32),
        compiler_params=_cparams(("arbitrary", "arbitrary")),
        name="fox_attn",
    )(qa, kaug, vbf)


_HALO = 32


def _ln_silu(y, g, b):
    mu = jnp.mean(y, axis=-1, keepdims=True)
    var = jnp.mean(jnp.square(y - mu), axis=-1, keepdims=True)
    y = (y - mu) * lax.rsqrt(var + EPS) * g + b
    return y * jax.nn.sigmoid(y)


def _conv_prompt_body(cur_ref, halo_ref, w_ref, cb_ref, lng_ref, lnb_ref, o_ref, xin_ref):
    tm = cur_ref.shape[1]
    first = pl.program_id(1) == 0
    xin_ref[0:_HALO, :] = jnp.where(first, 0.0, halo_ref[0])
    xin_ref[_HALO:, :] = cur_ref[0]
    off = _HALO - (CONV_WIDTH - 1)
    acc = jnp.zeros((tm, GROUP_WIDTH), F32)
    for k in range(CONV_WIDTH):
        acc = acc + xin_ref[pl.ds(off + k, tm), :] * w_ref[k:k + 1, :]
    o_ref[0] = _ln_silu(acc + cb_ref[...], lng_ref[...], lnb_ref[...])


def _conv_prompt(glu, w, *, tm):
    B, S, C = glu.shape
    r = tm // _HALO
    full = lambda a: pl.BlockSpec(a.shape, lambda b, i: (0,) * a.ndim)
    ins = [glu, glu, w['conv_w'], w['conv_b'], w['conv_ln_g'], w['conv_ln_b']]
    return pl.pallas_call(
        _conv_prompt_body,
        grid=(B, S // tm),
        in_specs=[pl.BlockSpec((1, tm, C), lambda b, i: (b, i, 0)),
                  pl.BlockSpec((1, _HALO, C), lambda b, i: (b, jnp.maximum(i * r - 1, 0), 0))]
                 + [full(a) for a in ins[2:]],
        out_specs=pl.BlockSpec((1, tm, C), lambda b, i: (b, i, 0)),
        out_shape=jax.ShapeDtypeStruct((B, S, C), F32),
        scratch_shapes=[pltpu.VMEM((tm + _HALO, C), F32)],
        compiler_params=_cparams(("arbitrary", "arbitrary")),
        name="conv_prompt",
    )(*ins)


def _compress_body(x_ref, pe_ref, phi_ref, o_ref):
    o_ref[0] = _dot((x_ref[0] + pe_ref[...]).astype(BF16), phi_ref[...])


def _compress(blocks, pe_flat, phi):
    B, n, width = blocks.shape
    return pl.pallas_call(
        _compress_body,
        grid=(B,),
        in_specs=[pl.BlockSpec((1, n, width), lambda b: (b, 0, 0)),
                  pl.BlockSpec(pe_flat.shape, lambda b: (0, 0)),
                  pl.BlockSpec(phi.shape, lambda b: (0, 0))],
        out_specs=pl.BlockSpec((1, n, LANES), lambda b: (b, 0, 0)),
        out_shape=jax.ShapeDtypeStruct((B, n, LANES), F32),
        compiler_params=_cparams(("arbitrary",)),
        name="nsa_compress",
    )(blocks, pe_flat, phi)


def _masked_softmax(s, mask):
    s = jnp.where(mask, s, NEG_BIG)
    m = jnp.max(s, axis=-1, keepdims=True)
    e = jnp.where(mask, jnp.exp(s - m), 0.0)
    return e / jnp.maximum(jnp.sum(e, axis=-1, keepdims=True), 1e-30)


def _stack_heads(q):
    return jnp.concatenate([q[:, h * HEAD_DIM:(h + 1) * HEAD_DIM] for h in range(N_HEADS)], axis=0)


def _select_blocks(imp, cur, n_select):
    j = lax.broadcasted_iota(jnp.int32, imp.shape, 1)
    forced = (j == 0) | (j == cur) | (j == cur - 1)
    v = jnp.where(forced, FORCED_SCORE, imp)
    v = jnp.where(j <= cur, v, -1.0)
    sel = jnp.zeros(imp.shape, jnp.bool_)
    for _ in range(n_select):
        m = jnp.max(v, axis=-1, keepdims=True)
        idx = jnp.min(jnp.where(v == m, j, imp.shape[1]), axis=-1, keepdims=True)
        pick = j == idx
        sel = sel | (pick & (m >= 0.0))
        v = jnp.where(pick, -2.0, v)
    return sel


def _nsa_local_body(q_ref, g_ref, kc_ref, vc_ref, kv_ref, o_ref, sb_ref):
    t = q_ref.shape[1]
    nc = kc_ref.shape[2]
    half = nc // 2
    st = pl.program_id(1) * t
    qs = _stack_heads(q_ref[0])
    qpos = st + lax.broadcasted_iota(jnp.int32, (N_HEADS * t, 1), 0) % t

    c = lax.broadcasted_iota(jnp.int32, (1, nc), 1)
    blk = jnp.where(c < half, 2 * c, 2 * (c - half) + 1)
    p_cmp = _masked_softmax(_dot(qs, kc_ref[0]), (blk + 1) * CMP_BLOCK - 1 <= qpos)
    o_cmp = _dot(p_cmp.astype(BF16), vc_ref[0])
    imp = p_cmp[0:t] + p_cmp[t:2 * t] + p_cmp[2 * t:3 * t] + p_cmp[3 * t:4 * t]
    imp = imp[:, :half] + imp[:, half:]
    sel = _select_blocks(imp, qpos[0:t] // SEL_BLOCK, min(N_SELECT, half))
    sb_ref[0] = jnp.where(sel, 0.0, SEL_NEG).astype(BF16)

    span = WINDOW + t
    start = pl.multiple_of(jnp.maximum(st - WINDOW, 0), LANES)
    kwpos = start + lax.broadcasted_iota(jnp.int32, (1, span), 1)
    wmask = (kwpos <= qpos) & (qpos - kwpos < WINDOW)
    p_win = _masked_softmax(_dot(qs, kv_ref[0, 128:192, pl.ds(start, span)]), wmask)
    o_win = _nt(p_win.astype(BF16), kv_ref[0, 192:256, pl.ds(start, span)])

    g = g_ref[0]
    outs = [g[:, 3 * h:3 * h + 1] * o_cmp[h * t:(h + 1) * t] + g[:, 3 * h + 2:3 * h + 3] * o_win[h * t:(h + 1) * t]
            for h in range(N_HEADS)]
    o_ref[0] = jnp.concatenate(outs, axis=-1)


def _nsa_local(qc, gc, kc_t, vc, nsabf, *, t):
    B, S, _ = qc.shape
    nc = kc_t.shape[2]
    tile = lambda width: pl.BlockSpec((1, t, width), lambda b, i: (b, i, 0))
    return pl.pallas_call(
        _nsa_local_body,
        grid=(B, S // t),
        in_specs=[tile(GROUP_WIDTH), tile(LANES),
                  pl.BlockSpec((1, HEAD_DIM, nc), lambda b, i: (b, 0, 0)),
                  pl.BlockSpec((1, nc, HEAD_DIM), lambda b, i: (b, 0, 0)),
                  pl.BlockSpec((1, 256, S), lambda b, i: (b, 0, 0))],
        out_specs=[tile(GROUP_WIDTH), tile(nc // 2)],
        out_shape=[jax.ShapeDtypeStruct((B, S, GROUP_WIDTH), F32),
                   jax.ShapeDtypeStruct((B, S, nc // 2), BF16)],
        compiler_params=_cparams(("arbitrary", "arbitrary")),
        name="nsa_local",
    )(qc, gc, kc_t, vc, nsabf)


def _nsa_sel_body(q_ref, sb_ref, g_ref, part_ref, kv_ref, e_ref, o_ref, *, tk):
    t = q_ref.shape[1]
    i = pl.program_id(1)
    st = i * t
    q2 = jnp.concatenate([_stack_heads(q_ref[0]), jnp.concatenate([sb_ref[0]] * N_HEADS, axis=0)], axis=-1)
    qpos = st + lax.broadcasted_iota(jnp.int32, (N_HEADS * t, 1), 0) % t

    def tile(kt, carry, masked):
        start = pl.multiple_of(kt * tk, tk)
        k2 = jnp.concatenate([kv_ref[0, 0:64, pl.ds(start, tk)], e_ref[:, pl.ds(start, tk)]], axis=0)
        s = _dot(q2, k2)
        if masked:
            kpos = start + lax.broadcasted_iota(jnp.int32, (1, tk), 1)
            s = jnp.where(kpos <= qpos, s, NEG_BIG)
        return _online_softmax_step(s, kv_ref[0, 64:128, pl.ds(start, tk)], carry)

    last = (st + t - 1) // tk
    carry = lax.fori_loop(0, last, functools.partial(tile, masked=False), _softmax_init(N_HEADS * t, HEAD_DIM))
    m, l, acc = tile(last, carry, True)
    o_sel = acc / l
    g = g_ref[0]
    outs = [g[:, 3 * h + 1:3 * h + 2] * o_sel[h * t:(h + 1) * t] for h in range(N_HEADS)]
    o_ref[0] = part_ref[0] + jnp.concatenate(outs, axis=-1)


def _nsa_sel(qc, selbias, gc, part, nsabf, onehot, *, t, tk):
    B, S, _ = qc.shape
    ns = selbias.shape[2]
    tile = lambda width: pl.BlockSpec((1, t, width), lambda b, i: (b, i, 0))
    return pl.pallas_call(
        functools.partial(_nsa_sel_body, tk=tk),
        grid=(B, S // t),
        in_specs=[tile(GROUP_WIDTH), tile(ns), tile(LANES), tile(GROUP_WIDTH),
                  pl.BlockSpec((1, 256, S), lambda b, i: (b, 0, 0)),
                  pl.BlockSpec((ns, S), lambda b, i: (0, 0))],
        out_specs=tile(GROUP_WIDTH),
        out_shape=jax.ShapeDtypeStruct((B, S, GROUP_WIDTH), F32),
        compiler_params=_cparams(("arbitrary", "arbitrary")),
        name="nsa_sel",
    )(qc, selbias, gc, part, nsabf, onehot)


def _nsa_prompt(o, w, *, t, tk):
    B, S, _ = o['qc'].shape
    n = S // CMP_BLOCK
    cmp = _compress(o['cmprow'].reshape(B, n, CMP_BLOCK * LANES), w['pe_flat'], w['phi'])
    kc_t, vc = _even_odd(cmp)
    part, selbias = _nsa_local(o['qc'], o['gc'], kc_t, vc, o['nsabf'], t=t)
    ns = n // 2
    onehot = jnp.asarray(np.arange(ns)[:, None] == (np.arange(S) // SEL_BLOCK)[None, :], BF16)
    return _nsa_sel(o['qc'], selbias, o['gc'], part, o['nsabf'], onehot, t=t, tk=tk)


def _merge_ffn_body(x_ref, oa_ref, ob_ref, oc_ref, od_ref, gn_ref, wout_ref, g2_ref, w1_ref, w2_ref,
                    y_ref, hn_ref, acc_ref):
    j = pl.program_id(1)

    @pl.when(j == 0)
    def _():
        h = x_ref[...]
        for i, o_ref in enumerate((oa_ref, ob_ref, oc_ref, od_ref)):
            o = o_ref[...]
            o = o * lax.rsqrt(jnp.mean(o * o, axis=-1, keepdims=True) + EPS) * gn_ref[i:i + 1, :]
            h = h + _dot(o.astype(BF16), wout_ref[i * GROUP_WIDTH:(i + 1) * GROUP_WIDTH, :])
        acc_ref[...] = h
        hn = h * lax.rsqrt(jnp.mean(h * h, axis=-1, keepdims=True) + EPS) * g2_ref[...]
        hn_ref[...] = hn.astype(BF16)

    u = jnp.maximum(_dot(hn_ref[...], w1_ref[...]), 0.0)
    acc_ref[...] += _dot((u * u).astype(BF16), w2_ref[...])

    @pl.when(j == pl.num_programs(1) - 1)
    def _():
        y_ref[...] = acc_ref[...]


def _merge_ffn(x, outs, w, *, tm, tf):
    R, D = x.shape
    F = w['w_ff1'].shape[1]
    rows = lambda width: pl.BlockSpec((tm, width), lambda i, j: (i, 0))
    full = lambda a: pl.BlockSpec(a.shape, lambda i, j: (0,) * a.ndim)
    return pl.pallas_call(
        _merge_ffn_body,
        grid=(R // tm, F // tf),
        in_specs=[rows(D)] + [rows(GROUP_WIDTH)] * 4 + [full(w['gnorm_g']), full(w['w_out']), full(w['norm2_g']),
                  pl.BlockSpec((D, tf), lambda i, j: (0, j)), pl.BlockSpec((tf, D), lambda i, j: (j, 0))],
        out_specs=rows(D),
        out_shape=jax.ShapeDtypeStruct((R, D), F32),
        scratch_shapes=[pltpu.VMEM((tm, D), BF16), pltpu.VMEM((tm, D), F32)],
        compiler_params=_cparams(("arbitrary", "arbitrary")),
        name="merge_ffn",
    )(x, *outs, w['gnorm_g'], w['w_out'], w['norm2_g'], w['w_ff1'], w['w_ff2'])


def _dot3(parts, rhs=None, lhs=None):
    if rhs is not None:
        return sum(_dot(p.astype(BF16), rhs) for p in parts)
    return sum(_dot(lhs, p.astype(BF16)) for p in parts)


def _page_scan_matrices(pg):
    r = np.arange(pg * N_HEADS)
    g, h = r // N_HEADS, r % N_HEADS
    s = np.arange(LANES)
    same = h[:, None] == h[None, :]
    return (jnp.asarray(s[:, None] >= s[None, :], BF16), jnp.asarray(same & (g[None, :] > g[:, None]), BF16),
            jnp.asarray(same, BF16))


def _same_seq_causal(b, t_new, n_cols, rows):
    col = lax.broadcasted_iota(jnp.int32, (rows, n_cols), 1)
    t = lax.broadcasted_iota(jnp.int32, (rows, n_cols), 0) % t_new
    return (col // t_new == b) & (col % t_new <= t)


def _fox_sample_body(pt_ref, q_ref, knew_ref, vnew_ref, tri_ref, later_ref, same_ref, kv_hbm, lf_hbm, o_ref,
                     kvbuf, lfbuf, sems, m_ref, l_ref, acc_ref, carry_ref, *, layer, pg):
    b, j = pl.program_id(0), pl.program_id(1)
    nb, nch = pl.num_programs(0), pl.num_programs(1)
    step = b * nch + j
    slot = step % 2
    t_new = q_ref.shape[0]
    rows = N_HEADS * t_new

    def copies(bb, jj, sl):
        first = (nch - 1 - jj) * pg
        out = []
        for g in range(pg):
            pid = pt_ref[bb, first + g]
            out.append(pltpu.make_async_copy(kv_hbm.at[layer, pid], kvbuf.at[sl, g], sems.at[0, sl]))
            out.append(pltpu.make_async_copy(lf_hbm.at[layer, pid], lfbuf.at[sl, pl.ds(g * N_HEADS, N_HEADS)],
                                             sems.at[1, sl]))
        return out

    @pl.when(step == 0)
    def _():
        for c in copies(b, j, slot):
            c.start()

    @pl.when(step + 1 < nb * nch)
    def _():
        nxt = step + 1
        for c in copies(nxt // nch, nxt % nch, 1 - slot):
            c.start()

    @pl.when(j == 0)
    def _():
        m_ref[...] = jnp.full_like(m_ref, NEG_BIG)
        l_ref[...] = jnp.zeros_like(l_ref)
        acc_ref[...] = jnp.zeros_like(acc_ref)
        carry_ref[...] = jnp.zeros_like(carry_ref)

    qv = q_ref[...]
    lane = lax.broadcasted_iota(jnp.int32, (t_new, LANES), 1)
    q4 = jnp.concatenate([jnp.where(lane < HEAD_DIM, qv[:, 0:128], qv[:, 128:256]),
                          jnp.where(lane < HEAD_DIM, qv[:, 256:384], qv[:, 384:512])], axis=-1)
    grp = lax.broadcasted_iota(jnp.int32, (t_new, GROUP_WIDTH), 1) // HEAD_DIM
    wq = jnp.concatenate([jnp.where(grp == h, q4, 0.0) for h in range(N_HEADS)], axis=0).astype(BF16)

    for c in copies(b, j, slot):
        c.wait()

    lf = lfbuf[slot]
    incl = _dot3(_split3(lf), rhs=tri_ref[...])
    tot = _split3(jnp.broadcast_to(incl[:, 0:1], lf.shape))
    suf = incl - lf + _dot3(tot, lhs=later_ref[...]) + carry_ref[...]
    carry_ref[...] += _dot3(tot, lhs=same_ref[...])

    scores = []
    for g in range(pg):
        bias = jnp.concatenate([jnp.broadcast_to(suf[g * N_HEADS + h:g * N_HEADS + h + 1], (t_new, LANES))
                                for h in range(N_HEADS)], axis=0)
        scores.append(_dot(wq, kvbuf[slot, g, 0:GROUP_WIDTH, :].astype(BF16)) + bias)
    m_old = m_ref[...]
    m = jnp.maximum(m_old, jnp.max(functools.reduce(jnp.maximum, scores), axis=-1, keepdims=True))
    alpha = jnp.exp(m_old - m)
    psum = jnp.zeros((rows, LANES), F32)
    acc = alpha * acc_ref[...]
    for g in range(pg):
        p = jnp.exp(scores[g] - m)
        psum = psum + p
        acc = acc + _nt(p.astype(BF16), kvbuf[slot, g, GROUP_WIDTH:, :].astype(BF16))
    l = alpha * l_ref[...] + jnp.sum(psum, axis=-1, keepdims=True)
    m_ref[...], l_ref[...], acc_ref[...] = m, l, acc

    @pl.when(j == nch - 1)
    def _():
        n_cols = knew_ref.shape[2]
        s = jnp.concatenate([_dot(qv[:, h * LANES:(h + 1) * LANES].astype(BF16), knew_ref[h])
                             for h in range(N_HEADS)], axis=0)
        s = jnp.where(_same_seq_causal(b, t_new, n_cols, rows), s, NEG_BIG)
        m2, l2, acc2 = _online_softmax_step(s, vnew_ref[...], (m, l, acc))
        o = acc2 / l2
        o_ref[...] = jnp.concatenate(
            [o[h * t_new:(h + 1) * t_new, h * HEAD_DIM:(h + 1) * HEAD_DIM] for h in range(N_HEADS)], axis=-1)


def _fox_sample(layer, page_table, q_aug, kaug_new, v_new, cache_kv, cache_lf, *, t_new, pg):
    db, n_pages = page_table.shape
    rows = N_HEADS * t_new
    mats = _page_scan_matrices(pg)
    full = lambda a: pl.BlockSpec(a.shape, lambda b, j, pt: (0,) * a.ndim)
    grid_spec = pltpu.PrefetchScalarGridSpec(
        num_scalar_prefetch=1,
        grid=(db, n_pages // pg),
        in_specs=[pl.BlockSpec((t_new, 4 * LANES), lambda b, j, pt: (b, 0)), full(kaug_new), full(v_new)]
                 + [full(a) for a in mats] + [pl.BlockSpec(memory_space=pl.ANY), pl.BlockSpec(memory_space=pl.ANY)],
        out_specs=pl.BlockSpec((t_new, GROUP_WIDTH), lambda b, j, pt: (b, 0)),
        scratch_shapes=[pltpu.VMEM((2, pg) + cache_kv.shape[2:], F32), pltpu.VMEM((2, pg * N_HEADS, LANES), F32),
                        pltpu.SemaphoreType.DMA((2, 2)),
                        pltpu.VMEM((rows, 1), F32), pltpu.VMEM((rows, 1), F32), pltpu.VMEM((rows, GROUP_WIDTH), F32),
                        pltpu.VMEM((pg * N_HEADS, LANES), F32)])
    return pl.pallas_call(
        functools.partial(_fox_sample_body, layer=layer, pg=pg),
        grid_spec=grid_spec,
        out_shape=jax.ShapeDtypeStruct((db * t_new, GROUP_WIDTH), F32),
        compiler_params=_cparams(("arbitrary", "arbitrary")),
        name="fox_sample",
    )(page_table, q_aug, kaug_new, v_new, *mats, cache_kv, cache_lf)


def _conv_sample_body(state_ref, glu_ref, w_ref, cb_ref, lng_ref, lnb_ref, o_ref, new_ref):
    n_state, t_new = state_ref.shape[1], glu_ref.shape[0]
    x = lambda i: state_ref[0, i] if i < n_state else glu_ref[i - n_state]
    for t in range(t_new):
        acc = x(t) * w_ref[0:1, :]
        for k in range(1, CONV_WIDTH):
            acc = acc + x(t + k) * w_ref[k:k + 1, :]
        o_ref[t] = _ln_silu(acc + cb_ref[...], lng_ref[...], lnb_ref[...])
    for i in range(n_state):
        new_ref[i] = x(i + t_new)


def _conv_sample(layer, state_t, glu_t, w):
    _, n_state, db, c = state_t.shape
    t_new = glu_t.shape[0]
    full = lambda a: pl.BlockSpec(a.shape, lambda i: (0,) * a.ndim)
    ins = [state_t, glu_t, w['conv_w'], w['conv_b'], w['conv_ln_g'], w['conv_ln_b']]
    return pl.pallas_call(
        _conv_sample_body,
        grid=(1,),
        in_specs=[pl.BlockSpec((1, n_state, db, c), lambda i: (layer, 0, 0, 0))] + [full(a) for a in ins[1:]],
        out_specs=[pl.BlockSpec((t_new, db, c), lambda i: (0, 0, 0)), pl.BlockSpec((n_state, db, c), lambda i: (0, 0, 0))],
        out_shape=[jax.ShapeDtypeStruct((t_new, db, c), F32), jax.ShapeDtypeStruct((n_state, db, c), F32)],
        compiler_params=_cparams(("arbitrary",)),
        name="conv_sample",
    )(*ins)


_CMP_PER_PAGE = 4


def _compress_sample_body(pt_ref, *refs, n_group):
    x_refs, (pe_ref, phi_ref, o_ref, x_scr) = refs[:n_group], refs[n_group:]
    page = x_scr.shape[0] // n_group
    for g, x_ref in enumerate(x_refs):
        x_scr[g * page:(g + 1) * page, :] = jnp.transpose(x_ref[0, 0] + pe_ref[...])
    n_blocks = n_group * _CMP_PER_PAGE
    acc = jnp.zeros((n_blocks, LANES), F32)
    for r in range(CMP_BLOCK):
        rows = x_scr[pl.ds(r, n_blocks, stride=CMP_BLOCK), :]
        acc = acc + _dot(rows.astype(BF16), phi_ref[r * LANES:(r + 1) * LANES, :])
    o_ref[0] = acc


def _compress_sample(layer, page_table, cache_nsa, pe_col, phi, *, n_group):
    db, n_pages = page_table.shape
    page = cache_nsa.shape[3]
    page_spec = lambda g: pl.BlockSpec((1, 1, LANES, page), lambda b, j, pt: (layer, pt[b, j * n_group + g], 0, 0))
    rows = n_group * _CMP_PER_PAGE
    grid_spec = pltpu.PrefetchScalarGridSpec(
        num_scalar_prefetch=1,
        grid=(db, n_pages // n_group),
        in_specs=[page_spec(g) for g in range(n_group)]
                 + [pl.BlockSpec(pe_col.shape, lambda b, j, pt: (0, 0)), pl.BlockSpec(phi.shape, lambda b, j, pt: (0, 0))],
        out_specs=pl.BlockSpec((1, rows, LANES), lambda b, j, pt: (b, j, 0)),
        scratch_shapes=[pltpu.VMEM((n_group * page, LANES), F32)])
    return pl.pallas_call(
        functools.partial(_compress_sample_body, n_group=n_group),
        grid_spec=grid_spec,
        out_shape=jax.ShapeDtypeStruct((db, n_pages * _CMP_PER_PAGE, LANES), F32),
        compiler_params=_cparams(("arbitrary", "arbitrary")),
        name="nsa_compress_sample",
    )(page_table, *([cache_nsa] * n_group), pe_col, phi)


def _nsa_local_sample_body(q_ref, g_ref, kc_ref, vc_ref, win_ref, new_ref, o_ref, flag_ref, *, past_len):
    b = pl.program_id(0)
    t_new = q_ref.shape[0]
    rows = N_HEADS * t_new
    nc = kc_ref.shape[2]
    half = nc // 2
    qs = _stack_heads(q_ref[...]).astype(BF16)
    tq = lax.broadcasted_iota(jnp.int32, (rows, 1), 0) % t_new
    qpos = past_len + tq

    c = lax.broadcasted_iota(jnp.int32, (1, nc), 1)
    blk = jnp.where(c < half, 2 * c, 2 * (c - half) + 1)
    p_cmp = _masked_softmax(_dot(qs, kc_ref[0]), (blk + 1) * CMP_BLOCK - 1 <= qpos)
    o_cmp = _dot(p_cmp.astype(BF16), vc_ref[0])
    imp = p_cmp[0:t_new]
    for h in range(1, N_HEADS):
        imp = imp + p_cmp[h * t_new:(h + 1) * t_new]
    imp = imp[:, :half] + imp[:, half:]
    cur = jnp.full((t_new, 1), past_len // SEL_BLOCK, jnp.int32)
    sel = _select_blocks(imp, cur, min(N_SELECT, half + 1) - 1)
    j = lax.broadcasted_iota(jnp.int32, sel.shape, 1)
    t = lax.broadcasted_iota(jnp.int32, sel.shape, 0)
    weight = jnp.left_shift(1, 2 * t + j % 2).astype(F32)
    colsum = jnp.sum(jnp.where(sel, weight, 0.0), axis=0, keepdims=True)
    flag_ref[0] = colsum + pltpu.roll(colsum, half - 1, 1)

    w = win_ref.shape[3]
    n_cols = new_ref.shape[1]
    s_old = _dot(qs, win_ref[0, 0, 0:HEAD_DIM, :].astype(BF16))
    s_new = _dot(qs, new_ref[128:192, :])
    i_old = lax.broadcasted_iota(jnp.int32, (1, w), 1)
    mask = jnp.concatenate([jnp.broadcast_to(i_old + (WINDOW - w) > tq, (rows, w)),
                            _same_seq_causal(b, t_new, n_cols, rows)], axis=-1)
    p_win = _masked_softmax(jnp.concatenate([s_old, s_new], axis=-1), mask).astype(BF16)
    o_win = _nt(p_win[:, :w], win_ref[0, 0, HEAD_DIM:, :].astype(BF16)) + _nt(p_win[:, w:], new_ref[192:256, :])

    g = g_ref[...]
    outs = [g[:, 3 * h:3 * h + 1] * o_cmp[h * t_new:(h + 1) * t_new]
            + g[:, 3 * h + 2:3 * h + 3] * o_win[h * t_new:(h + 1) * t_new] for h in range(N_HEADS)]
    o_ref[...] = jnp.concatenate(outs, axis=-1)


def _nsa_local_sample(layer, qc, gc, kc_t, vc, win_t, new_bf, *, t_new, past_len):
    db = kc_t.shape[0]
    nc = kc_t.shape[2]
    w = win_t.shape[3]
    rows = lambda width: pl.BlockSpec((t_new, width), lambda b: (b, 0))
    return pl.pallas_call(
        functools.partial(_nsa_local_sample_body, past_len=past_len),
        grid=(db,),
        in_specs=[rows(GROUP_WIDTH), rows(LANES),
                  pl.BlockSpec((1, HEAD_DIM, nc), lambda b: (b, 0, 0)),
                  pl.BlockSpec((1, nc, HEAD_DIM), lambda b: (b, 0, 0)),
                  pl.BlockSpec((1, 1, 2 * HEAD_DIM, w), lambda b: (layer, b, 0, 0)),
                  pl.BlockSpec(new_bf.shape, lambda b: (0, 0))],
        out_specs=[rows(GROUP_WIDTH), pl.BlockSpec((1, 1, nc // 2), lambda b: (b, 0, 0))],
        out_shape=[jax.ShapeDtypeStruct((db * t_new, GROUP_WIDTH), F32),
                   jax.ShapeDtypeStruct((db, 1, nc // 2), F32)],
        compiler_params=_cparams(("arbitrary",)),
        name="nsa_local_sample",
    )(qc, gc, kc_t, vc, win_t, new_bf)


_SEL_GROUP = 4


def _nsa_sel_sample_body(pt_ref, fl_ref, q_ref, g_ref, part_ref, new_ref, kv_hbm, o_ref,
                         buf, sems, cflag, count, *, layer):
    b = pl.program_id(0)
    nb = pl.num_programs(0)
    n_pages = fl_ref.shape[1]
    slot = b % 2
    t_new = q_ref.shape[0]
    rows = N_HEADS * t_new

    def copy(bb, p, sl, k):
        src = kv_hbm.at[layer, pt_ref[bb, p], pl.ds(2 * HEAD_DIM, 2 * HEAD_DIM)]
        return pltpu.make_async_copy(src, buf.at[sl, k], sems.at[sl])

    def start_all(bb, sl):
        def body(p, k):
            flag = fl_ref[bb, p]

            @pl.when(flag != 0)
            def _():
                copy(bb, p, sl, k).start()
                cflag[sl, k] = flag
            return k + (flag != 0).astype(jnp.int32)
        count[sl] = lax.fori_loop(0, n_pages, body, 0)

    @pl.when(b == 0)
    def _():
        start_all(b, slot)

    @pl.when(b + 1 < nb)
    def _():
        start_all(b + 1, 1 - slot)

    qs = _stack_heads(q_ref[...]).astype(BF16)
    shamt = (2 * (lax.broadcasted_iota(jnp.int32, (rows, LANES), 0) % t_new)
             + lax.broadcasted_iota(jnp.int32, (rows, LANES), 1) // SEL_BLOCK)
    n_fetched = count[slot]

    def wait_body(k, c):
        copy(b, 0, slot, k).wait()
        return c
    lax.fori_loop(0, n_fetched, wait_body, 0)

    def group(gi, carry):
        m_old, l_old, acc = carry
        scores, entries = [], []
        for u in range(_SEL_GROUP):
            k = gi * _SEL_GROUP + u
            entry = jnp.minimum(k, n_fetched - 1)
            flag = jnp.where(k < n_fetched, cflag[slot, entry], 0)
            picked = (jnp.right_shift(jnp.full((rows, LANES), flag, jnp.int32), shamt) & 1) == 1
            scores.append(jnp.where(picked, _dot(qs, buf[slot, entry, 0:HEAD_DIM, :].astype(BF16)), NEG_BIG))
            entries.append(entry)
        m = jnp.maximum(m_old, jnp.max(functools.reduce(jnp.maximum, scores), axis=-1, keepdims=True))
        alpha = jnp.exp(m_old - m)
        psum = jnp.zeros((rows, LANES), F32)
        acc = alpha * acc
        for u in range(_SEL_GROUP):
            p = jnp.exp(scores[u] - m)
            psum = psum + p
            acc = acc + _nt(p.astype(BF16), buf[slot, entries[u], HEAD_DIM:, :].astype(BF16))
        return m, alpha * l_old + jnp.sum(psum, axis=-1, keepdims=True), acc

    carry = lax.fori_loop(0, (n_fetched + _SEL_GROUP - 1) // _SEL_GROUP, group, _softmax_init(rows, HEAD_DIM))
    s = jnp.where(_same_seq_causal(b, t_new, new_ref.shape[1], rows), _dot(qs, new_ref[0:64, :]), NEG_BIG)
    m, l, acc = _online_softmax_step(s, new_ref[64:128, :], carry)
    o_sel = acc / l
    g = g_ref[...]
    outs = [g[:, 3 * h + 1:3 * h + 2] * o_sel[h * t_new:(h + 1) * t_new] for h in range(N_HEADS)]
    o_ref[...] = part_ref[...] + jnp.concatenate(outs, axis=-1)


def _nsa_sel_sample(layer, page_table, flags, qc, gc, part, new_bf, cache_nsa, *, t_new):
    db, n_pages = page_table.shape
    rows = N_HEADS * t_new
    tile = lambda width: pl.BlockSpec((t_new, width), lambda b, pt, fl: (b, 0))
    grid_spec = pltpu.PrefetchScalarGridSpec(
        num_scalar_prefetch=2,
        grid=(db,),
        in_specs=[tile(GROUP_WIDTH), tile(LANES), tile(GROUP_WIDTH),
                  pl.BlockSpec(new_bf.shape, lambda b, pt, fl: (0, 0)),
                  pl.BlockSpec(memory_space=pl.ANY)],
        out_specs=tile(GROUP_WIDTH),
        scratch_shapes=[pltpu.VMEM((2, n_pages, 2 * HEAD_DIM, LANES), F32), pltpu.SemaphoreType.DMA((2,)),
                        pltpu.SMEM((2, n_pages), jnp.int32), pltpu.SMEM((2,), jnp.int32)])
    return pl.pallas_call(
        functools.partial(_nsa_sel_sample_body, layer=layer),
        grid_spec=grid_spec,
        out_shape=jax.ShapeDtypeStruct((db * t_new, GROUP_WIDTH), F32),
        compiler_params=_cparams(("arbitrary",)),
        name="nsa_sel_sample",
    )(page_table, flags, qc, gc, part, new_bf, cache_nsa)


_SPLITS = np.cumsum([0, 256, 256, 256, 4, 512, 256, 384, 12, 256, 256])


def _prep_layer(l, P):
    w_in = P['w_in'][l]
    sec = [w_in[:, _SPLITS[i]:_SPLITS[i + 1]] for i in range(10)]
    qa, ka, va, fa, glu, qc, kvc, gc, ud, vd = sec
    w_row = jnp.concatenate([qa, glu, qc, ud, vd, jnp.pad(gc, ((0, 0), (0, LANES - 12)))], axis=1).astype(BF16)
    w_col = jnp.concatenate([ka, va, kvc, jnp.pad(fa, ((0, 0), (0, 12)))], axis=1).T.astype(BF16)
    row = lambda v: v.reshape(1, -1)
    phi_k = P['nsa_phi_k'][l].reshape(CMP_BLOCK, 1, HEAD_DIM, HEAD_DIM)
    phi_v = P['nsa_phi_v'][l].reshape(CMP_BLOCK, 1, HEAD_DIM, HEAD_DIM)
    zero = jnp.zeros_like(phi_k)
    phi = jnp.concatenate([jnp.concatenate([phi_k, zero], axis=-1), jnp.concatenate([zero, phi_v], axis=-1)], axis=1)
    return dict(
        pe_flat=jnp.transpose(P['nsa_pe'][l], (1, 0, 2)).reshape(1, CMP_BLOCK * LANES),
        pe_col=jnp.tile(jnp.transpose(P['nsa_pe'][l], (0, 2, 1)).reshape(LANES, CMP_BLOCK), (1, _CMP_PER_PAGE)),
        phi=phi.reshape(CMP_BLOCK * LANES, LANES).astype(BF16),
        norm1_g=row(P['norm1_g'][l]), w_row=w_row, w_col=w_col,
        gqa=row(jnp.tile(P['fox_qn_g'][l], N_HEADS)), gka=P['fox_kn_g'][l].reshape(HEAD_DIM, 1),
        bf=jnp.pad(P['fox_bf'][l], (0, 4)).reshape(8, 1),
        gqc=row(jnp.tile(P['nsa_qn_g'][l], N_HEADS)), gkc=P['nsa_kn_g'][l].T,
        gmlp_ln_g=row(P['gmlp_ln_g'][l]), gmlp_ln_b=row(P['gmlp_ln_b'][l]),
        gmlp_ws=P['gmlp_ws'][l], gmlp_bs=P['gmlp_bs'][l],
        gnorm_g=P['gnorm_g'][l], w_out=P['w_out'][l].astype(BF16), norm2_g=row(P['norm2_g'][l]),
        w_ff1=P['w_ff1'][l].astype(BF16), w_ff2=P['w_ff2'][l].astype(BF16),
        conv_w=P['conv_w'][l], conv_b=row(P['conv_b'][l]),
        conv_ln_g=row(P['conv_ln_g'][l]), conv_ln_b=row(P['conv_ln_b'][l]),
    )


def _rope_tables(pos):
    inv = ROPE_THETA ** (-jnp.arange(HALF, dtype=F32) / HALF)
    ang = pos.astype(F32)[:, None] * inv
    cos, sin = jnp.cos(ang), jnp.sin(ang)
    return dict(cos_r=jnp.tile(cos, (1, 4)), sin_r=jnp.tile(jnp.concatenate([-sin, sin], axis=1), (1, 2)),
                cos_t=cos.T, sin_t=sin.T)


def _const_tables():
    g = np.arange(GROUP_WIDTH) // HEAD_DIM
    return dict(gsum=jnp.asarray(g[:, None] == g[None, :], BF16))


def _gmlp_tables(w, seq_len, n_seq):
    t = min(seq_len, CHUNK)
    wm = (w['gmlp_ws'] * jnp.tril(jnp.ones((CHUNK, CHUNK), F32)))[:, :t, :t]
    bs = w['gmlp_bs'][:, :t]
    if seq_len < CHUNK:
        eye = jnp.eye(n_seq, dtype=F32)
        wm = jnp.einsum('ab,gts->gatbs', eye, wm).reshape(N_HEADS, n_seq * t, n_seq * t)
        bs = jnp.tile(bs, (1, n_seq))
    c = wm.shape[1]
    return dict(wm=wm.reshape(N_HEADS * c, c).astype(BF16), bs_tab=jnp.repeat(bs.T, HEAD_DIM, axis=1))


def _layer_prompt(x, w, consts, *, tm, ta, tk, tf):
    B, S, D = x.shape
    tabs = dict(consts, **_gmlp_tables(w, S, B))
    o = _proj(x, w, tabs, tm=tm, chunk=CHUNK)
    o_a = _fox_attn(o['qa'], o['kaug'], o['vbf'], t=ta)
    o_b = _conv_prompt(o['glu'], w, tm=tm)
    o_c = _nsa_prompt(o, w, t=LANES, tk=tk)
    flat = lambda a: a.reshape(B * S, a.shape[-1])
    y = _merge_ffn(flat(x), [flat(o_a), flat(o_b), flat(o_c), flat(o['od'])], w, tm=tm, tf=tf)
    wp = min(WINDOW, S)
    states = dict(fox_kv=o['foxkv'], fox_logf=o['logf'], nsa_kv=o['nsakv'],
                  nsa_win=o['nsawin'][:, :, S - wp:], conv=o['glu'][:, S - (CONV_WIDTH - 1):])
    return y.reshape(B, S, D), states


def _even_odd(cmp):
    n = cmp.shape[1]
    order = np.concatenate([np.arange(0, n, 2), np.arange(1, n, 2)])
    cmp = cmp[:, order].astype(BF16)
    return jnp.swapaxes(cmp[:, :, :HEAD_DIM], 1, 2), cmp[:, :, HEAD_DIM:]


def _prep_caches(cache_fox_kv, cache_fox_logf, cache_nsa_kv, state_nsa_win, state_conv):
    L, pool, page = cache_fox_kv.shape[:3]
    db = state_nsa_win.shape[1]
    return dict(
        fox_kv=jnp.transpose(cache_fox_kv, (0, 1, 3, 4, 5, 2)).reshape(L, pool, 2 * GROUP_WIDTH, page),
        fox_lf=jnp.transpose(cache_fox_logf, (0, 1, 3, 2)),
        nsa_kv=jnp.transpose(cache_nsa_kv, (0, 1, 3, 4, 2)).reshape(L, pool, 4 * HEAD_DIM, page),
        win=jnp.transpose(state_nsa_win, (0, 1, 3, 4, 2)).reshape(L, db, 2 * HEAD_DIM, -1),
        conv=jnp.transpose(state_conv, (0, 2, 1, 3)),
    )


def _layer_sample(l, xs, w, caches, consts, page_table, *, past_len, pg, tf):
    db, t_new, D = xs.shape
    R = db * t_new
    tabs = dict(consts, **_gmlp_tables(w, t_new, db))
    o = {k: v[0] for k, v in _proj(xs.reshape(1, R, D), w, tabs, tm=R, chunk=R).items()}
    o_a = _fox_sample(l, page_table, o['qa'].astype(F32), o['kaug'], o['vbf'], caches['fox_kv'], caches['fox_lf'],
                      t_new=t_new, pg=pg)
    glu_t = jnp.swapaxes(o['glu'].reshape(db, t_new, GROUP_WIDTH), 0, 1)
    o_b_t, conv_new = _conv_sample(l, caches['conv'], glu_t, w)
    o_b = jnp.swapaxes(o_b_t, 0, 1).reshape(R, GROUP_WIDTH)
    cmp = _compress_sample(l, page_table, caches['nsa_kv'], w['pe_col'], w['phi'], n_group=pg)
    kc_t, vc = _even_odd(cmp)
    qc = o['qc'].astype(F32)
    part, flags = _nsa_local_sample(l, qc, o['gc'], kc_t, vc, caches['win'], o['nsabf'],
                                    t_new=t_new, past_len=past_len)
    flags = flags[:, 0, ::2].astype(jnp.int32)
    o_c = _nsa_sel_sample(l, page_table, flags, qc, o['gc'], part, o['nsabf'], caches['nsa_kv'], t_new=t_new)
    y = _merge_ffn(xs.reshape(R, D), [o_a, o_b, o_c, o['od']], w, tm=R, tf=tf)
    rows = lambda a, *shape: a.T.reshape(db, t_new, *shape)
    win_new = jnp.swapaxes(o['nsawin'].reshape(2 * HEAD_DIM, db, t_new), 0, 1)
    win = jnp.concatenate([caches['win'][l][:, :, t_new:], win_new], axis=-1)
    states = dict(fox_kv=rows(o['foxkv'], 2, N_HEADS, HEAD_DIM), fox_logf=rows(o['logf'], N_HEADS),
                  nsa_kv=rows(o['nsakv'], 4, HEAD_DIM),
                  nsa_win=jnp.transpose(win.reshape(db, 2, HEAD_DIM, -1), (0, 3, 1, 2)),
                  conv=jnp.swapaxes(conv_new, 0, 1), gmlp_v=o['vn'].reshape(db, t_new, GROUP_WIDTH))
    return y.reshape(db, t_new, D), states


def _scan_matrix(n, seg):
    i = np.arange(n)
    return jnp.asarray((i[:, None] <= i[None, :]) & (i[:, None] // seg == i[None, :] // seg), BF16)


_PARAM_NAMES = ('norm1_g', 'w_in', 'fox_bf', 'fox_qn_g', 'fox_kn_g', 'conv_w', 'conv_b', 'conv_ln_g', 'conv_ln_b',
                'nsa_qn_g', 'nsa_kn_g', 'nsa_pe', 'nsa_phi_k', 'nsa_phi_v', 'gmlp_ln_g', 'gmlp_ln_b', 'gmlp_ws',
                'gmlp_bs', 'gnorm_g', 'w_out', 'norm2_g', 'w_ff1', 'w_ff2')


def kernel(x_prompt, x_sample, cache_fox_kv, cache_fox_logf, cache_nsa_kv, state_nsa_win, state_conv, page_table,
           *params):
    P = dict(zip(_PARAM_NAMES, params))
    depth = P['w_in'].shape[0]
    B, S, D = x_prompt.shape
    DB, T, _ = x_sample.shape
    n_pages, page = page_table.shape[1], cache_fox_kv.shape[2]
    past_len = n_pages * page
    assert past_len % SEL_BLOCK == 0 and T <= SEL_BLOCK and past_len >= WINDOW and page == LANES
    tm = 512
    consts_p = dict(_rope_tables(jnp.arange(S)), **_const_tables(), utri=_scan_matrix(tm, tm))
    consts_s = dict(_rope_tables(jnp.tile(past_len + jnp.arange(T), DB)), **_const_tables(),
                    utri=_scan_matrix(DB * T, T))
    caches = _prep_caches(cache_fox_kv, cache_fox_logf, cache_nsa_kv, state_nsa_win, state_conv)
    xp, xs = x_prompt, x_sample
    st_p, st_s = [], []
    for l in range(depth):
        w = _prep_layer(l, P)
        xp, sp = _layer_prompt(xp, w, consts_p, tm=tm, ta=512, tk=512, tf=1024)
        xs, ss = _layer_sample(l, xs, w, caches, consts_s, page_table, past_len=past_len, pg=32, tf=1024)
        st_p.append(sp)
        st_s.append(ss)
    stack_p = lambda k: jnp.stack([s[k] for s in st_p])
    stack_s = lambda k: jnp.stack([s[k] for s in st_s])
    fox_kv_p = jnp.transpose(stack_p('fox_kv').reshape(depth, B, 2, N_HEADS, HEAD_DIM, S), (0, 1, 5, 2, 3, 4))
    fox_logf_p = jnp.transpose(stack_p('fox_logf'), (0, 1, 3, 2))
    nsa_kv_p = jnp.transpose(stack_p('nsa_kv').reshape(depth, B, 4, HEAD_DIM, S), (0, 1, 4, 2, 3))
    nsa_win_p = jnp.transpose(stack_p('nsa_win').reshape(depth, B, 2, HEAD_DIM, -1), (0, 1, 4, 2, 3))
    return (xp, xs, fox_kv_p, stack_s('fox_kv'), fox_logf_p, stack_s('fox_logf'),
            nsa_kv_p, stack_s('nsa_kv'), nsa_win_p, stack_s('nsa_win'),
            stack_p('conv'), stack_s('conv'), stack_s('gmlp_v'))
```

```python
import functools

import jax
import jax.numpy as jnp
import numpy as np
from jax import lax
from jax.experimental import pallas as pl
from jax.experimental.pallas import tpu as pltpu

F32 = jnp.float32
BF16 = jnp.bfloat16

HEAD_DIM = 64
HALF = HEAD_DIM // 2
GROUP_WIDTH = 256
N_HEADS = GROUP_WIDTH // HEAD_DIM
CONV_WIDTH = 31
CMP_BLOCK = 32
SEL_BLOCK = 64
N_SELECT = 16
WINDOW = 512
CHUNK = 128
ROPE_THETA = 10000.0
EPS = 1e-6
FORCED_SCORE = 1e4
Q_SCALE = HEAD_DIM ** -0.5
NEG_BIG = -1e30
SEL_NEG = -32768.0
LANES = 128
VMEM_LIMIT = 56 * 1024 * 1024


def _cparams(sem):
    return pltpu.CompilerParams(dimension_semantics=sem, vmem_limit_bytes=VMEM_LIMIT)


def _nt(a, b):
    return lax.dot_general(a, b, (((1,), (1,)), ((), ())), preferred_element_type=F32)


def _dot(a, b):
    return jnp.dot(a, b, preferred_element_type=F32)


def _split3(x):
    h = x.astype(BF16).astype(F32)
    r = x - h
    m = r.astype(BF16).astype(F32)
    l = (r - m).astype(BF16).astype(F32)
    return h, m, l


def _log_sigmoid(x):
    return jnp.minimum(x, 0.0) - jnp.log1p(jnp.exp(-jnp.abs(x)))


def _group_mean_sq(x, gsum):
    x2 = x * x
    hi = x2.astype(BF16)
    lo = (x2 - hi.astype(F32)).astype(BF16)
    return (_dot(hi, gsum) + _dot(lo, gsum)) * (1.0 / HEAD_DIM)


def _rope_rows(x, cos, sin_signed):
    lane = lax.broadcasted_iota(jnp.int32, x.shape, 1)
    first_half = (lane % HEAD_DIM) < HALF
    swapped = jnp.where(first_half, pltpu.roll(x, LANES - HALF, 1), pltpu.roll(x, HALF, 1))
    return x * cos + swapped * sin_signed


_R_QA, _R_GLU, _R_QC, _R_UD, _R_VD, _R_SMALL, _R_END = 0, 256, 768, 1024, 1280, 1536, 1664
_C_KA, _C_VA, _C_KVC, _C_FA, _C_END = 0, 256, 512, 896, 912


def _proj_body(x_ref, g1_ref, wrow_ref, wcol_ref, cosr_ref, sinr_ref, cost_ref, sint_ref,
               gsum_ref, gqa_ref, gka_ref, bf_ref, gqc_ref, gkc_ref, lng_ref, lnb_ref,
               wm_ref, bstab_ref, utri_ref,
               qa_ref, foxkv_ref, kaug_ref, vbf_ref, logf_ref, glu_ref, qc_ref, nsakv_ref,
               nsawin_ref, nsabf_ref, cmprow_ref, gc_ref, od_ref, vn_ref, qat_ref, karow_ref, qct_ref, ksrow_ref,
               carry_ref, *, chunk):
    tm = x_ref.shape[1]

    @pl.when(pl.program_id(1) == 0)
    def _():
        carry_ref[...] = jnp.zeros_like(carry_ref)

    x = x_ref[0]
    ms = jnp.mean(x * x, axis=-1, keepdims=True)
    xn = ((x * lax.rsqrt(ms + EPS)) * g1_ref[...]).astype(BF16)
    zr = _dot(xn, wrow_ref[...])
    zc = _nt(wcol_ref[...], xn)
    gsum = gsum_ref[...]

    qa = zr[:, _R_QA:_R_QA + 256]
    qa = qa * lax.rsqrt(_group_mean_sq(qa, gsum) + EPS) * gqa_ref[...] * Q_SCALE
    lane = lax.broadcasted_iota(jnp.int32, (tm, LANES), 1)
    for h in range(N_HEADS):
        src = qa[:, (h // 2) * LANES:(h // 2 + 1) * LANES]
        if h % 2 == 0:
            aug = jnp.where(lane < HEAD_DIM, src, jnp.where(lane < HEAD_DIM + 3, 1.0, 0.0))
        else:
            aug = jnp.where(lane >= HEAD_DIM, src, jnp.where(lane < 3, 1.0, 0.0))
        qa_ref[0, :, h * LANES:(h + 1) * LANES] = aug.astype(BF16)
        qat_ref[0, h] = jnp.transpose(aug).astype(BF16)

    logf = _log_sigmoid(zc[_C_FA:_C_FA + 8] + bf_ref[...])
    logf_ref[0] = logf[0:N_HEADS]
    parts = _split3(logf)
    l3 = jnp.concatenate(parts, axis=0).astype(BF16)
    cs = _dot(l3, utri_ref[...])
    fcum = cs[0:8] + cs[8:16] + cs[16:24] + carry_ref[:, 0:1]
    carry_ref[...] = jnp.broadcast_to(fcum[:, tm - 1:tm], carry_ref.shape)
    nfh, nfm, nfl = _split3(-fcum)
    row8 = lax.broadcasted_iota(jnp.int32, (8, tm), 0)
    zeros56 = jnp.zeros((HEAD_DIM - 8, tm), F32)
    gka = gka_ref[:, 0:1]
    for h in range(N_HEADS):
        k = zc[_C_KA + h * HEAD_DIM:_C_KA + (h + 1) * HEAD_DIM]
        k = k * lax.rsqrt(jnp.mean(k * k, axis=0, keepdims=True) + EPS) * gka
        v = zc[_C_VA + h * HEAD_DIM:_C_VA + (h + 1) * HEAD_DIM]
        foxkv_ref[0, h * HEAD_DIM:(h + 1) * HEAD_DIM, :] = k
        foxkv_ref[0, GROUP_WIDTH + h * HEAD_DIM:GROUP_WIDTH + (h + 1) * HEAD_DIM, :] = v
        vbf_ref[0, h * HEAD_DIM:(h + 1) * HEAD_DIM, :] = v.astype(BF16)
        extra8 = jnp.where(row8 == 0, nfh[h:h + 1],
                           jnp.where(row8 == 1, nfm[h:h + 1],
                                     jnp.where(row8 == 2, nfl[h:h + 1], 0.0)))
        extra = jnp.concatenate([extra8, zeros56], axis=0)
        pieces = [k, extra] if h % 2 == 0 else [extra, k]
        kaug = jnp.concatenate(pieces, axis=0)
        kaug_ref[0, h] = kaug.astype(BF16)
        karow_ref[0, :, h * LANES:(h + 1) * LANES] = jnp.transpose(kaug).astype(BF16)

    glu_in = zr[:, _R_GLU:_R_GLU + 512]
    glu_ref[0] = glu_in[:, :256] * jax.nn.sigmoid(glu_in[:, 256:])

    qc = zr[:, _R_QC:_R_QC + 256]
    qc = qc * lax.rsqrt(_group_mean_sq(qc, gsum) + EPS) * gqc_ref[...]
    cosr, sinr = cosr_ref[...], sinr_ref[...]
    for p in range(2):
        qh = _rope_rows(qc[:, p * LANES:(p + 1) * LANES], cosr, sinr) * Q_SCALE
        qc_ref[0, :, p * LANES:(p + 1) * LANES] = qh.astype(BF16)
        qct_ref[0, p * LANES:(p + 1) * LANES, :] = jnp.transpose(qh).astype(BF16)
    gc_ref[0] = jax.nn.sigmoid(zr[:, _R_SMALL:_R_SMALL + LANES])
    cost, sint = cost_ref[...], sint_ref[...]
    keys = []
    for b in range(3):
        kb = zc[_C_KVC + 2 * b * HEAD_DIM:_C_KVC + (2 * b + 1) * HEAD_DIM]
        kb = kb * lax.rsqrt(jnp.mean(kb * kb, axis=0, keepdims=True) + EPS) * gkc_ref[:, b:b + 1]
        x1, x2 = kb[:HALF], kb[HALF:]
        keys.append(jnp.concatenate([x1 * cost - x2 * sint, x2 * cost + x1 * sint], axis=0))
    vals = [zc[_C_KVC + (2 * b + 1) * HEAD_DIM:_C_KVC + (2 * b + 2) * HEAD_DIM] for b in range(3)]
    nsakv_ref[0, 0:64, :] = keys[0]
    nsakv_ref[0, 64:128, :] = vals[0]
    nsakv_ref[0, 128:192, :] = keys[1]
    nsakv_ref[0, 192:256, :] = vals[1]
    nsawin_ref[0, 0:64, :] = keys[2]
    nsawin_ref[0, 64:128, :] = vals[2]
    nsabf_ref[0, 0:64, :] = keys[1].astype(BF16)
    nsabf_ref[0, 64:128, :] = vals[1].astype(BF16)
    nsabf_ref[0, 128:192, :] = keys[2].astype(BF16)
    nsabf_ref[0, 192:256, :] = vals[2].astype(BF16)
    cmprow_ref[0] = jnp.transpose(jnp.concatenate([keys[0], vals[0]], axis=0))
    ksrow_ref[0, :, 0:LANES] = jnp.transpose(jnp.concatenate([keys[1], jnp.zeros_like(keys[1])], axis=0)).astype(BF16)
    blk = (pl.program_id(1) * tm + lax.broadcasted_iota(jnp.int32, (tm, LANES), 0)) // SEL_BLOCK
    ksrow_ref[0, :, LANES:] = jnp.where(blk == lane, 1.0, 0.0).astype(BF16)

    ud = zr[:, _R_UD:_R_UD + 256]
    vd = zr[:, _R_VD:_R_VD + 256]
    mu = jnp.mean(vd, axis=-1, keepdims=True)
    var = jnp.mean(jnp.square(vd - mu), axis=-1, keepdims=True)
    vn = (vd - mu) * lax.rsqrt(var + EPS) * lng_ref[...] + lnb_ref[...]
    vn_ref[0] = vn
    grp = lax.broadcasted_iota(jnp.int32, (chunk, GROUP_WIDTH), 1) // HEAD_DIM
    wm = wm_ref[...]
    for c in range(tm // chunk):
        r = _dot(wm, vn[c * chunk:(c + 1) * chunk].astype(BF16))
        mixed = bstab_ref[...]
        for g in range(N_HEADS):
            mixed = mixed + jnp.where(grp == g, r[g * chunk:(g + 1) * chunk], 0.0)
        od_ref[0, c * chunk:(c + 1) * chunk, :] = ud[c * chunk:(c + 1) * chunk] * mixed


def _proj(x, w, tabs, *, tm, chunk):
    B, S, D = x.shape
    ns = S // tm
    row = lambda width: pl.BlockSpec((1, tm, width), lambda b, i: (b, i, 0))
    col = lambda height: pl.BlockSpec((1, height, tm), lambda b, i: (b, 0, i))
    full = lambda a: pl.BlockSpec(a.shape, lambda b, i: (0,) * a.ndim)
    ins = [x, w['norm1_g'], w['w_row'], w['w_col'], tabs['cos_r'], tabs['sin_r'], tabs['cos_t'], tabs['sin_t'],
           tabs['gsum'], w['gqa'], w['gka'], w['bf'], w['gqc'], w['gkc'], w['gmlp_ln_g'], w['gmlp_ln_b'],
           tabs['wm'], tabs['bs_tab'], tabs['utri']]
    in_specs = [row(D), full(ins[1]), full(ins[2]), full(ins[3]),
                pl.BlockSpec((tm, LANES), lambda b, i: (i, 0)), pl.BlockSpec((tm, LANES), lambda b, i: (i, 0)),
                pl.BlockSpec((HALF, tm), lambda b, i: (0, i)), pl.BlockSpec((HALF, tm), lambda b, i: (0, i))]
    in_specs += [full(a) for a in ins[8:]]
    outs = dict(
        qa=(jax.ShapeDtypeStruct((B, S, 4 * LANES), BF16), row(4 * LANES)),
        foxkv=(jax.ShapeDtypeStruct((B, 2 * GROUP_WIDTH, S), F32), col(2 * GROUP_WIDTH)),
        kaug=(jax.ShapeDtypeStruct((B, N_HEADS, LANES, S), BF16),
              pl.BlockSpec((1, N_HEADS, LANES, tm), lambda b, i: (b, 0, 0, i))),
        vbf=(jax.ShapeDtypeStruct((B, GROUP_WIDTH, S), BF16), col(GROUP_WIDTH)),
        logf=(jax.ShapeDtypeStruct((B, N_HEADS, S), F32), col(N_HEADS)),
        glu=(jax.ShapeDtypeStruct((B, S, GROUP_WIDTH), F32), row(GROUP_WIDTH)),
        qc=(jax.ShapeDtypeStruct((B, S, GROUP_WIDTH), BF16), row(GROUP_WIDTH)),
        nsakv=(jax.ShapeDtypeStruct((B, 256, S), F32), col(256)),
        nsawin=(jax.ShapeDtypeStruct((B, 128, S), F32), col(128)),
        nsabf=(jax.ShapeDtypeStruct((B, 256, S), BF16), col(256)),
        cmprow=(jax.ShapeDtypeStruct((B, S, LANES), F32), row(LANES)),
        gc=(jax.ShapeDtypeStruct((B, S, LANES), F32), row(LANES)),
        od=(jax.ShapeDtypeStruct((B, S, GROUP_WIDTH), F32), row(GROUP_WIDTH)),
        vn=(jax.ShapeDtypeStruct((B, S, GROUP_WIDTH), F32), row(GROUP_WIDTH)),
        qat=(jax.ShapeDtypeStruct((B, N_HEADS, LANES, S), BF16),
             pl.BlockSpec((1, N_HEADS, LANES, tm), lambda b, i: (b, 0, 0, i))),
        karow=(jax.ShapeDtypeStruct((B, S, 4 * LANES), BF16), row(4 * LANES)),
        qct=(jax.ShapeDtypeStruct((B, GROUP_WIDTH, S), BF16), col(GROUP_WIDTH)),
        ksrow=(jax.ShapeDtypeStruct((B, S, 2 * LANES), BF16), row(2 * LANES)),
    )
    names = list(outs)
    res = pl.pallas_call(
        functools.partial(_proj_body, chunk=chunk),
        grid=(B, ns),
        in_specs=in_specs,
        out_specs=[outs[n][1] for n in names],
        out_shape=[outs[n][0] for n in names],
        scratch_shapes=[pltpu.VMEM((8, LANES), F32)],
        compiler_params=_cparams(("arbitrary", "arbitrary")),
        name="proj",
    )(*ins)
    return dict(zip(names, res))


def _online_softmax_step(s, v, carry):
    m, l, acc = carry
    m_new = jnp.maximum(m, jnp.max(s, axis=-1, keepdims=True))
    alpha = jnp.exp(m - m_new)
    p = jnp.exp(s - m_new)
    l = alpha * l + jnp.sum(p, axis=-1, keepdims=True)
    acc = alpha * acc + _nt(p.astype(BF16), v)
    return m_new, l, acc


def _softmax_init(rows, dv):
    return (jnp.full((rows, 1), NEG_BIG, F32), jnp.zeros((rows, 1), F32), jnp.zeros((rows, dv), F32))


def _fox_attn_body(q_ref, k_ref, v_ref, o_ref):
    t = q_ref.shape[3]
    i = pl.program_id(1)
    key = lax.broadcasted_iota(jnp.int32, (t, t), 0)
    qry = lax.broadcasted_iota(jnp.int32, (t, t), 1)

    def tile(kt, states, masked):
        start = pl.multiple_of(kt * t, t)
        scores = [_dot(k_ref[0, pl.ds(start, t), h * LANES:(h + 1) * LANES], q_ref[0, h])
                  for h in range(N_HEADS)]
        out = []
        for h in range(N_HEADS):
            m, l, acc = states[h]
            s = jnp.where(key <= qry, scores[h], NEG_BIG) if masked else scores[h]
            m_new = jnp.maximum(m, jnp.max(s, axis=0, keepdims=True))
            alpha = jnp.exp(m - m_new)
            p = jnp.exp(s - m_new)
            l = alpha * l + jnp.sum(p, axis=0, keepdims=True)
            v = v_ref[0, h * HEAD_DIM:(h + 1) * HEAD_DIM, pl.ds(start, t)]
            out.append((m_new, l, alpha * acc + _dot(v, p.astype(BF16))))
        return tuple(out)

    init = tuple((jnp.full((1, t), NEG_BIG, F32), jnp.zeros((1, t), F32), jnp.zeros((HEAD_DIM, t), F32))
                 for _ in range(N_HEADS))
    states = tile(i, lax.fori_loop(0, i, functools.partial(tile, masked=False), init), True)
    o_ref[0] = jnp.concatenate([jnp.transpose(acc / l) for _, l, acc in states], axis=-1)


def _fox_attn(qat, karow, vbf, *, t):
    B, S, _ = karow.shape
    return pl.pallas_call(
        _fox_attn_body,
        grid=(B, S // t),
        in_specs=[pl.BlockSpec((1, N_HEADS, LANES, t), lambda b, i: (b, 0, 0, i)),
                  pl.BlockSpec((1, S, 4 * LANES), lambda b, i: (b, 0, 0)),
                  pl.BlockSpec((1, GROUP_WIDTH, S), lambda b, i: (b, 0, 0))],
        out_specs=pl.BlockSpec((1, t, GROUP_WIDTH), lambda b, i: (b, i, 0)),
        out_shape=jax.ShapeDtypeStruct((B, S, GROUP_WIDTH), F32),
        compiler_params=_cparams(("arbitrary", "arbitrary")),
        name="fox_attn",
    )(qat, karow, vbf)


_HALO = 32


def _ln_silu(y, g, b):
    mu = jnp.mean(y, axis=-1, keepdims=True)
    var = jnp.mean(jnp.square(y - mu), axis=-1, keepdims=True)
    y = (y - mu) * lax.rsqrt(var + EPS) * g + b
    return y * jax.nn.sigmoid(y)


def _conv_prompt_body(cur_ref, halo_ref, w_ref, cb_ref, lng_ref, lnb_ref, o_ref, xin_ref):
    tm = cur_ref.shape[1]
    first = pl.program_id(1) == 0
    xin_ref[0:_HALO, :] = jnp.where(first, 0.0, halo_ref[0])
    xin_ref[_HALO:, :] = cur_ref[0]
    off = _HALO - (CONV_WIDTH - 1)
    acc = jnp.zeros((tm, GROUP_WIDTH), F32)
    for k in range(CONV_WIDTH):
        acc = acc + xin_ref[pl.ds(off + k, tm), :] * w_ref[k:k + 1, :]
    o_ref[0] = _ln_silu(acc + cb_ref[...], lng_ref[...], lnb_ref[...])


def _conv_prompt(glu, w, *, tm):
    B, S, C = glu.shape
    r = tm // _HALO
    full = lambda a: pl.BlockSpec(a.shape, lambda b, i: (0,) * a.ndim)
    ins = [glu, glu, w['conv_w'], w['conv_b'], w['conv_ln_g'], w['conv_ln_b']]
    return pl.pallas_call(
        _conv_prompt_body,
        grid=(B, S // tm),
        in_specs=[pl.BlockSpec((1, tm, C), lambda b, i: (b, i, 0)),
                  pl.BlockSpec((1, _HALO, C), lambda b, i: (b, jnp.maximum(i * r - 1, 0), 0))]
                 + [full(a) for a in ins[2:]],
        out_specs=pl.BlockSpec((1, tm, C), lambda b, i: (b, i, 0)),
        out_shape=jax.ShapeDtypeStruct((B, S, C), F32),
        scratch_shapes=[pltpu.VMEM((tm + _HALO, C), F32)],
        compiler_params=_cparams(("arbitrary", "arbitrary")),
        name="conv_prompt",
    )(*ins)


def _compress_body(x_ref, pe_ref, phi_ref, o_ref):
    o_ref[0] = _dot((x_ref[0] + pe_ref[...]).astype(BF16), phi_ref[...])


def _compress(blocks, pe_flat, phi):
    B, n, width = blocks.shape
    return pl.pallas_call(
        _compress_body,
        grid=(B,),
        in_specs=[pl.BlockSpec((1, n, width), lambda b: (b, 0, 0)),
                  pl.BlockSpec(pe_flat.shape, lambda b: (0, 0)),
                  pl.BlockSpec(phi.shape, lambda b: (0, 0))],
        out_specs=pl.BlockSpec((1, n, LANES), lambda b: (b, 0, 0)),
        out_shape=jax.ShapeDtypeStruct((B, n, LANES), F32),
        compiler_params=_cparams(("arbitrary",)),
        name="nsa_compress",
    )(blocks, pe_flat, phi)


def _masked_softmax(s, mask):
    s = jnp.where(mask, s, NEG_BIG)
    m = jnp.max(s, axis=-1, keepdims=True)
    e = jnp.where(mask, jnp.exp(s - m), 0.0)
    return e / jnp.maximum(jnp.sum(e, axis=-1, keepdims=True), 1e-30)


def _stack_heads(q):
    return jnp.concatenate([q[:, h * HEAD_DIM:(h + 1) * HEAD_DIM] for h in range(N_HEADS)], axis=0)


def _select_blocks(imp, cur, n_select):
    j = lax.broadcasted_iota(jnp.int32, imp.shape, 1)
    forced = (j == 0) | (j == cur) | (j == cur - 1)
    v = jnp.where(forced, FORCED_SCORE, imp)
    v = jnp.where(j <= cur, v, -1.0)
    jf = j.astype(F32)
    sel = jnp.zeros(imp.shape, jnp.bool_)
    for _ in range(n_select):
        m = jnp.max(v, axis=-1, keepdims=True)
        idx = jnp.min(jnp.where(v == m, jf, float(imp.shape[1])), axis=-1, keepdims=True)
        pick = jf == idx
        sel = sel | (pick & (m >= 0.0))
        v = jnp.where(pick, -2.0, v)
    return sel


def _nsa_local_body(q_ref, g_ref, kc_ref, vc_ref, kv_ref, o_ref, sbt_ref):
    t = q_ref.shape[1]
    nc = kc_ref.shape[2]
    half = nc // 2
    st = pl.program_id(1) * t
    qs = _stack_heads(q_ref[0])
    qpos = st + lax.broadcasted_iota(jnp.int32, (N_HEADS * t, 1), 0) % t

    c = lax.broadcasted_iota(jnp.int32, (1, nc), 1)
    blk = jnp.where(c < half, 2 * c, 2 * (c - half) + 1)
    p_cmp = _masked_softmax(_dot(qs, kc_ref[0]), (blk + 1) * CMP_BLOCK - 1 <= qpos)
    o_cmp = _dot(p_cmp.astype(BF16), vc_ref[0])
    imp = p_cmp[0:t] + p_cmp[t:2 * t] + p_cmp[2 * t:3 * t] + p_cmp[3 * t:4 * t]
    imp = imp[:, :half] + imp[:, half:]
    sel = _select_blocks(imp, qpos[0:t] // SEL_BLOCK, min(N_SELECT, half))
    sb = jnp.where(sel, 0.0, SEL_NEG)
    if half < LANES:
        sb = jnp.concatenate([sb, jnp.zeros((t, LANES - half), F32)], axis=1)
    sbt_ref[0] = jnp.transpose(sb).astype(BF16)

    span = WINDOW + t
    start = pl.multiple_of(jnp.maximum(st - WINDOW, 0), LANES)
    kwpos = start + lax.broadcasted_iota(jnp.int32, (1, span), 1)
    wmask = (kwpos <= qpos) & (qpos - kwpos < WINDOW)
    p_win = _masked_softmax(_dot(qs, kv_ref[0, 128:192, pl.ds(start, span)]), wmask)
    o_win = _nt(p_win.astype(BF16), kv_ref[0, 192:256, pl.ds(start, span)])

    g = g_ref[0]
    outs = [g[:, 3 * h:3 * h + 1] * o_cmp[h * t:(h + 1) * t] + g[:, 3 * h + 2:3 * h + 3] * o_win[h * t:(h + 1) * t]
            for h in range(N_HEADS)]
    o_ref[0] = jnp.concatenate(outs, axis=-1)


def _nsa_local(qc, gc, kc_t, vc, nsabf, *, t):
    B, S, _ = qc.shape
    nc = kc_t.shape[2]
    tile = lambda width: pl.BlockSpec((1, t, width), lambda b, i: (b, i, 0))
    return pl.pallas_call(
        _nsa_local_body,
        grid=(B, S // t),
        in_specs=[tile(GROUP_WIDTH), tile(LANES),
                  pl.BlockSpec((1, HEAD_DIM, nc), lambda b, i: (b, 0, 0)),
                  pl.BlockSpec((1, nc, HEAD_DIM), lambda b, i: (b, 0, 0)),
                  pl.BlockSpec((1, 256, S), lambda b, i: (b, 0, 0))],
        out_specs=[tile(GROUP_WIDTH), pl.BlockSpec((1, max(nc // 2, LANES), t), lambda b, i: (b, 0, i))],
        out_shape=[jax.ShapeDtypeStruct((B, S, GROUP_WIDTH), F32),
                   jax.ShapeDtypeStruct((B, max(nc // 2, LANES), S), BF16)],
        compiler_params=_cparams(("arbitrary", "arbitrary")),
        name="nsa_local",
    )(qc, gc, kc_t, vc, nsabf)


def _nsa_sel_body(qt_ref, sbt_ref, g_ref, part_ref, ks_ref, kv_ref, o_ref, *, tk):
    t = qt_ref.shape[2]
    st = pl.program_id(1) * t
    n_grp, per = N_HEADS, 1
    pad = jnp.zeros((LANES - HEAD_DIM, per * t), BF16)
    bias = jnp.concatenate([sbt_ref[0]] * per, axis=1)
    q2 = [jnp.concatenate([jnp.concatenate([qt_ref[0, h * HEAD_DIM:(h + 1) * HEAD_DIM, :]
                                            for h in range(c * per, (c + 1) * per)], axis=1), pad, bias], axis=0)
          for c in range(n_grp)]
    qpos = st + lax.broadcasted_iota(jnp.int32, (1, per * t), 1) % t

    def tile(kt, states, masked):
        start = pl.multiple_of(kt * tk, tk)
        keys = ks_ref[0, pl.ds(start, tk), :]
        scores = [_dot(keys, q2[c]) for c in range(n_grp)]
        v = kv_ref[0, HEAD_DIM:2 * HEAD_DIM, pl.ds(start, tk)]
        out = []
        for c in range(n_grp):
            m, l, acc = states[c]
            s = scores[c]
            if masked:
                kpos = start + lax.broadcasted_iota(jnp.int32, (tk, 1), 0)
                s = jnp.where(kpos <= qpos, s, NEG_BIG)
            m_new = jnp.maximum(m, jnp.max(s, axis=0, keepdims=True))
            alpha = jnp.exp(m - m_new)
            p = jnp.exp(s - m_new)
            l = alpha * l + jnp.sum(p, axis=0, keepdims=True)
            out.append((m_new, l, alpha * acc + _dot(v, p.astype(BF16))))
        return tuple(out)

    last = (st + t - 1) // tk
    init = tuple((jnp.full((1, per * t), NEG_BIG, F32), jnp.zeros((1, per * t), F32),
                  jnp.zeros((HEAD_DIM, per * t), F32)) for _ in range(n_grp))
    states = tile(last, lax.fori_loop(0, last, functools.partial(tile, masked=False), init), True)
    g = g_ref[0]
    outs = []
    for h in range(N_HEADS):
        _, l, acc = states[h // per]
        o_sel = jnp.transpose((acc / l)[:, (h % per) * t:(h % per + 1) * t])
        outs.append(g[:, 3 * h + 1:3 * h + 2] * o_sel)
    o_ref[0] = part_ref[0] + jnp.concatenate(outs, axis=-1)


def _nsa_sel(qct, selbt, gc, part, ksrow, nsabf, *, t, tk):
    B, _, S = qct.shape
    ns = selbt.shape[1]
    tile = lambda width: pl.BlockSpec((1, t, width), lambda b, i: (b, i, 0))
    return pl.pallas_call(
        functools.partial(_nsa_sel_body, tk=tk),
        grid=(B, S // t),
        in_specs=[pl.BlockSpec((1, GROUP_WIDTH, t), lambda b, i: (b, 0, i)),
                  pl.BlockSpec((1, ns, t), lambda b, i: (b, 0, i)),
                  tile(LANES), tile(GROUP_WIDTH),
                  pl.BlockSpec((1, S, 2 * LANES), lambda b, i: (b, 0, 0)),
                  pl.BlockSpec((1, 256, S), lambda b, i: (b, 0, 0))],
        out_specs=tile(GROUP_WIDTH),
        out_shape=jax.ShapeDtypeStruct((B, S, GROUP_WIDTH), F32),
        compiler_params=_cparams(("arbitrary", "arbitrary")),
        name="nsa_sel",
    )(qct, selbt, gc, part, ksrow, nsabf)


def _nsa_prompt(o, w, *, t, tk, tl):
    B, S, _ = o['qc'].shape
    n = S // CMP_BLOCK
    cmp = _compress(o['cmprow'].reshape(B, n, CMP_BLOCK * LANES), w['pe_flat'], w['phi'])
    kc_t, vc = _even_odd(cmp)
    part, selbias = _nsa_local(o['qc'], o['gc'], kc_t, vc, o['nsabf'], t=tl)
    assert S // SEL_BLOCK <= LANES
    return _nsa_sel(o['qct'], selbias, o['gc'], part, o['ksrow'], o['nsabf'], t=t, tk=tk)


def _merge_ffn_body(x_ref, oa_ref, ob_ref, oc_ref, od_ref, gn_ref, wout_ref, g2_ref, w1_ref, w2_ref,
                    y_ref, hn_ref, acc_ref):
    j = pl.program_id(1)

    @pl.when(j == 0)
    def _():
        h = x_ref[...]
        for i, o_ref in enumerate((oa_ref, ob_ref, oc_ref, od_ref)):
            o = o_ref[...]
            o = o * lax.rsqrt(jnp.mean(o * o, axis=-1, keepdims=True) + EPS) * gn_ref[i:i + 1, :]
            h = h + _dot(o.astype(BF16), wout_ref[i * GROUP_WIDTH:(i + 1) * GROUP_WIDTH, :])
        acc_ref[...] = h
        hn = h * lax.rsqrt(jnp.mean(h * h, axis=-1, keepdims=True) + EPS) * g2_ref[...]
        hn_ref[...] = hn.astype(BF16)

    u = jnp.maximum(_dot(hn_ref[...], w1_ref[...]), 0.0)
    acc_ref[...] += _dot((u * u).astype(BF16), w2_ref[...])

    @pl.when(j == pl.num_programs(1) - 1)
    def _():
        y_ref[...] = acc_ref[...]


def _merge_ffn(x, outs, w, *, tm, tf):
    R, D = x.shape
    F = w['w_ff1'].shape[1]
    rows = lambda width: pl.BlockSpec((tm, width), lambda i, j: (i, 0))
    full = lambda a: pl.BlockSpec(a.shape, lambda i, j: (0,) * a.ndim)
    return pl.pallas_call(
        _merge_ffn_body,
        grid=(R // tm, F // tf),
        in_specs=[rows(D)] + [rows(GROUP_WIDTH)] * 4 + [full(w['gnorm_g']), full(w['w_out']), full(w['norm2_g']),
                  pl.BlockSpec((D, tf), lambda i, j: (0, j)), pl.BlockSpec((tf, D), lambda i, j: (j, 0))],
        out_specs=rows(D),
        out_shape=jax.ShapeDtypeStruct((R, D), F32),
        scratch_shapes=[pltpu.VMEM((tm, D), BF16), pltpu.VMEM((tm, D), F32)],
        compiler_params=_cparams(("arbitrary", "arbitrary")),
        name="merge_ffn",
    )(x, *outs, w['gnorm_g'], w['w_out'], w['norm2_g'], w['w_ff1'], w['w_ff2'])


def _dot3(parts, rhs=None, lhs=None):
    if rhs is not None:
        return sum(_dot(p.astype(BF16), rhs) for p in parts)
    return sum(_dot(lhs, p.astype(BF16)) for p in parts)


def _page_scan_matrices(pg):
    r = np.arange(pg * N_HEADS)
    g, h = r // N_HEADS, r % N_HEADS
    s = np.arange(LANES)
    same = h[:, None] == h[None, :]
    return (jnp.asarray(s[:, None] >= s[None, :], BF16), jnp.asarray(same & (g[None, :] > g[:, None]), BF16),
            jnp.asarray(same, BF16))


def _same_seq_causal(b, t_new, n_cols, rows):
    col = lax.broadcasted_iota(jnp.int32, (rows, n_cols), 1)
    t = lax.broadcasted_iota(jnp.int32, (rows, n_cols), 0) % t_new
    return (col // t_new == b) & (col % t_new <= t)


def _fox_sample_body(pt_ref, q_ref, knew_ref, vnew_ref, tri_ref, later_ref, same_ref, kv_hbm, lf_hbm, o_ref,
                     kvbuf, lfbuf, sems, m_ref, l_ref, acc_ref, carry_ref, *, layer, pg):
    b, j = pl.program_id(0), pl.program_id(1)
    nb, nch = pl.num_programs(0), pl.num_programs(1)
    step = b * nch + j
    slot = step % 2
    t_new = q_ref.shape[0]
    rows = N_HEADS * t_new

    def copies(bb, jj, sl):
        first = (nch - 1 - jj) * pg
        out = []
        for g in range(pg):
            pid = pt_ref[bb, first + g]
            out.append(pltpu.make_async_copy(kv_hbm.at[layer, pid], kvbuf.at[sl, g], sems.at[0, sl]))
            out.append(pltpu.make_async_copy(lf_hbm.at[layer, pid], lfbuf.at[sl, pl.ds(g * N_HEADS, N_HEADS)],
                                             sems.at[1, sl]))
        return out

    @pl.when(step == 0)
    def _():
        for c in copies(b, j, slot):
            c.start()

    @pl.when(step + 1 < nb * nch)
    def _():
        nxt = step + 1
        for c in copies(nxt // nch, nxt % nch, 1 - slot):
            c.start()

    @pl.when(j == 0)
    def _():
        m_ref[...] = jnp.full_like(m_ref, NEG_BIG)
        l_ref[...] = jnp.zeros_like(l_ref)
        acc_ref[...] = jnp.zeros_like(acc_ref)
        carry_ref[...] = jnp.zeros_like(carry_ref)

    qv = q_ref[...]
    lane = lax.broadcasted_iota(jnp.int32, (t_new, LANES), 1)
    q4 = jnp.concatenate([jnp.where(lane < HEAD_DIM, qv[:, 0:128], qv[:, 128:256]),
                          jnp.where(lane < HEAD_DIM, qv[:, 256:384], qv[:, 384:512])], axis=-1)
    grp = lax.broadcasted_iota(jnp.int32, (t_new, GROUP_WIDTH), 1) // HEAD_DIM
    wq = jnp.concatenate([jnp.where(grp == h, q4, 0.0) for h in range(N_HEADS)], axis=0).astype(BF16)

    for c in copies(b, j, slot):
        c.wait()

    lf = lfbuf[slot]
    incl = _dot3(_split3(lf), rhs=tri_ref[...])
    tot = _split3(jnp.broadcast_to(incl[:, 0:1], lf.shape))
    suf = incl - lf + _dot3(tot, lhs=later_ref[...]) + carry_ref[...]
    carry_ref[...] += _dot3(tot, lhs=same_ref[...])

    scores = []
    for g in range(pg):
        bias = jnp.concatenate([jnp.broadcast_to(suf[g * N_HEADS + h:g * N_HEADS + h + 1], (t_new, LANES))
                                for h in range(N_HEADS)], axis=0)
        scores.append(_dot(wq, kvbuf[slot, g, 0:GROUP_WIDTH, :].astype(BF16)) + bias)
    m_old = m_ref[...]
    m = jnp.maximum(m_old, jnp.max(functools.reduce(jnp.maximum, scores), axis=-1, keepdims=True))
    alpha = jnp.exp(m_old - m)
    psum = jnp.zeros((rows, LANES), F32)
    acc = alpha * acc_ref[...]
    for g in range(pg):
        p = jnp.exp(scores[g] - m)
        psum = psum + p
        acc = acc + _nt(p.astype(BF16), kvbuf[slot, g, GROUP_WIDTH:, :].astype(BF16))
    l = alpha * l_ref[...] + jnp.sum(psum, axis=-1, keepdims=True)
    m_ref[...], l_ref[...], acc_ref[...] = m, l, acc

    @pl.when(j == nch - 1)
    def _():
        n_cols = knew_ref.shape[2]
        s = jnp.concatenate([_dot(qv[:, h * LANES:(h + 1) * LANES].astype(BF16), knew_ref[h])
                             for h in range(N_HEADS)], axis=0)
        s = jnp.where(_same_seq_causal(b, t_new, n_cols, rows), s, NEG_BIG)
        m2, l2, acc2 = _online_softmax_step(s, vnew_ref[...], (m, l, acc))
        o = acc2 / l2
        o_ref[...] = jnp.concatenate(
            [o[h * t_new:(h + 1) * t_new, h * HEAD_DIM:(h + 1) * HEAD_DIM] for h in range(N_HEADS)], axis=-1)


def _fox_sample(layer, page_table, q_aug, kaug_new, v_new, cache_kv, cache_lf, *, t_new, pg):
    db, n_pages = page_table.shape
    rows = N_HEADS * t_new
    mats = _page_scan_matrices(pg)
    full = lambda a: pl.BlockSpec(a.shape, lambda b, j, pt: (0,) * a.ndim)
    grid_spec = pltpu.PrefetchScalarGridSpec(
        num_scalar_prefetch=1,
        grid=(db, n_pages // pg),
        in_specs=[pl.BlockSpec((t_new, 4 * LANES), lambda b, j, pt: (b, 0)), full(kaug_new), full(v_new)]
                 + [full(a) for a in mats] + [pl.BlockSpec(memory_space=pl.ANY), pl.BlockSpec(memory_space=pl.ANY)],
        out_specs=pl.BlockSpec((t_new, GROUP_WIDTH), lambda b, j, pt: (b, 0)),
        scratch_shapes=[pltpu.VMEM((2, pg) + cache_kv.shape[2:], F32), pltpu.VMEM((2, pg * N_HEADS, LANES), F32),
                        pltpu.SemaphoreType.DMA((2, 2)),
                        pltpu.VMEM((rows, 1), F32), pltpu.VMEM((rows, 1), F32), pltpu.VMEM((rows, GROUP_WIDTH), F32),
                        pltpu.VMEM((pg * N_HEADS, LANES), F32)])
    return pl.pallas_call(
        functools.partial(_fox_sample_body, layer=layer, pg=pg),
        grid_spec=grid_spec,
        out_shape=jax.ShapeDtypeStruct((db * t_new, GROUP_WIDTH), F32),
        compiler_params=_cparams(("arbitrary", "arbitrary")),
        name="fox_sample",
    )(page_table, q_aug, kaug_new, v_new, *mats, cache_kv, cache_lf)


def _conv_sample_body(state_ref, glu_ref, w_ref, cb_ref, lng_ref, lnb_ref, o_ref, new_ref):
    n_state, t_new = state_ref.shape[1], glu_ref.shape[0]
    x = lambda i: state_ref[0, i] if i < n_state else glu_ref[i - n_state]
    for t in range(t_new):
        acc = x(t) * w_ref[0:1, :]
        for k in range(1, CONV_WIDTH):
            acc = acc + x(t + k) * w_ref[k:k + 1, :]
        o_ref[t] = _ln_silu(acc + cb_ref[...], lng_ref[...], lnb_ref[...])
    for i in range(n_state):
        new_ref[i] = x(i + t_new)


def _conv_sample(layer, state_t, glu_t, w):
    _, n_state, db, c = state_t.shape
    t_new = glu_t.shape[0]
    full = lambda a: pl.BlockSpec(a.shape, lambda i: (0,) * a.ndim)
    ins = [state_t, glu_t, w['conv_w'], w['conv_b'], w['conv_ln_g'], w['conv_ln_b']]
    return pl.pallas_call(
        _conv_sample_body,
        grid=(1,),
        in_specs=[pl.BlockSpec((1, n_state, db, c), lambda i: (layer, 0, 0, 0))] + [full(a) for a in ins[1:]],
        out_specs=[pl.BlockSpec((t_new, db, c), lambda i: (0, 0, 0)), pl.BlockSpec((n_state, db, c), lambda i: (0, 0, 0))],
        out_shape=[jax.ShapeDtypeStruct((t_new, db, c), F32), jax.ShapeDtypeStruct((n_state, db, c), F32)],
        compiler_params=_cparams(("arbitrary",)),
        name="conv_sample",
    )(*ins)


_CMP_PER_PAGE = 4


def _compress_sample_body(pt_ref, *refs, n_group):
    x_refs, (pe_ref, phi_ref, o_ref, x_scr) = refs[:n_group], refs[n_group:]
    page = x_scr.shape[0] // n_group
    for g, x_ref in enumerate(x_refs):
        x_scr[g * page:(g + 1) * page, :] = jnp.transpose(x_ref[0, 0] + pe_ref[...])
    n_blocks = n_group * _CMP_PER_PAGE
    acc = jnp.zeros((n_blocks, LANES), F32)
    for r in range(CMP_BLOCK):
        rows = x_scr[pl.ds(r, n_blocks, stride=CMP_BLOCK), :]
        acc = acc + _dot(rows.astype(BF16), phi_ref[r * LANES:(r + 1) * LANES, :])
    o_ref[0] = acc


def _compress_sample(layer, page_table, cache_nsa, pe_col, phi, *, n_group):
    db, n_pages = page_table.shape
    page = cache_nsa.shape[3]
    page_spec = lambda g: pl.BlockSpec((1, 1, LANES, page), lambda b, j, pt: (layer, pt[b, j * n_group + g], 0, 0))
    rows = n_group * _CMP_PER_PAGE
    grid_spec = pltpu.PrefetchScalarGridSpec(
        num_scalar_prefetch=1,
        grid=(db, n_pages // n_group),
        in_specs=[page_spec(g) for g in range(n_group)]
                 + [pl.BlockSpec(pe_col.shape, lambda b, j, pt: (0, 0)), pl.BlockSpec(phi.shape, lambda b, j, pt: (0, 0))],
        out_specs=pl.BlockSpec((1, rows, LANES), lambda b, j, pt: (b, j, 0)),
        scratch_shapes=[pltpu.VMEM((n_group * page, LANES), F32)])
    return pl.pallas_call(
        functools.partial(_compress_sample_body, n_group=n_group),
        grid_spec=grid_spec,
        out_shape=jax.ShapeDtypeStruct((db, n_pages * _CMP_PER_PAGE, LANES), F32),
        compiler_params=_cparams(("arbitrary", "arbitrary")),
        name="nsa_compress_sample",
    )(page_table, *([cache_nsa] * n_group), pe_col, phi)


def _nsa_local_sample_body(q_ref, g_ref, kc_ref, vc_ref, win_ref, new_ref, o_ref, flag_ref, *, past_len):
    b = pl.program_id(0)
    t_new = q_ref.shape[0]
    rows = N_HEADS * t_new
    nc = kc_ref.shape[2]
    half = nc // 2
    qs = _stack_heads(q_ref[...]).astype(BF16)
    tq = lax.broadcasted_iota(jnp.int32, (rows, 1), 0) % t_new
    qpos = past_len + tq

    c = lax.broadcasted_iota(jnp.int32, (1, nc), 1)
    blk = jnp.where(c < half, 2 * c, 2 * (c - half) + 1)
    p_cmp = _masked_softmax(_dot(qs, kc_ref[0]), (blk + 1) * CMP_BLOCK - 1 <= qpos)
    o_cmp = _dot(p_cmp.astype(BF16), vc_ref[0])
    imp = p_cmp[0:t_new]
    for h in range(1, N_HEADS):
        imp = imp + p_cmp[h * t_new:(h + 1) * t_new]
    imp = imp[:, :half] + imp[:, half:]
    cur = jnp.full((t_new, 1), past_len // SEL_BLOCK, jnp.int32)
    sel = _select_blocks(imp, cur, min(N_SELECT, half + 1) - 1)
    j = lax.broadcasted_iota(jnp.int32, sel.shape, 1)
    t = lax.broadcasted_iota(jnp.int32, sel.shape, 0)
    weight = jnp.left_shift(1, 2 * t + j % 2).astype(F32)
    colsum = jnp.sum(jnp.where(sel, weight, 0.0), axis=0, keepdims=True)
    flag_ref[0] = colsum + pltpu.roll(colsum, half - 1, 1)

    w = win_ref.shape[3]
    n_cols = new_ref.shape[1]
    s_old = _dot(qs, win_ref[0, 0, 0:HEAD_DIM, :].astype(BF16))
    s_new = _dot(qs, new_ref[128:192, :])
    i_old = lax.broadcasted_iota(jnp.int32, (1, w), 1)
    mask = jnp.concatenate([jnp.broadcast_to(i_old + (WINDOW - w) > tq, (rows, w)),
                            _same_seq_causal(b, t_new, n_cols, rows)], axis=-1)
    p_win = _masked_softmax(jnp.concatenate([s_old, s_new], axis=-1), mask).astype(BF16)
    o_win = _nt(p_win[:, :w], win_ref[0, 0, HEAD_DIM:, :].astype(BF16)) + _nt(p_win[:, w:], new_ref[192:256, :])

    g = g_ref[...]
    outs = [g[:, 3 * h:3 * h + 1] * o_cmp[h * t_new:(h + 1) * t_new]
            + g[:, 3 * h + 2:3 * h + 3] * o_win[h * t_new:(h + 1) * t_new] for h in range(N_HEADS)]
    o_ref[...] = jnp.concatenate(outs, axis=-1)


def _nsa_local_sample(layer, qc, gc, kc_t, vc, win_t, new_bf, *, t_new, past_len):
    db = kc_t.shape[0]
    nc = kc_t.shape[2]
    w = win_t.shape[3]
    rows = lambda width: pl.BlockSpec((t_new, width), lambda b: (b, 0))
    return pl.pallas_call(
        functools.partial(_nsa_local_sample_body, past_len=past_len),
        grid=(db,),
        in_specs=[rows(GROUP_WIDTH), rows(LANES),
                  pl.BlockSpec((1, HEAD_DIM, nc), lambda b: (b, 0, 0)),
                  pl.BlockSpec((1, nc, HEAD_DIM), lambda b: (b, 0, 0)),
                  pl.BlockSpec((1, 1, 2 * HEAD_DIM, w), lambda b: (layer, b, 0, 0)),
                  pl.BlockSpec(new_bf.shape, lambda b: (0, 0))],
        out_specs=[rows(GROUP_WIDTH), pl.BlockSpec((1, 1, nc // 2), lambda b: (b, 0, 0))],
        out_shape=[jax.ShapeDtypeStruct((db * t_new, GROUP_WIDTH), F32),
                   jax.ShapeDtypeStruct((db, 1, nc // 2), F32)],
        compiler_params=_cparams(("arbitrary",)),
        name="nsa_local_sample",
    )(qc, gc, kc_t, vc, win_t, new_bf)


_SEL_GROUP = 4


def _nsa_sel_sample_body(pt_ref, fl_ref, q_ref, g_ref, part_ref, new_ref, kv_hbm, o_ref,
                         buf, sems, cflag, count, *, layer):
    b = pl.program_id(0)
    nb = pl.num_programs(0)
    n_pages = fl_ref.shape[1]
    slot = b % 2
    t_new = q_ref.shape[0]
    rows = N_HEADS * t_new

    def copy(bb, p, sl, k):
        src = kv_hbm.at[layer, pt_ref[bb, p], pl.ds(2 * HEAD_DIM, 2 * HEAD_DIM)]
        return pltpu.make_async_copy(src, buf.at[sl, k], sems.at[sl])

    def start_all(bb, sl):
        def body(p, k):
            flag = fl_ref[bb, p]

            @pl.when(flag != 0)
            def _():
                copy(bb, p, sl, k).start()
                cflag[sl, k] = flag
            return k + (flag != 0).astype(jnp.int32)
        count[sl] = lax.fori_loop(0, n_pages, body, 0)

    @pl.when(b == 0)
    def _():
        start_all(b, slot)

    @pl.when(b + 1 < nb)
    def _():
        start_all(b + 1, 1 - slot)

    qs = _stack_heads(q_ref[...]).astype(BF16)
    shamt = (2 * (lax.broadcasted_iota(jnp.int32, (rows, LANES), 0) % t_new)
             + lax.broadcasted_iota(jnp.int32, (rows, LANES), 1) // SEL_BLOCK)
    n_fetched = count[slot]

    def wait_body(k, c):
        copy(b, 0, slot, k).wait()
        return c
    lax.fori_loop(0, n_fetched, wait_body, 0)

    def group(gi, carry):
        m_old, l_old, acc = carry
        scores, entries = [], []
        for u in range(_SEL_GROUP):
            k = gi * _SEL_GROUP + u
            entry = jnp.minimum(k, n_fetched - 1)
            flag = jnp.where(k < n_fetched, cflag[slot, entry], 0)
            picked = (jnp.right_shift(jnp.full((rows, LANES), flag, jnp.int32), shamt) & 1) == 1
            scores.append(jnp.where(picked, _dot(qs, buf[slot, entry, 0:HEAD_DIM, :].astype(BF16)), NEG_BIG))
            entries.append(entry)
        m = jnp.maximum(m_old, jnp.max(functools.reduce(jnp.maximum, scores), axis=-1, keepdims=True))
        alpha = jnp.exp(m_old - m)
        psum = jnp.zeros((rows, LANES), F32)
        acc = alpha * acc
        for u in range(_SEL_GROUP):
            p = jnp.exp(scores[u] - m)
            psum = psum + p
            acc = acc + _nt(p.astype(BF16), buf[slot, entries[u], HEAD_DIM:, :].astype(BF16))
        return m, alpha * l_old + jnp.sum(psum, axis=-1, keepdims=True), acc

    carry = lax.fori_loop(0, (n_fetched + _SEL_GROUP - 1) // _SEL_GROUP, group, _softmax_init(rows, HEAD_DIM))
    s = jnp.where(_same_seq_causal(b, t_new, new_ref.shape[1], rows), _dot(qs, new_ref[0:64, :]), NEG_BIG)
    m, l, acc = _online_softmax_step(s, new_ref[64:128, :], carry)
    o_sel = acc / l
    g = g_ref[...]
    outs = [g[:, 3 * h + 1:3 * h + 2] * o_sel[h * t_new:(h + 1) * t_new] for h in range(N_HEADS)]
    o_ref[...] = part_ref[...] + jnp.concatenate(outs, axis=-1)


def _nsa_sel_sample(layer, page_table, flags, qc, gc, part, new_bf, cache_nsa, *, t_new):
    db, n_pages = page_table.shape
    rows = N_HEADS * t_new
    tile = lambda width: pl.BlockSpec((t_new, width), lambda b, pt, fl: (b, 0))
    grid_spec = pltpu.PrefetchScalarGridSpec(
        num_scalar_prefetch=2,
        grid=(db,),
        in_specs=[tile(GROUP_WIDTH), tile(LANES), tile(GROUP_WIDTH),
                  pl.BlockSpec(new_bf.shape, lambda b, pt, fl: (0, 0)),
                  pl.BlockSpec(memory_space=pl.ANY)],
        out_specs=tile(GROUP_WIDTH),
        scratch_shapes=[pltpu.VMEM((2, n_pages, 2 * HEAD_DIM, LANES), F32), pltpu.SemaphoreType.DMA((2,)),
                        pltpu.SMEM((2, n_pages), jnp.int32), pltpu.SMEM((2,), jnp.int32)])
    return pl.pallas_call(
        functools.partial(_nsa_sel_sample_body, layer=layer),
        grid_spec=grid_spec,
        out_shape=jax.ShapeDtypeStruct((db * t_new, GROUP_WIDTH), F32),
        compiler_params=_cparams(("arbitrary",)),
        name="nsa_sel_sample",
    )(page_table, flags, qc, gc, part, new_bf, cache_nsa)


_SPLITS = np.cumsum([0, 256, 256, 256, 4, 512, 256, 384, 12, 256, 256])


def _prep_layer(l, P):
    w_in = P['w_in'][l]
    sec = [w_in[:, _SPLITS[i]:_SPLITS[i + 1]] for i in range(10)]
    qa, ka, va, fa, glu, qc, kvc, gc, ud, vd = sec
    w_row = jnp.concatenate([qa, glu, qc, ud, vd, jnp.pad(gc, ((0, 0), (0, LANES - 12)))], axis=1).astype(BF16)
    w_col = jnp.concatenate([ka, va, kvc, jnp.pad(fa, ((0, 0), (0, 12)))], axis=1).T.astype(BF16)
    row = lambda v: v.reshape(1, -1)
    phi_k = P['nsa_phi_k'][l].reshape(CMP_BLOCK, 1, HEAD_DIM, HEAD_DIM)
    phi_v = P['nsa_phi_v'][l].reshape(CMP_BLOCK, 1, HEAD_DIM, HEAD_DIM)
    zero = jnp.zeros_like(phi_k)
    phi = jnp.concatenate([jnp.concatenate([phi_k, zero], axis=-1), jnp.concatenate([zero, phi_v], axis=-1)], axis=1)
    return dict(
        pe_flat=jnp.transpose(P['nsa_pe'][l], (1, 0, 2)).reshape(1, CMP_BLOCK * LANES),
        pe_col=jnp.tile(jnp.transpose(P['nsa_pe'][l], (0, 2, 1)).reshape(LANES, CMP_BLOCK), (1, _CMP_PER_PAGE)),
        phi=phi.reshape(CMP_BLOCK * LANES, LANES).astype(BF16),
        norm1_g=row(P['norm1_g'][l]), w_row=w_row, w_col=w_col,
        gqa=row(jnp.tile(P['fox_qn_g'][l], N_HEADS)), gka=P['fox_kn_g'][l].reshape(HEAD_DIM, 1),
        bf=jnp.pad(P['fox_bf'][l], (0, 4)).reshape(8, 1),
        gqc=row(jnp.tile(P['nsa_qn_g'][l], N_HEADS)), gkc=P['nsa_kn_g'][l].T,
        gmlp_ln_g=row(P['gmlp_ln_g'][l]), gmlp_ln_b=row(P['gmlp_ln_b'][l]),
        gmlp_ws=P['gmlp_ws'][l], gmlp_bs=P['gmlp_bs'][l],
        gnorm_g=P['gnorm_g'][l], w_out=P['w_out'][l].astype(BF16), norm2_g=row(P['norm2_g'][l]),
        w_ff1=P['w_ff1'][l].astype(BF16), w_ff2=P['w_ff2'][l].astype(BF16),
        conv_w=P['conv_w'][l], conv_b=row(P['conv_b'][l]),
        conv_ln_g=row(P['conv_ln_g'][l]), conv_ln_b=row(P['conv_ln_b'][l]),
    )


def _rope_tables(pos):
    inv = ROPE_THETA ** (-jnp.arange(HALF, dtype=F32) / HALF)
    ang = pos.astype(F32)[:, None] * inv
    cos, sin = jnp.cos(ang), jnp.sin(ang)
    return dict(cos_r=jnp.tile(cos, (1, 4)), sin_r=jnp.tile(jnp.concatenate([-sin, sin], axis=1), (1, 2)),
                cos_t=cos.T, sin_t=sin.T)


def _const_tables():
    g = np.arange(GROUP_WIDTH) // HEAD_DIM
    return dict(gsum=jnp.asarray(g[:, None] == g[None, :], BF16))


def _gmlp_tables(w, seq_len, n_seq):
    t = min(seq_len, CHUNK)
    wm = (w['gmlp_ws'] * jnp.tril(jnp.ones((CHUNK, CHUNK), F32)))[:, :t, :t]
    bs = w['gmlp_bs'][:, :t]
    if seq_len < CHUNK:
        eye = jnp.eye(n_seq, dtype=F32)
        wm = jnp.einsum('ab,gts->gatbs', eye, wm).reshape(N_HEADS, n_seq * t, n_seq * t)
        bs = jnp.tile(bs, (1, n_seq))
    c = wm.shape[1]
    return dict(wm=wm.reshape(N_HEADS * c, c).astype(BF16), bs_tab=jnp.repeat(bs.T, HEAD_DIM, axis=1))


def _layer_prompt(x, w, consts, *, tm, ta, tk, tf):
    B, S, D = x.shape
    tabs = dict(consts, **_gmlp_tables(w, S, B))
    o = _proj(x, w, tabs, tm=tm, chunk=CHUNK)
    o_a = _fox_attn(o['qat'], o['karow'], o['vbf'], t=ta)
    o_b = _conv_prompt(o['glu'], w, tm=tm)
    o_c = _nsa_prompt(o, w, t=tk, tk=tk, tl=min(4 * LANES, S))
    flat = lambda a: a.reshape(B * S, a.shape[-1])
    y = _merge_ffn(flat(x), [flat(o_a), flat(o_b), flat(o_c), flat(o['od'])], w, tm=tm, tf=tf)
    wp = min(WINDOW, S)
    states = dict(fox_kv=o['foxkv'], fox_logf=o['logf'], nsa_kv=o['nsakv'],
                  nsa_win=o['nsawin'][:, :, S - wp:], conv=o['glu'][:, S - (CONV_WIDTH - 1):])
    return y.reshape(B, S, D), states


def _even_odd(cmp):
    n = cmp.shape[1]
    order = np.concatenate([np.arange(0, n, 2), np.arange(1, n, 2)])
    cmp = cmp[:, order].astype(BF16)
    return jnp.swapaxes(cmp[:, :, :HEAD_DIM], 1, 2), cmp[:, :, HEAD_DIM:]


def _prep_caches(cache_fox_kv, cache_fox_logf, cache_nsa_kv, state_nsa_win, state_conv):
    L, pool, page = cache_fox_kv.shape[:3]
    db = state_nsa_win.shape[1]
    return dict(
        fox_kv=jnp.transpose(cache_fox_kv, (0, 1, 3, 4, 5, 2)).reshape(L, pool, 2 * GROUP_WIDTH, page),
        fox_lf=jnp.transpose(cache_fox_logf, (0, 1, 3, 2)),
        nsa_kv=jnp.transpose(cache_nsa_kv, (0, 1, 3, 4, 2)).reshape(L, pool, 4 * HEAD_DIM, page),
        win=jnp.transpose(state_nsa_win, (0, 1, 3, 4, 2)).reshape(L, db, 2 * HEAD_DIM, -1),
        conv=jnp.transpose(state_conv, (0, 2, 1, 3)),
    )


def _layer_sample(l, xs, w, caches, consts, page_table, *, past_len, pg, tf):
    db, t_new, D = xs.shape
    R = db * t_new
    tabs = dict(consts, **_gmlp_tables(w, t_new, db))
    o = {k: v[0] for k, v in _proj(xs.reshape(1, R, D), w, tabs, tm=R, chunk=R).items()}
    o_a = _fox_sample(l, page_table, o['qa'].astype(F32), o['kaug'], o['vbf'], caches['fox_kv'], caches['fox_lf'],
                      t_new=t_new, pg=pg)
    glu_t = jnp.swapaxes(o['glu'].reshape(db, t_new, GROUP_WIDTH), 0, 1)
    o_b_t, conv_new = _conv_sample(l, caches['conv'], glu_t, w)
    o_b = jnp.swapaxes(o_b_t, 0, 1).reshape(R, GROUP_WIDTH)
    cmp = _compress_sample(l, page_table, caches['nsa_kv'], w['pe_col'], w['phi'], n_group=pg)
    kc_t, vc = _even_odd(cmp)
    qc = o['qc'].astype(F32)
    part, flags = _nsa_local_sample(l, qc, o['gc'], kc_t, vc, caches['win'], o['nsabf'],
                                    t_new=t_new, past_len=past_len)
    flags = flags[:, 0, ::2].astype(jnp.int32)
    o_c = _nsa_sel_sample(l, page_table, flags, qc, o['gc'], part, o['nsabf'], caches['nsa_kv'], t_new=t_new)
    y = _merge_ffn(xs.reshape(R, D), [o_a, o_b, o_c, o['od']], w, tm=R, tf=tf)
    rows = lambda a, *shape: a.T.reshape(db, t_new, *shape)
    win_new = jnp.swapaxes(o['nsawin'].reshape(2 * HEAD_DIM, db, t_new), 0, 1)
    win = jnp.concatenate([caches['win'][l][:, :, t_new:], win_new], axis=-1)
    states = dict(fox_kv=rows(o['foxkv'], 2, N_HEADS, HEAD_DIM), fox_logf=rows(o['logf'], N_HEADS),
                  nsa_kv=rows(o['nsakv'], 4, HEAD_DIM),
                  nsa_win=jnp.transpose(win.reshape(db, 2, HEAD_DIM, -1), (0, 3, 1, 2)),
                  conv=jnp.swapaxes(conv_new, 0, 1), gmlp_v=o['vn'].reshape(db, t_new, GROUP_WIDTH))
    return y.reshape(db, t_new, D), states


def _scan_matrix(n, seg):
    i = np.arange(n)
    return jnp.asarray((i[:, None] <= i[None, :]) & (i[:, None] // seg == i[None, :] // seg), BF16)


_PARAM_NAMES = ('norm1_g', 'w_in', 'fox_bf', 'fox_qn_g', 'fox_kn_g', 'conv_w', 'conv_b', 'conv_ln_g', 'conv_ln_b',
                'nsa_qn_g', 'nsa_kn_g', 'nsa_pe', 'nsa_phi_k', 'nsa_phi_v', 'gmlp_ln_g', 'gmlp_ln_b', 'gmlp_ws',
                'gmlp_bs', 'gnorm_g', 'w_out', 'norm2_g', 'w_ff1', 'w_ff2')


def kernel(x_prompt, x_sample, cache_fox_kv, cache_fox_logf, cache_nsa_kv, state_nsa_win, state_conv, page_table,
           *params):
    P = dict(zip(_PARAM_NAMES, params))
    depth = P['w_in'].shape[0]
    B, S, D = x_prompt.shape
    DB, T, _ = x_sample.shape
    n_pages, page = page_table.shape[1], cache_fox_kv.shape[2]
    past_len = n_pages * page
    assert past_len % SEL_BLOCK == 0 and T <= SEL_BLOCK and past_len >= WINDOW and page == LANES
    tm = 512
    consts_p = dict(_rope_tables(jnp.arange(S)), **_const_tables(), utri=_scan_matrix(tm, tm))
    consts_s = dict(_rope_tables(jnp.tile(past_len + jnp.arange(T), DB)), **_const_tables(),
                    utri=_scan_matrix(DB * T, T))
    caches = _prep_caches(cache_fox_kv, cache_fox_logf, cache_nsa_kv, state_nsa_win, state_conv)
    xp, xs = x_prompt, x_sample
    st_p, st_s = [], []
    for l in range(depth):
        w = _prep_layer(l, P)
        xp, sp = _layer_prompt(xp, w, consts_p, tm=tm, ta=512, tk=512, tf=1024)
        xs, ss = _layer_sample(l, xs, w, caches, consts_s, page_table, past_len=past_len, pg=32, tf=1024)
        st_p.append(sp)
        st_s.append(ss)
    stack_p = lambda k: jnp.stack([s[k] for s in st_p])
    stack_s = lambda k: jnp.stack([s[k] for s in st_s])
    fox_kv_p = jnp.transpose(stack_p('fox_kv').reshape(depth, B, 2, N_HEADS, HEAD_DIM, S), (0, 1, 5, 2, 3, 4))
    fox_logf_p = jnp.transpose(stack_p('fox_logf'), (0, 1, 3, 2))
    nsa_kv_p = jnp.transpose(stack_p('nsa_kv').reshape(depth, B, 4, HEAD_DIM, S), (0, 1, 4, 2, 3))
    nsa_win_p = jnp.transpose(stack_p('nsa_win').reshape(depth, B, 2, HEAD_DIM, -1), (0, 1, 4, 2, 3))
    return (xp, xs, fox_kv_p, stack_s('fox_kv'), fox_logf_p, stack_s('fox_logf'),
            nsa_kv_p, stack_s('nsa_kv'), nsa_win_p, stack_s('nsa_win'),
            stack_p('conv'), stack_s('conv'), stack_s('gmlp_v'))
```

```python
import functools

import jax
import jax.numpy as jnp
import numpy as np
from jax import lax
from jax.experimental import pallas as pl
from jax.experimental.pallas import tpu as pltpu

F32 = jnp.float32
BF16 = jnp.bfloat16

HEAD_DIM = 64
HALF = HEAD_DIM // 2
GROUP_WIDTH = 256
N_HEADS = GROUP_WIDTH // HEAD_DIM
CONV_WIDTH = 31
CMP_BLOCK = 32
SEL_BLOCK = 64
N_SELECT = 16
WINDOW = 512
CHUNK = 128
ROPE_THETA = 10000.0
EPS = 1e-6
FORCED_SCORE = 1e4
Q_SCALE = HEAD_DIM ** -0.5
LOG2E = 1.4426950408889634
NEG_BIG = -1e30
SEL_NEG = -32768.0
LANES = 128
VMEM_LIMIT = 56 * 1024 * 1024


def _cparams(sem):
    return pltpu.CompilerParams(dimension_semantics=sem, vmem_limit_bytes=VMEM_LIMIT)


def _nt(a, b):
    return lax.dot_general(a, b, (((1,), (1,)), ((), ())), preferred_element_type=F32)


def _dot(a, b):
    return jnp.dot(a, b, preferred_element_type=F32)


def _split3(x):
    h = x.astype(BF16).astype(F32)
    r = x - h
    m = r.astype(BF16).astype(F32)
    l = (r - m).astype(BF16).astype(F32)
    return h, m, l


def _log_sigmoid(x):
    return jnp.minimum(x, 0.0) - jnp.log1p(jnp.exp(-jnp.abs(x)))


def _group_mean_sq(x, gsum):
    x2 = x * x
    hi = x2.astype(BF16)
    lo = (x2 - hi.astype(F32)).astype(BF16)
    return (_dot(hi, gsum) + _dot(lo, gsum)) * (1.0 / HEAD_DIM)


def _rope_rows(x, cos, sin_signed):
    lane = lax.broadcasted_iota(jnp.int32, x.shape, 1)
    first_half = (lane % HEAD_DIM) < HALF
    swapped = jnp.where(first_half, pltpu.roll(x, LANES - HALF, 1), pltpu.roll(x, HALF, 1))
    return x * cos + swapped * sin_signed


_R_QA, _R_GLU, _R_QC, _R_UD, _R_VD, _R_SMALL, _R_END = 0, 256, 768, 1024, 1280, 1536, 1664
_C_KA, _C_VA, _C_KVC, _C_FA, _C_END = 0, 256, 512, 896, 912


def _proj_body(x_ref, g1_ref, wrow_ref, wcol_ref, cosr_ref, sinr_ref, cost_ref, sint_ref,
               gsum_ref, gqa_ref, gka_ref, bf_ref, gqc_ref, gkc_ref, lng_ref, lnb_ref,
               wm_ref, bstab_ref, utri_ref,
               qa_ref, foxkv_ref, kaug_ref, vbf_ref, logf_ref, glu_ref, qc_ref, nsakv_ref,
               nsawin_ref, nsabf_ref, cmprow_ref, gc_ref, od_ref, vn_ref, qat_ref, karow_ref, qct_ref, ksrow_ref,
               carry_ref, *, chunk):
    tm = x_ref.shape[1]

    @pl.when(pl.program_id(1) == 0)
    def _():
        carry_ref[...] = jnp.zeros_like(carry_ref)

    x = x_ref[0]
    ms = jnp.mean(x * x, axis=-1, keepdims=True)
    xn = ((x * lax.rsqrt(ms + EPS)) * g1_ref[...]).astype(BF16)
    zr = _dot(xn, wrow_ref[...])
    zc = _nt(wcol_ref[...], xn)
    gsum = gsum_ref[...]

    qa = zr[:, _R_QA:_R_QA + 256]
    qa = qa * lax.rsqrt(_group_mean_sq(qa, gsum) + EPS) * gqa_ref[...] * Q_SCALE
    lane = lax.broadcasted_iota(jnp.int32, (tm, LANES), 1)
    for h in range(N_HEADS):
        src = qa[:, (h // 2) * LANES:(h // 2 + 1) * LANES]
        if h % 2 == 0:
            aug = jnp.where(lane < HEAD_DIM, src, jnp.where(lane < HEAD_DIM + 3, 1.0, 0.0))
        else:
            aug = jnp.where(lane >= HEAD_DIM, src, jnp.where(lane < 3, 1.0, 0.0))
        qa_ref[0, :, h * LANES:(h + 1) * LANES] = aug.astype(BF16)
        is_q = (lane < HEAD_DIM) if h % 2 == 0 else (lane >= HEAD_DIM)
        qat_ref[0, h] = jnp.transpose(aug * jnp.where(is_q, LOG2E, 1.0)).astype(BF16)

    logf = _log_sigmoid(zc[_C_FA:_C_FA + 8] + bf_ref[...])
    logf_ref[0] = logf[0:N_HEADS]
    parts = _split3(logf)
    l3 = jnp.concatenate(parts, axis=0).astype(BF16)
    cs = _dot(l3, utri_ref[...])
    fcum = cs[0:8] + cs[8:16] + cs[16:24] + carry_ref[:, 0:1]
    carry_ref[...] = jnp.broadcast_to(fcum[:, tm - 1:tm], carry_ref.shape)
    nfh, nfm, nfl = _split3(-fcum)
    nf2 = _split3(-fcum * LOG2E)
    row8 = lax.broadcasted_iota(jnp.int32, (8, tm), 0)
    zeros56 = jnp.zeros((HEAD_DIM - 8, tm), F32)
    gka = gka_ref[:, 0:1]
    for h in range(N_HEADS):
        k = zc[_C_KA + h * HEAD_DIM:_C_KA + (h + 1) * HEAD_DIM]
        k = k * lax.rsqrt(jnp.mean(k * k, axis=0, keepdims=True) + EPS) * gka
        v = zc[_C_VA + h * HEAD_DIM:_C_VA + (h + 1) * HEAD_DIM]
        foxkv_ref[0, h * HEAD_DIM:(h + 1) * HEAD_DIM, :] = k
        foxkv_ref[0, GROUP_WIDTH + h * HEAD_DIM:GROUP_WIDTH + (h + 1) * HEAD_DIM, :] = v
        vbf_ref[0, h * HEAD_DIM:(h + 1) * HEAD_DIM, :] = v.astype(BF16)
        extra8 = jnp.where(row8 == 0, nfh[h:h + 1],
                           jnp.where(row8 == 1, nfm[h:h + 1],
                                     jnp.where(row8 == 2, nfl[h:h + 1], 0.0)))
        extra = jnp.concatenate([extra8, zeros56], axis=0)
        pieces = [k, extra] if h % 2 == 0 else [extra, k]
        kaug_ref[0, h] = jnp.concatenate(pieces, axis=0).astype(BF16)
        extra2 = jnp.where(row8 == 0, nf2[0][h:h + 1],
                           jnp.where(row8 == 1, nf2[1][h:h + 1],
                                     jnp.where(row8 == 2, nf2[2][h:h + 1], 0.0)))
        extra2 = jnp.concatenate([extra2, zeros56], axis=0)
        kaug2 = jnp.concatenate([k, extra2] if h % 2 == 0 else [extra2, k], axis=0)
        karow_ref[0, :, h * LANES:(h + 1) * LANES] = jnp.transpose(kaug2).astype(BF16)

    glu_in = zr[:, _R_GLU:_R_GLU + 512]
    glu_ref[0] = glu_in[:, :256] * jax.nn.sigmoid(glu_in[:, 256:])

    qc = zr[:, _R_QC:_R_QC + 256]
    qc = qc * lax.rsqrt(_group_mean_sq(qc, gsum) + EPS) * gqc_ref[...]
    cosr, sinr = cosr_ref[...], sinr_ref[...]
    for p in range(2):
        qh = _rope_rows(qc[:, p * LANES:(p + 1) * LANES], cosr, sinr) * Q_SCALE
        qc_ref[0, :, p * LANES:(p + 1) * LANES] = qh.astype(BF16)
        qct_ref[0, p * LANES:(p + 1) * LANES, :] = jnp.transpose(qh * LOG2E).astype(BF16)
    gc_ref[0] = jax.nn.sigmoid(zr[:, _R_SMALL:_R_SMALL + LANES])
    cost, sint = cost_ref[...], sint_ref[...]
    keys = []
    for b in range(3):
        kb = zc[_C_KVC + 2 * b * HEAD_DIM:_C_KVC + (2 * b + 1) * HEAD_DIM]
        kb = kb * lax.rsqrt(jnp.mean(kb * kb, axis=0, keepdims=True) + EPS) * gkc_ref[:, b:b + 1]
        x1, x2 = kb[:HALF], kb[HALF:]
        keys.append(jnp.concatenate([x1 * cost - x2 * sint, x2 * cost + x1 * sint], axis=0))
    vals = [zc[_C_KVC + (2 * b + 1) * HEAD_DIM:_C_KVC + (2 * b + 2) * HEAD_DIM] for b in range(3)]
    nsakv_ref[0, 0:64, :] = keys[0]
    nsakv_ref[0, 64:128, :] = vals[0]
    nsakv_ref[0, 128:192, :] = keys[1]
    nsakv_ref[0, 192:256, :] = vals[1]
    nsawin_ref[0, 0:64, :] = keys[2]
    nsawin_ref[0, 64:128, :] = vals[2]
    nsabf_ref[0, 0:64, :] = keys[1].astype(BF16)
    nsabf_ref[0, 64:128, :] = vals[1].astype(BF16)
    nsabf_ref[0, 128:192, :] = keys[2].astype(BF16)
    nsabf_ref[0, 192:256, :] = vals[2].astype(BF16)
    cmprow_ref[0] = jnp.transpose(jnp.concatenate([keys[0], vals[0]], axis=0))
    ksrow_ref[0, :, 0:LANES] = jnp.transpose(jnp.concatenate([keys[1], jnp.zeros_like(keys[1])], axis=0)).astype(BF16)
    blk = (pl.program_id(1) * tm + lax.broadcasted_iota(jnp.int32, (tm, LANES), 0)) // SEL_BLOCK
    ksrow_ref[0, :, LANES:] = jnp.where(blk == lane, 1.0, 0.0).astype(BF16)

    ud = zr[:, _R_UD:_R_UD + 256]
    vd = zr[:, _R_VD:_R_VD + 256]
    mu = jnp.mean(vd, axis=-1, keepdims=True)
    var = jnp.mean(jnp.square(vd - mu), axis=-1, keepdims=True)
    vn = (vd - mu) * lax.rsqrt(var + EPS) * lng_ref[...] + lnb_ref[...]
    vn_ref[0] = vn
    grp = lax.broadcasted_iota(jnp.int32, (chunk, GROUP_WIDTH), 1) // HEAD_DIM
    wm = wm_ref[...]
    for c in range(tm // chunk):
        r = _dot(wm, vn[c * chunk:(c + 1) * chunk].astype(BF16))
        mixed = bstab_ref[...]
        for g in range(N_HEADS):
            mixed = mixed + jnp.where(grp == g, r[g * chunk:(g + 1) * chunk], 0.0)
        od_ref[0, c * chunk:(c + 1) * chunk, :] = ud[c * chunk:(c + 1) * chunk] * mixed


def _proj(x, w, tabs, *, tm, chunk):
    B, S, D = x.shape
    ns = S // tm
    row = lambda width: pl.BlockSpec((1, tm, width), lambda b, i: (b, i, 0))
    col = lambda height: pl.BlockSpec((1, height, tm), lambda b, i: (b, 0, i))
    full = lambda a: pl.BlockSpec(a.shape, lambda b, i: (0,) * a.ndim)
    ins = [x, w['norm1_g'], w['w_row'], w['w_col'], tabs['cos_r'], tabs['sin_r'], tabs['cos_t'], tabs['sin_t'],
           tabs['gsum'], w['gqa'], w['gka'], w['bf'], w['gqc'], w['gkc'], w['gmlp_ln_g'], w['gmlp_ln_b'],
           tabs['wm'], tabs['bs_tab'], tabs['utri']]
    in_specs = [row(D), full(ins[1]), full(ins[2]), full(ins[3]),
                pl.BlockSpec((tm, LANES), lambda b, i: (i, 0)), pl.BlockSpec((tm, LANES), lambda b, i: (i, 0)),
                pl.BlockSpec((HALF, tm), lambda b, i: (0, i)), pl.BlockSpec((HALF, tm), lambda b, i: (0, i))]
    in_specs += [full(a) for a in ins[8:]]
    outs = dict(
        qa=(jax.ShapeDtypeStruct((B, S, 4 * LANES), BF16), row(4 * LANES)),
        foxkv=(jax.ShapeDtypeStruct((B, 2 * GROUP_WIDTH, S), F32), col(2 * GROUP_WIDTH)),
        kaug=(jax.ShapeDtypeStruct((B, N_HEADS, LANES, S), BF16),
              pl.BlockSpec((1, N_HEADS, LANES, tm), lambda b, i: (b, 0, 0, i))),
        vbf=(jax.ShapeDtypeStruct((B, GROUP_WIDTH, S), BF16), col(GROUP_WIDTH)),
        logf=(jax.ShapeDtypeStruct((B, N_HEADS, S), F32), col(N_HEADS)),
        glu=(jax.ShapeDtypeStruct((B, S, GROUP_WIDTH), F32), row(GROUP_WIDTH)),
        qc=(jax.ShapeDtypeStruct((B, S, GROUP_WIDTH), BF16), row(GROUP_WIDTH)),
        nsakv=(jax.ShapeDtypeStruct((B, 256, S), F32), col(256)),
        nsawin=(jax.ShapeDtypeStruct((B, 128, S), F32), col(128)),
        nsabf=(jax.ShapeDtypeStruct((B, 256, S), BF16), col(256)),
        cmprow=(jax.ShapeDtypeStruct((B, S, LANES), F32), row(LANES)),
        gc=(jax.ShapeDtypeStruct((B, S, LANES), F32), row(LANES)),
        od=(jax.ShapeDtypeStruct((B, S, GROUP_WIDTH), F32), row(GROUP_WIDTH)),
        vn=(jax.ShapeDtypeStruct((B, S, GROUP_WIDTH), F32), row(GROUP_WIDTH)),
        qat=(jax.ShapeDtypeStruct((B, N_HEADS, LANES, S), BF16),
             pl.BlockSpec((1, N_HEADS, LANES, tm), lambda b, i: (b, 0, 0, i))),
        karow=(jax.ShapeDtypeStruct((B, S, 4 * LANES), BF16), row(4 * LANES)),
        qct=(jax.ShapeDtypeStruct((B, GROUP_WIDTH, S), BF16), col(GROUP_WIDTH)),
        ksrow=(jax.ShapeDtypeStruct((B, S, 2 * LANES), BF16), row(2 * LANES)),
    )
    names = list(outs)
    res = pl.pallas_call(
        functools.partial(_proj_body, chunk=chunk),
        grid=(B, ns),
        in_specs=in_specs,
        out_specs=[outs[n][1] for n in names],
        out_shape=[outs[n][0] for n in names],
        scratch_shapes=[pltpu.VMEM((8, LANES), F32)],
        compiler_params=_cparams(("arbitrary", "arbitrary")),
        name="proj",
    )(*ins)
    return dict(zip(names, res))


def _online_softmax_step(s, v, carry):
    m, l, acc = carry
    m_new = jnp.maximum(m, jnp.max(s, axis=-1, keepdims=True))
    alpha = jnp.exp(m - m_new)
    p = jnp.exp(s - m_new)
    l = alpha * l + jnp.sum(p, axis=-1, keepdims=True)
    acc = alpha * acc + _nt(p.astype(BF16), v)
    return m_new, l, acc


def _softmax_init(rows, dv):
    return (jnp.full((rows, 1), NEG_BIG, F32), jnp.zeros((rows, 1), F32), jnp.zeros((rows, dv), F32))


_SUM_ROWS = 16


def _col_softmax_init(cols):
    return jnp.full((1, cols), NEG_BIG, F32), jnp.zeros((HEAD_DIM + _SUM_ROWS, cols), F32)


def _col_softmax_step(s, v, state):
    m, acc = state
    m_new = jnp.maximum(m, jnp.max(s, axis=0, keepdims=True))
    p = jnp.exp2(s - m_new).astype(BF16)
    v1 = jnp.concatenate([v, jnp.ones((_SUM_ROWS, v.shape[1]), BF16)], axis=0)
    return m_new, jnp.exp2(m - m_new) * acc + _dot(v1, p)


def _col_softmax_result(state):
    acc = state[1]
    return acc[:HEAD_DIM] / acc[HEAD_DIM:HEAD_DIM + 1]


def _fox_attn_body(q_ref, k_ref, v_ref, o_ref):
    t = q_ref.shape[3]
    i = pl.program_id(1)
    key = lax.broadcasted_iota(jnp.int32, (t, t), 0)
    qry = lax.broadcasted_iota(jnp.int32, (t, t), 1)

    def tile(kt, states, masked):
        start = pl.multiple_of(kt * t, t)
        scores = [_dot(k_ref[0, pl.ds(start, t), h * LANES:(h + 1) * LANES], q_ref[0, h])
                  for h in range(N_HEADS)]
        out = []
        for h in range(N_HEADS):
            s = jnp.where(key <= qry, scores[h], NEG_BIG) if masked else scores[h]
            out.append(_col_softmax_step(s, v_ref[0, h * HEAD_DIM:(h + 1) * HEAD_DIM, pl.ds(start, t)], states[h]))
        return tuple(out)

    init = tuple(_col_softmax_init(t) for _ in range(N_HEADS))
    states = tile(i, lax.fori_loop(0, i, functools.partial(tile, masked=False), init), True)
    o_ref[0] = jnp.concatenate([jnp.transpose(_col_softmax_result(st)) for st in states], axis=-1)


def _fox_attn(qat, karow, vbf, *, t):
    B, S, _ = karow.shape
    return pl.pallas_call(
        _fox_attn_body,
        grid=(B, S // t),
        in_specs=[pl.BlockSpec((1, N_HEADS, LANES, t), lambda b, i: (b, 0, 0, i)),
                  pl.BlockSpec((1, S, 4 * LANES), lambda b, i: (b, 0, 0)),
                  pl.BlockSpec((1, GROUP_WIDTH, S), lambda b, i: (b, 0, 0))],
        out_specs=pl.BlockSpec((1, t, GROUP_WIDTH), lambda b, i: (b, i, 0)),
        out_shape=jax.ShapeDtypeStruct((B, S, GROUP_WIDTH), F32),
        compiler_params=_cparams(("arbitrary", "arbitrary")),
        name="fox_attn",
    )(qat, karow, vbf)


_HALO = 32


def _ln_silu(y, g, b):
    mu = jnp.mean(y, axis=-1, keepdims=True)
    var = jnp.mean(jnp.square(y - mu), axis=-1, keepdims=True)
    y = (y - mu) * lax.rsqrt(var + EPS) * g + b
    return y * jax.nn.sigmoid(y)


def _conv_prompt_body(cur_ref, halo_ref, w_ref, cb_ref, lng_ref, lnb_ref, o_ref, xin_ref):
    tm = cur_ref.shape[1]
    first = pl.program_id(1) == 0
    xin_ref[0:_HALO, :] = jnp.where(first, 0.0, halo_ref[0])
    xin_ref[_HALO:, :] = cur_ref[0]
    off = _HALO - (CONV_WIDTH - 1)
    acc = jnp.zeros((tm, GROUP_WIDTH), F32)
    for k in range(CONV_WIDTH):
        acc = acc + xin_ref[pl.ds(off + k, tm), :] * w_ref[k:k + 1, :]
    o_ref[0] = _ln_silu(acc + cb_ref[...], lng_ref[...], lnb_ref[...])


def _conv_prompt(glu, w, *, tm):
    B, S, C = glu.shape
    r = tm // _HALO
    full = lambda a: pl.BlockSpec(a.shape, lambda b, i: (0,) * a.ndim)
    ins = [glu, glu, w['conv_w'], w['conv_b'], w['conv_ln_g'], w['conv_ln_b']]
    return pl.pallas_call(
        _conv_prompt_body,
        grid=(B, S // tm),
        in_specs=[pl.BlockSpec((1, tm, C), lambda b, i: (b, i, 0)),
                  pl.BlockSpec((1, _HALO, C), lambda b, i: (b, jnp.maximum(i * r - 1, 0), 0))]
                 + [full(a) for a in ins[2:]],
        out_specs=pl.BlockSpec((1, tm, C), lambda b, i: (b, i, 0)),
        out_shape=jax.ShapeDtypeStruct((B, S, C), F32),
        scratch_shapes=[pltpu.VMEM((tm + _HALO, C), F32)],
        compiler_params=_cparams(("arbitrary", "arbitrary")),
        name="conv_prompt",
    )(*ins)


def _compress_body(x_ref, pe_ref, phi_ref, o_ref):
    o_ref[0] = _dot((x_ref[0] + pe_ref[...]).astype(BF16), phi_ref[...])


def _compress(blocks, pe_flat, phi):
    B, n, width = blocks.shape
    return pl.pallas_call(
        _compress_body,
        grid=(B,),
        in_specs=[pl.BlockSpec((1, n, width), lambda b: (b, 0, 0)),
                  pl.BlockSpec(pe_flat.shape, lambda b: (0, 0)),
                  pl.BlockSpec(phi.shape, lambda b: (0, 0))],
        out_specs=pl.BlockSpec((1, n, LANES), lambda b: (b, 0, 0)),
        out_shape=jax.ShapeDtypeStruct((B, n, LANES), F32),
        compiler_params=_cparams(("arbitrary",)),
        name="nsa_compress",
    )(blocks, pe_flat, phi)


def _masked_softmax(s, mask):
    s = jnp.where(mask, s, NEG_BIG)
    m = jnp.max(s, axis=-1, keepdims=True)
    e = jnp.where(mask, jnp.exp(s - m), 0.0)
    return e / jnp.maximum(jnp.sum(e, axis=-1, keepdims=True), 1e-30)


def _stack_heads(q):
    return jnp.concatenate([q[:, h * HEAD_DIM:(h + 1) * HEAD_DIM] for h in range(N_HEADS)], axis=0)


def _select_blocks(imp, cur, n_select):
    j = lax.broadcasted_iota(jnp.int32, imp.shape, 1)
    forced = (j == 0) | (j == cur) | (j == cur - 1)
    v = jnp.where(forced, FORCED_SCORE, imp)
    v = jnp.where(j <= cur, v, -1.0)
    jf = j.astype(F32)
    sel = jnp.zeros(imp.shape, jnp.bool_)
    for _ in range(n_select):
        m = jnp.max(v, axis=-1, keepdims=True)
        idx = jnp.min(jnp.where(v == m, jf, float(imp.shape[1])), axis=-1, keepdims=True)
        pick = jf == idx
        sel = sel | (pick & (m >= 0.0))
        v = jnp.where(pick, -2.0, v)
    return sel


def _nsa_local_body(q_ref, g_ref, kc_ref, vc_ref, kv_ref, o_ref, sbt_ref):
    t = q_ref.shape[1]
    nc = kc_ref.shape[2]
    half = nc // 2
    st = pl.program_id(1) * t
    qs = _stack_heads(q_ref[0])
    qpos = st + lax.broadcasted_iota(jnp.int32, (N_HEADS * t, 1), 0) % t

    c = lax.broadcasted_iota(jnp.int32, (1, nc), 1)
    blk = jnp.where(c < half, 2 * c, 2 * (c - half) + 1)
    p_cmp = _masked_softmax(_dot(qs, kc_ref[0]), (blk + 1) * CMP_BLOCK - 1 <= qpos)
    o_cmp = _dot(p_cmp.astype(BF16), vc_ref[0])
    imp = p_cmp[0:t] + p_cmp[t:2 * t] + p_cmp[2 * t:3 * t] + p_cmp[3 * t:4 * t]
    imp = imp[:, :half] + imp[:, half:]
    sel = _select_blocks(imp, qpos[0:t] // SEL_BLOCK, min(N_SELECT, half))
    sb = jnp.where(sel, 0.0, SEL_NEG)
    if half < LANES:
        sb = jnp.concatenate([sb, jnp.zeros((t, LANES - half), F32)], axis=1)
    sbt_ref[0] = jnp.transpose(sb).astype(BF16)

    span = WINDOW + t
    start = pl.multiple_of(jnp.maximum(st - WINDOW, 0), LANES)
    kwpos = start + lax.broadcasted_iota(jnp.int32, (1, span), 1)
    wmask = (kwpos <= qpos) & (qpos - kwpos < WINDOW)
    p_win = _masked_softmax(_dot(qs, kv_ref[0, 128:192, pl.ds(start, span)]), wmask)
    o_win = _nt(p_win.astype(BF16), kv_ref[0, 192:256, pl.ds(start, span)])

    g = g_ref[0]
    outs = [g[:, 3 * h:3 * h + 1] * o_cmp[h * t:(h + 1) * t] + g[:, 3 * h + 2:3 * h + 3] * o_win[h * t:(h + 1) * t]
            for h in range(N_HEADS)]
    o_ref[0] = jnp.concatenate(outs, axis=-1)


def _nsa_local(qc, gc, kc_t, vc, nsabf, *, t):
    B, S, _ = qc.shape
    nc = kc_t.shape[2]
    tile = lambda width: pl.BlockSpec((1, t, width), lambda b, i: (b, i, 0))
    return pl.pallas_call(
        _nsa_local_body,
        grid=(B, S // t),
        in_specs=[tile(GROUP_WIDTH), tile(LANES),
                  pl.BlockSpec((1, HEAD_DIM, nc), lambda b, i: (b, 0, 0)),
                  pl.BlockSpec((1, nc, HEAD_DIM), lambda b, i: (b, 0, 0)),
                  pl.BlockSpec((1, 256, S), lambda b, i: (b, 0, 0))],
        out_specs=[tile(GROUP_WIDTH), pl.BlockSpec((1, max(nc // 2, LANES), t), lambda b, i: (b, 0, i))],
        out_shape=[jax.ShapeDtypeStruct((B, S, GROUP_WIDTH), F32),
                   jax.ShapeDtypeStruct((B, max(nc // 2, LANES), S), BF16)],
        compiler_params=_cparams(("arbitrary", "arbitrary")),
        name="nsa_local",
    )(qc, gc, kc_t, vc, nsabf)


def _nsa_sel_body(qt_ref, sbt_ref, g_ref, part_ref, ks_ref, kv_ref, o_ref, *, tk):
    t = qt_ref.shape[2]
    st = pl.program_id(1) * t
    n_grp, per = N_HEADS, 1
    pad = jnp.zeros((LANES - HEAD_DIM, per * t), BF16)
    bias = jnp.concatenate([sbt_ref[0]] * per, axis=1)
    q2 = [jnp.concatenate([jnp.concatenate([qt_ref[0, h * HEAD_DIM:(h + 1) * HEAD_DIM, :]
                                            for h in range(c * per, (c + 1) * per)], axis=1), pad, bias], axis=0)
          for c in range(n_grp)]
    qpos = st + lax.broadcasted_iota(jnp.int32, (1, per * t), 1) % t

    def tile(kt, states, masked):
        start = pl.multiple_of(kt * tk, tk)
        keys = ks_ref[0, pl.ds(start, tk), :]
        scores = [_dot(keys, q2[c]) for c in range(n_grp)]
        v = kv_ref[0, HEAD_DIM:2 * HEAD_DIM, pl.ds(start, tk)]
        out = []
        for c in range(n_grp):
            s = scores[c]
            if masked:
                kpos = start + lax.broadcasted_iota(jnp.int32, (tk, 1), 0)
                s = jnp.where(kpos <= qpos, s, NEG_BIG)
            out.append(_col_softmax_step(s, v, states[c]))
        return tuple(out)

    last = (st + t - 1) // tk
    init = tuple(_col_softmax_init(per * t) for _ in range(n_grp))
    states = tile(last, lax.fori_loop(0, last, functools.partial(tile, masked=False), init), True)
    g = g_ref[0]
    outs = []
    for h in range(N_HEADS):
        o_sel = jnp.transpose(_col_softmax_result(states[h // per])[:, (h % per) * t:(h % per + 1) * t])
        outs.append(g[:, 3 * h + 1:3 * h + 2] * o_sel)
    o_ref[0] = part_ref[0] + jnp.concatenate(outs, axis=-1)


def _nsa_sel(qct, selbt, gc, part, ksrow, nsabf, *, t, tk):
    B, _, S = qct.shape
    ns = selbt.shape[1]
    tile = lambda width: pl.BlockSpec((1, t, width), lambda b, i: (b, i, 0))
    return pl.pallas_call(
        functools.partial(_nsa_sel_body, tk=tk),
        grid=(B, S // t),
        in_specs=[pl.BlockSpec((1, GROUP_WIDTH, t), lambda b, i: (b, 0, i)),
                  pl.BlockSpec((1, ns, t), lambda b, i: (b, 0, i)),
                  tile(LANES), tile(GROUP_WIDTH),
                  pl.BlockSpec((1, S, 2 * LANES), lambda b, i: (b, 0, 0)),
                  pl.BlockSpec((1, 256, S), lambda b, i: (b, 0, 0))],
        out_specs=tile(GROUP_WIDTH),
        out_shape=jax.ShapeDtypeStruct((B, S, GROUP_WIDTH), F32),
        compiler_params=_cparams(("arbitrary", "arbitrary")),
        name="nsa_sel",
    )(qct, selbt, gc, part, ksrow, nsabf)


def _nsa_prompt(o, w, *, t, tk, tl):
    B, S, _ = o['qc'].shape
    n = S // CMP_BLOCK
    cmp = _compress(o['cmprow'].reshape(B, n, CMP_BLOCK * LANES), w['pe_flat'], w['phi'])
    kc_t, vc = _even_odd(cmp)
    part, selbias = _nsa_local(o['qc'], o['gc'], kc_t, vc, o['nsabf'], t=tl)
    assert S // SEL_BLOCK <= LANES
    return _nsa_sel(o['qct'], selbias, o['gc'], part, o['ksrow'], o['nsabf'], t=t, tk=tk)


def _merge_ffn_body(x_ref, oa_ref, ob_ref, oc_ref, od_ref, gn_ref, wout_ref, g2_ref, w1_ref, w2_ref,
                    y_ref, hn_ref, acc_ref):
    j = pl.program_id(1)

    @pl.when(j == 0)
    def _():
        h = x_ref[...]
        for i, o_ref in enumerate((oa_ref, ob_ref, oc_ref, od_ref)):
            o = o_ref[...]
            o = o * lax.rsqrt(jnp.mean(o * o, axis=-1, keepdims=True) + EPS) * gn_ref[i:i + 1, :]
            h = h + _dot(o.astype(BF16), wout_ref[i * GROUP_WIDTH:(i + 1) * GROUP_WIDTH, :])
        acc_ref[...] = h
        hn = h * lax.rsqrt(jnp.mean(h * h, axis=-1, keepdims=True) + EPS) * g2_ref[...]
        hn_ref[...] = hn.astype(BF16)

    u = jnp.maximum(_dot(hn_ref[...], w1_ref[...]), 0.0)
    acc_ref[...] += _dot((u * u).astype(BF16), w2_ref[...])

    @pl.when(j == pl.num_programs(1) - 1)
    def _():
        y_ref[...] = acc_ref[...]


def _merge_ffn(x, outs, w, *, tm, tf):
    R, D = x.shape
    F = w['w_ff1'].shape[1]
    rows = lambda width: pl.BlockSpec((tm, width), lambda i, j: (i, 0))
    full = lambda a: pl.BlockSpec(a.shape, lambda i, j: (0,) * a.ndim)
    return pl.pallas_call(
        _merge_ffn_body,
        grid=(R // tm, F // tf),
        in_specs=[rows(D)] + [rows(GROUP_WIDTH)] * 4 + [full(w['gnorm_g']), full(w['w_out']), full(w['norm2_g']),
                  pl.BlockSpec((D, tf), lambda i, j: (0, j)), pl.BlockSpec((tf, D), lambda i, j: (j, 0))],
        out_specs=rows(D),
        out_shape=jax.ShapeDtypeStruct((R, D), F32),
        scratch_shapes=[pltpu.VMEM((tm, D), BF16), pltpu.VMEM((tm, D), F32)],
        compiler_params=_cparams(("arbitrary", "arbitrary")),
        name="merge_ffn",
    )(x, *outs, w['gnorm_g'], w['w_out'], w['norm2_g'], w['w_ff1'], w['w_ff2'])


def _dot3(parts, rhs=None, lhs=None):
    if rhs is not None:
        return sum(_dot(p.astype(BF16), rhs) for p in parts)
    return sum(_dot(lhs, p.astype(BF16)) for p in parts)


def _page_scan_matrices(pg):
    r = np.arange(pg * N_HEADS)
    g, h = r // N_HEADS, r % N_HEADS
    s = np.arange(LANES)
    same = h[:, None] == h[None, :]
    return (jnp.asarray(s[:, None] >= s[None, :], BF16), jnp.asarray(same & (g[None, :] > g[:, None]), BF16),
            jnp.asarray(same, BF16))


def _same_seq_causal(b, t_new, n_cols, rows):
    col = lax.broadcasted_iota(jnp.int32, (rows, n_cols), 1)
    t = lax.broadcasted_iota(jnp.int32, (rows, n_cols), 0) % t_new
    return (col // t_new == b) & (col % t_new <= t)


def _fox_sample_body(pt_ref, q_ref, knew_ref, vnew_ref, tri_ref, later_ref, same_ref, kv_hbm, lf_hbm, o_ref,
                     kvbuf, lfbuf, sems, m_ref, l_ref, acc_ref, carry_ref, *, layer, pg):
    b, j = pl.program_id(0), pl.program_id(1)
    nb, nch = pl.num_programs(0), pl.num_programs(1)
    step = b * nch + j
    slot = step % 2
    t_new = q_ref.shape[0]
    rows = N_HEADS * t_new

    def copies(bb, jj, sl):
        first = (nch - 1 - jj) * pg
        out = []
        for g in range(pg):
            pid = pt_ref[bb, first + g]
            out.append(pltpu.make_async_copy(kv_hbm.at[layer, pid], kvbuf.at[sl, g], sems.at[0, sl]))
            out.append(pltpu.make_async_copy(lf_hbm.at[layer, pid], lfbuf.at[sl, pl.ds(g * N_HEADS, N_HEADS)],
                                             sems.at[1, sl]))
        return out

    @pl.when(step == 0)
    def _():
        for c in copies(b, j, slot):
            c.start()

    @pl.when(step + 1 < nb * nch)
    def _():
        nxt = step + 1
        for c in copies(nxt // nch, nxt % nch, 1 - slot):
            c.start()

    @pl.when(j == 0)
    def _():
        m_ref[...] = jnp.full_like(m_ref, NEG_BIG)
        l_ref[...] = jnp.zeros_like(l_ref)
        acc_ref[...] = jnp.zeros_like(acc_ref)
        carry_ref[...] = jnp.zeros_like(carry_ref)

    qv = q_ref[...]
    lane = lax.broadcasted_iota(jnp.int32, (t_new, LANES), 1)
    q4 = jnp.concatenate([jnp.where(lane < HEAD_DIM, qv[:, 0:128], qv[:, 128:256]),
                          jnp.where(lane < HEAD_DIM, qv[:, 256:384], qv[:, 384:512])], axis=-1)
    grp = lax.broadcasted_iota(jnp.int32, (t_new, GROUP_WIDTH), 1) // HEAD_DIM
    wq = jnp.concatenate([jnp.where(grp == h, q4, 0.0) for h in range(N_HEADS)], axis=0).astype(BF16)

    for c in copies(b, j, slot):
        c.wait()

    lf = lfbuf[slot]
    incl = _dot3(_split3(lf), rhs=tri_ref[...])
    tot = _split3(jnp.broadcast_to(incl[:, 0:1], lf.shape))
    suf = incl - lf + _dot3(tot, lhs=later_ref[...]) + carry_ref[...]
    carry_ref[...] += _dot3(tot, lhs=same_ref[...])

    scores = []
    for g in range(pg):
        bias = jnp.concatenate([jnp.broadcast_to(suf[g * N_HEADS + h:g * N_HEADS + h + 1], (t_new, LANES))
                                for h in range(N_HEADS)], axis=0)
        scores.append(_dot(wq, kvbuf[slot, g, 0:GROUP_WIDTH, :].astype(BF16)) + bias)
    m_old = m_ref[...]
    m = jnp.maximum(m_old, jnp.max(functools.reduce(jnp.maximum, scores), axis=-1, keepdims=True))
    alpha = jnp.exp(m_old - m)
    psum = jnp.zeros((rows, LANES), F32)
    acc = alpha * acc_ref[...]
    for g in range(pg):
        p = jnp.exp(scores[g] - m)
        psum = psum + p
        acc = acc + _nt(p.astype(BF16), kvbuf[slot, g, GROUP_WIDTH:, :].astype(BF16))
    l = alpha * l_ref[...] + jnp.sum(psum, axis=-1, keepdims=True)
    m_ref[...], l_ref[...], acc_ref[...] = m, l, acc

    @pl.when(j == nch - 1)
    def _():
        n_cols = knew_ref.shape[2]
        s = jnp.concatenate([_dot(qv[:, h * LANES:(h + 1) * LANES].astype(BF16), knew_ref[h])
                             for h in range(N_HEADS)], axis=0)
        s = jnp.where(_same_seq_causal(b, t_new, n_cols, rows), s, NEG_BIG)
        m2, l2, acc2 = _online_softmax_step(s, vnew_ref[...], (m, l, acc))
        o = acc2 / l2
        o_ref[...] = jnp.concatenate(
            [o[h * t_new:(h + 1) * t_new, h * HEAD_DIM:(h + 1) * HEAD_DIM] for h in range(N_HEADS)], axis=-1)


def _fox_sample(layer, page_table, q_aug, kaug_new, v_new, cache_kv, cache_lf, *, t_new, pg):
    db, n_pages = page_table.shape
    rows = N_HEADS * t_new
    mats = _page_scan_matrices(pg)
    full = lambda a: pl.BlockSpec(a.shape, lambda b, j, pt: (0,) * a.ndim)
    grid_spec = pltpu.PrefetchScalarGridSpec(
        num_scalar_prefetch=1,
        grid=(db, n_pages // pg),
        in_specs=[pl.BlockSpec((t_new, 4 * LANES), lambda b, j, pt: (b, 0)), full(kaug_new), full(v_new)]
                 + [full(a) for a in mats] + [pl.BlockSpec(memory_space=pl.ANY), pl.BlockSpec(memory_space=pl.ANY)],
        out_specs=pl.BlockSpec((t_new, GROUP_WIDTH), lambda b, j, pt: (b, 0)),
        scratch_shapes=[pltpu.VMEM((2, pg) + cache_kv.shape[2:], F32), pltpu.VMEM((2, pg * N_HEADS, LANES), F32),
                        pltpu.SemaphoreType.DMA((2, 2)),
                        pltpu.VMEM((rows, 1), F32), pltpu.VMEM((rows, 1), F32), pltpu.VMEM((rows, GROUP_WIDTH), F32),
                        pltpu.VMEM((pg * N_HEADS, LANES), F32)])
    return pl.pallas_call(
        functools.partial(_fox_sample_body, layer=layer, pg=pg),
        grid_spec=grid_spec,
        out_shape=jax.ShapeDtypeStruct((db * t_new, GROUP_WIDTH), F32),
        compiler_params=_cparams(("arbitrary", "arbitrary")),
        name="fox_sample",
    )(page_table, q_aug, kaug_new, v_new, *mats, cache_kv, cache_lf)


def _conv_sample_body(state_ref, glu_ref, w_ref, cb_ref, lng_ref, lnb_ref, o_ref, new_ref):
    n_state, t_new = state_ref.shape[1], glu_ref.shape[0]
    x = lambda i: state_ref[0, i] if i < n_state else glu_ref[i - n_state]
    for t in range(t_new):
        acc = x(t) * w_ref[0:1, :]
        for k in range(1, CONV_WIDTH):
            acc = acc + x(t + k) * w_ref[k:k + 1, :]
        o_ref[t] = _ln_silu(acc + cb_ref[...], lng_ref[...], lnb_ref[...])
    for i in range(n_state):
        new_ref[i] = x(i + t_new)


def _conv_sample(layer, state_t, glu_t, w):
    _, n_state, db, c = state_t.shape
    t_new = glu_t.shape[0]
    full = lambda a: pl.BlockSpec(a.shape, lambda i: (0,) * a.ndim)
    ins = [state_t, glu_t, w['conv_w'], w['conv_b'], w['conv_ln_g'], w['conv_ln_b']]
    return pl.pallas_call(
        _conv_sample_body,
        grid=(1,),
        in_specs=[pl.BlockSpec((1, n_state, db, c), lambda i: (layer, 0, 0, 0))] + [full(a) for a in ins[1:]],
        out_specs=[pl.BlockSpec((t_new, db, c), lambda i: (0, 0, 0)), pl.BlockSpec((n_state, db, c), lambda i: (0, 0, 0))],
        out_shape=[jax.ShapeDtypeStruct((t_new, db, c), F32), jax.ShapeDtypeStruct((n_state, db, c), F32)],
        compiler_params=_cparams(("arbitrary",)),
        name="conv_sample",
    )(*ins)


_CMP_PER_PAGE = 4


def _compress_sample_body(pt_ref, *refs, n_group):
    x_refs, (pe_ref, phi_ref, o_ref, x_scr) = refs[:n_group], refs[n_group:]
    page = x_scr.shape[0] // n_group
    for g, x_ref in enumerate(x_refs):
        x_scr[g * page:(g + 1) * page, :] = jnp.transpose(x_ref[0, 0] + pe_ref[...])
    n_blocks = n_group * _CMP_PER_PAGE
    acc = jnp.zeros((n_blocks, LANES), F32)
    for r in range(CMP_BLOCK):
        rows = x_scr[pl.ds(r, n_blocks, stride=CMP_BLOCK), :]
        acc = acc + _dot(rows.astype(BF16), phi_ref[r * LANES:(r + 1) * LANES, :])
    o_ref[0] = acc


def _compress_sample(layer, page_table, cache_nsa, pe_col, phi, *, n_group):
    db, n_pages = page_table.shape
    page = cache_nsa.shape[3]
    page_spec = lambda g: pl.BlockSpec((1, 1, LANES, page), lambda b, j, pt: (layer, pt[b, j * n_group + g], 0, 0))
    rows = n_group * _CMP_PER_PAGE
    grid_spec = pltpu.PrefetchScalarGridSpec(
        num_scalar_prefetch=1,
        grid=(db, n_pages // n_group),
        in_specs=[page_spec(g) for g in range(n_group)]
                 + [pl.BlockSpec(pe_col.shape, lambda b, j, pt: (0, 0)), pl.BlockSpec(phi.shape, lambda b, j, pt: (0, 0))],
        out_specs=pl.BlockSpec((1, rows, LANES), lambda b, j, pt: (b, j, 0)),
        scratch_shapes=[pltpu.VMEM((n_group * page, LANES), F32)])
    return pl.pallas_call(
        functools.partial(_compress_sample_body, n_group=n_group),
        grid_spec=grid_spec,
        out_shape=jax.ShapeDtypeStruct((db, n_pages * _CMP_PER_PAGE, LANES), F32),
        compiler_params=_cparams(("arbitrary", "arbitrary")),
        name="nsa_compress_sample",
    )(page_table, *([cache_nsa] * n_group), pe_col, phi)


def _nsa_local_sample_body(q_ref, g_ref, kc_ref, vc_ref, win_ref, new_ref, o_ref, flag_ref, *, past_len):
    b = pl.program_id(0)
    t_new = q_ref.shape[0]
    rows = N_HEADS * t_new
    nc = kc_ref.shape[2]
    half = nc // 2
    qs = _stack_heads(q_ref[...]).astype(BF16)
    tq = lax.broadcasted_iota(jnp.int32, (rows, 1), 0) % t_new
    qpos = past_len + tq

    c = lax.broadcasted_iota(jnp.int32, (1, nc), 1)
    blk = jnp.where(c < half, 2 * c, 2 * (c - half) + 1)
    p_cmp = _masked_softmax(_dot(qs, kc_ref[0]), (blk + 1) * CMP_BLOCK - 1 <= qpos)
    o_cmp = _dot(p_cmp.astype(BF16), vc_ref[0])
    imp = p_cmp[0:t_new]
    for h in range(1, N_HEADS):
        imp = imp + p_cmp[h * t_new:(h + 1) * t_new]
    imp = imp[:, :half] + imp[:, half:]
    cur = jnp.full((t_new, 1), past_len // SEL_BLOCK, jnp.int32)
    sel = _select_blocks(imp, cur, min(N_SELECT, half + 1) - 1)
    j = lax.broadcasted_iota(jnp.int32, sel.shape, 1)
    t = lax.broadcasted_iota(jnp.int32, sel.shape, 0)
    weight = jnp.left_shift(1, 2 * t + j % 2).astype(F32)
    colsum = jnp.sum(jnp.where(sel, weight, 0.0), axis=0, keepdims=True)
    flag_ref[0] = colsum + pltpu.roll(colsum, half - 1, 1)

    w = win_ref.shape[3]
    n_cols = new_ref.shape[1]
    s_old = _dot(qs, win_ref[0, 0, 0:HEAD_DIM, :].astype(BF16))
    s_new = _dot(qs, new_ref[128:192, :])
    i_old = lax.broadcasted_iota(jnp.int32, (1, w), 1)
    mask = jnp.concatenate([jnp.broadcast_to(i_old + (WINDOW - w) > tq, (rows, w)),
                            _same_seq_causal(b, t_new, n_cols, rows)], axis=-1)
    p_win = _masked_softmax(jnp.concatenate([s_old, s_new], axis=-1), mask).astype(BF16)
    o_win = _nt(p_win[:, :w], win_ref[0, 0, HEAD_DIM:, :].astype(BF16)) + _nt(p_win[:, w:], new_ref[192:256, :])

    g = g_ref[...]
    outs = [g[:, 3 * h:3 * h + 1] * o_cmp[h * t_new:(h + 1) * t_new]
            + g[:, 3 * h + 2:3 * h + 3] * o_win[h * t_new:(h + 1) * t_new] for h in range(N_HEADS)]
    o_ref[...] = jnp.concatenate(outs, axis=-1)


def _nsa_local_sample(layer, qc, gc, kc_t, vc, win_t, new_bf, *, t_new, past_len):
    db = kc_t.shape[0]
    nc = kc_t.shape[2]
    w = win_t.shape[3]
    rows = lambda width: pl.BlockSpec((t_new, width), lambda b: (b, 0))
    return pl.pallas_call(
        functools.partial(_nsa_local_sample_body, past_len=past_len),
        grid=(db,),
        in_specs=[rows(GROUP_WIDTH), rows(LANES),
                  pl.BlockSpec((1, HEAD_DIM, nc), lambda b: (b, 0, 0)),
                  pl.BlockSpec((1, nc, HEAD_DIM), lambda b: (b, 0, 0)),
                  pl.BlockSpec((1, 1, 2 * HEAD_DIM, w), lambda b: (layer, b, 0, 0)),
                  pl.BlockSpec(new_bf.shape, lambda b: (0, 0))],
        out_specs=[rows(GROUP_WIDTH), pl.BlockSpec((1, 1, nc // 2), lambda b: (b, 0, 0))],
        out_shape=[jax.ShapeDtypeStruct((db * t_new, GROUP_WIDTH), F32),
                   jax.ShapeDtypeStruct((db, 1, nc // 2), F32)],
        compiler_params=_cparams(("arbitrary",)),
        name="nsa_local_sample",
    )(qc, gc, kc_t, vc, win_t, new_bf)


_SEL_GROUP = 4


def _nsa_sel_sample_body(pt_ref, fl_ref, q_ref, g_ref, part_ref, new_ref, kv_hbm, o_ref,
                         buf, sems, cflag, count, *, layer):
    b = pl.program_id(0)
    nb = pl.num_programs(0)
    n_pages = fl_ref.shape[1]
    slot = b % 2
    t_new = q_ref.shape[0]
    rows = N_HEADS * t_new

    def copy(bb, p, sl, k):
        src = kv_hbm.at[layer, pt_ref[bb, p], pl.ds(2 * HEAD_DIM, 2 * HEAD_DIM)]
        return pltpu.make_async_copy(src, buf.at[sl, k], sems.at[sl])

    def start_all(bb, sl):
        def body(p, k):
            flag = fl_ref[bb, p]

            @pl.when(flag != 0)
            def _():
                copy(bb, p, sl, k).start()
                cflag[sl, k] = flag
            return k + (flag != 0).astype(jnp.int32)
        count[sl] = lax.fori_loop(0, n_pages, body, 0)

    @pl.when(b == 0)
    def _():
        start_all(b, slot)

    @pl.when(b + 1 < nb)
    def _():
        start_all(b + 1, 1 - slot)

    qs = _stack_heads(q_ref[...]).astype(BF16)
    shamt = (2 * (lax.broadcasted_iota(jnp.int32, (rows, LANES), 0) % t_new)
             + lax.broadcasted_iota(jnp.int32, (rows, LANES), 1) // SEL_BLOCK)
    n_fetched = count[slot]

    def wait_body(k, c):
        copy(b, 0, slot, k).wait()
        return c
    lax.fori_loop(0, n_fetched, wait_body, 0)

    def group(gi, carry):
        m_old, l_old, acc = carry
        scores, entries = [], []
        for u in range(_SEL_GROUP):
            k = gi * _SEL_GROUP + u
            entry = jnp.minimum(k, n_fetched - 1)
            flag = jnp.where(k < n_fetched, cflag[slot, entry], 0)
            picked = (jnp.right_shift(jnp.full((rows, LANES), flag, jnp.int32), shamt) & 1) == 1
            scores.append(jnp.where(picked, _dot(qs, buf[slot, entry, 0:HEAD_DIM, :].astype(BF16)), NEG_BIG))
            entries.append(entry)
        m = jnp.maximum(m_old, jnp.max(functools.reduce(jnp.maximum, scores), axis=-1, keepdims=True))
        alpha = jnp.exp(m_old - m)
        psum = jnp.zeros((rows, LANES), F32)
        acc = alpha * acc
        for u in range(_SEL_GROUP):
            p = jnp.exp(scores[u] - m)
            psum = psum + p
            acc = acc + _nt(p.astype(BF16), buf[slot, entries[u], HEAD_DIM:, :].astype(BF16))
        return m, alpha * l_old + jnp.sum(psum, axis=-1, keepdims=True), acc

    carry = lax.fori_loop(0, (n_fetched + _SEL_GROUP - 1) // _SEL_GROUP, group, _softmax_init(rows, HEAD_DIM))
    s = jnp.where(_same_seq_causal(b, t_new, new_ref.shape[1], rows), _dot(qs, new_ref[0:64, :]), NEG_BIG)
    m, l, acc = _online_softmax_step(s, new_ref[64:128, :], carry)
    o_sel = acc / l
    g = g_ref[...]
    outs = [g[:, 3 * h + 1:3 * h + 2] * o_sel[h * t_new:(h + 1) * t_new] for h in range(N_HEADS)]
    o_ref[...] = part_ref[...] + jnp.concatenate(outs, axis=-1)


def _nsa_sel_sample(layer, page_table, flags, qc, gc, part, new_bf, cache_nsa, *, t_new):
    db, n_pages = page_table.shape
    rows = N_HEADS * t_new
    tile = lambda width: pl.BlockSpec((t_new, width), lambda b, pt, fl: (b, 0))
    grid_spec = pltpu.PrefetchScalarGridSpec(
        num_scalar_prefetch=2,
        grid=(db,),
        in_specs=[tile(GROUP_WIDTH), tile(LANES), tile(GROUP_WIDTH),
                  pl.BlockSpec(new_bf.shape, lambda b, pt, fl: (0, 0)),
                  pl.BlockSpec(memory_space=pl.ANY)],
        out_specs=tile(GROUP_WIDTH),
        scratch_shapes=[pltpu.VMEM((2, n_pages, 2 * HEAD_DIM, LANES), F32), pltpu.SemaphoreType.DMA((2,)),
                        pltpu.SMEM((2, n_pages), jnp.int32), pltpu.SMEM((2,), jnp.int32)])
    return pl.pallas_call(
        functools.partial(_nsa_sel_sample_body, layer=layer),
        grid_spec=grid_spec,
        out_shape=jax.ShapeDtypeStruct((db * t_new, GROUP_WIDTH), F32),
        compiler_params=_cparams(("arbitrary",)),
        name="nsa_sel_sample",
    )(page_table, flags, qc, gc, part, new_bf, cache_nsa)


_SPLITS = np.cumsum([0, 256, 256, 256, 4, 512, 256, 384, 12, 256, 256])


def _prep_layer(l, P):
    w_in = P['w_in'][l]
    sec = [w_in[:, _SPLITS[i]:_SPLITS[i + 1]] for i in range(10)]
    qa, ka, va, fa, glu, qc, kvc, gc, ud, vd = sec
    w_row = jnp.concatenate([qa, glu, qc, ud, vd, jnp.pad(gc, ((0, 0), (0, LANES - 12)))], axis=1).astype(BF16)
    w_col = jnp.concatenate([ka, va, kvc, jnp.pad(fa, ((0, 0), (0, 12)))], axis=1).T.astype(BF16)
    row = lambda v: v.reshape(1, -1)
    phi_k = P['nsa_phi_k'][l].reshape(CMP_BLOCK, 1, HEAD_DIM, HEAD_DIM)
    phi_v = P['nsa_phi_v'][l].reshape(CMP_BLOCK, 1, HEAD_DIM, HEAD_DIM)
    zero = jnp.zeros_like(phi_k)
    phi = jnp.concatenate([jnp.concatenate([phi_k, zero], axis=-1), jnp.concatenate([zero, phi_v], axis=-1)], axis=1)
    return dict(
        pe_flat=jnp.transpose(P['nsa_pe'][l], (1, 0, 2)).reshape(1, CMP_BLOCK * LANES),
        pe_col=jnp.tile(jnp.transpose(P['nsa_pe'][l], (0, 2, 1)).reshape(LANES, CMP_BLOCK), (1, _CMP_PER_PAGE)),
        phi=phi.reshape(CMP_BLOCK * LANES, LANES).astype(BF16),
        norm1_g=row(P['norm1_g'][l]), w_row=w_row, w_col=w_col,
        gqa=row(jnp.tile(P['fox_qn_g'][l], N_HEADS)), gka=P['fox_kn_g'][l].reshape(HEAD_DIM, 1),
        bf=jnp.pad(P['fox_bf'][l], (0, 4)).reshape(8, 1),
        gqc=row(jnp.tile(P['nsa_qn_g'][l], N_HEADS)), gkc=P['nsa_kn_g'][l].T,
        gmlp_ln_g=row(P['gmlp_ln_g'][l]), gmlp_ln_b=row(P['gmlp_ln_b'][l]),
        gmlp_ws=P['gmlp_ws'][l], gmlp_bs=P['gmlp_bs'][l],
        gnorm_g=P['gnorm_g'][l], w_out=P['w_out'][l].astype(BF16), norm2_g=row(P['norm2_g'][l]),
        w_ff1=P['w_ff1'][l].astype(BF16), w_ff2=P['w_ff2'][l].astype(BF16),
        conv_w=P['conv_w'][l], conv_b=row(P['conv_b'][l]),
        conv_ln_g=row(P['conv_ln_g'][l]), conv_ln_b=row(P['conv_ln_b'][l]),
    )


def _rope_tables(pos):
    inv = ROPE_THETA ** (-jnp.arange(HALF, dtype=F32) / HALF)
    ang = pos.astype(F32)[:, None] * inv
    cos, sin = jnp.cos(ang), jnp.sin(ang)
    return dict(cos_r=jnp.tile(cos, (1, 4)), sin_r=jnp.tile(jnp.concatenate([-sin, sin], axis=1), (1, 2)),
                cos_t=cos.T, sin_t=sin.T)


def _const_tables():
    g = np.arange(GROUP_WIDTH) // HEAD_DIM
    return dict(gsum=jnp.asarray(g[:, None] == g[None, :], BF16))


def _gmlp_tables(w, seq_len, n_seq):
    t = min(seq_len, CHUNK)
    wm = (w['gmlp_ws'] * jnp.tril(jnp.ones((CHUNK, CHUNK), F32)))[:, :t, :t]
    bs = w['gmlp_bs'][:, :t]
    if seq_len < CHUNK:
        eye = jnp.eye(n_seq, dtype=F32)
        wm = jnp.einsum('ab,gts->gatbs', eye, wm).reshape(N_HEADS, n_seq * t, n_seq * t)
        bs = jnp.tile(bs, (1, n_seq))
    c = wm.shape[1]
    return dict(wm=wm.reshape(N_HEADS * c, c).astype(BF16), bs_tab=jnp.repeat(bs.T, HEAD_DIM, axis=1))


def _layer_prompt(x, w, consts, *, tm, ta, tk, tf):
    B, S, D = x.shape
    tabs = dict(consts, **_gmlp_tables(w, S, B))
    o = _proj(x, w, tabs, tm=tm, chunk=CHUNK)
    o_a = _fox_attn(o['qat'], o['karow'], o['vbf'], t=ta)
    o_b = _conv_prompt(o['glu'], w, tm=tm)
    o_c = _nsa_prompt(o, w, t=tk, tk=tk, tl=min(4 * LANES, S))
    flat = lambda a: a.reshape(B * S, a.shape[-1])
    y = _merge_ffn(flat(x), [flat(o_a), flat(o_b), flat(o_c), flat(o['od'])], w, tm=min(2 * tm, B * S), tf=tf)
    wp = min(WINDOW, S)
    states = dict(fox_kv=o['foxkv'], fox_logf=o['logf'], nsa_kv=o['nsakv'],
                  nsa_win=o['nsawin'][:, :, S - wp:], conv=o['glu'][:, S - (CONV_WIDTH - 1):])
    return y.reshape(B, S, D), states


def _even_odd(cmp):
    n = cmp.shape[1]
    order = np.concatenate([np.arange(0, n, 2), np.arange(1, n, 2)])
    cmp = cmp[:, order].astype(BF16)
    return jnp.swapaxes(cmp[:, :, :HEAD_DIM], 1, 2), cmp[:, :, HEAD_DIM:]


def _prep_caches(cache_fox_kv, cache_fox_logf, cache_nsa_kv, state_nsa_win, state_conv):
    L, pool, page = cache_fox_kv.shape[:3]
    db = state_nsa_win.shape[1]
    return dict(
        fox_kv=jnp.transpose(cache_fox_kv, (0, 1, 3, 4, 5, 2)).reshape(L, pool, 2 * GROUP_WIDTH, page),
        fox_lf=jnp.transpose(cache_fox_logf, (0, 1, 3, 2)),
        nsa_kv=jnp.transpose(cache_nsa_kv, (0, 1, 3, 4, 2)).reshape(L, pool, 4 * HEAD_DIM, page),
        win=jnp.transpose(state_nsa_win, (0, 1, 3, 4, 2)).reshape(L, db, 2 * HEAD_DIM, -1),
        conv=jnp.transpose(state_conv, (0, 2, 1, 3)),
    )


def _layer_sample(l, xs, w, caches, consts, page_table, *, past_len, pg, tf):
    db, t_new, D = xs.shape
    R = db * t_new
    tabs = dict(consts, **_gmlp_tables(w, t_new, db))
    o = {k: v[0] for k, v in _proj(xs.reshape(1, R, D), w, tabs, tm=R, chunk=R).items()}
    o_a = _fox_sample(l, page_table, o['qa'].astype(F32), o['kaug'], o['vbf'], caches['fox_kv'], caches['fox_lf'],
                      t_new=t_new, pg=pg)
    glu_t = jnp.swapaxes(o['glu'].reshape(db, t_new, GROUP_WIDTH), 0, 1)
    o_b_t, conv_new = _conv_sample(l, caches['conv'], glu_t, w)
    o_b = jnp.swapaxes(o_b_t, 0, 1).reshape(R, GROUP_WIDTH)
    cmp = _compress_sample(l, page_table, caches['nsa_kv'], w['pe_col'], w['phi'], n_group=pg)
    kc_t, vc = _even_odd(cmp)
    qc = o['qc'].astype(F32)
    part, flags = _nsa_local_sample(l, qc, o['gc'], kc_t, vc, caches['win'], o['nsabf'],
                                    t_new=t_new, past_len=past_len)
    flags = flags[:, 0, ::2].astype(jnp.int32)
    o_c = _nsa_sel_sample(l, page_table, flags, qc, o['gc'], part, o['nsabf'], caches['nsa_kv'], t_new=t_new)
    y = _merge_ffn(xs.reshape(R, D), [o_a, o_b, o_c, o['od']], w, tm=R, tf=tf)
    rows = lambda a, *shape: a.T.reshape(db, t_new, *shape)
    win_new = jnp.swapaxes(o['nsawin'].reshape(2 * HEAD_DIM, db, t_new), 0, 1)
    win = jnp.concatenate([caches['win'][l][:, :, t_new:], win_new], axis=-1)
    states = dict(fox_kv=rows(o['foxkv'], 2, N_HEADS, HEAD_DIM), fox_logf=rows(o['logf'], N_HEADS),
                  nsa_kv=rows(o['nsakv'], 4, HEAD_DIM),
                  nsa_win=jnp.transpose(win.reshape(db, 2, HEAD_DIM, -1), (0, 3, 1, 2)),
                  conv=jnp.swapaxes(conv_new, 0, 1), gmlp_v=o['vn'].reshape(db, t_new, GROUP_WIDTH))
    return y.reshape(db, t_new, D), states


def _scan_matrix(n, seg):
    i = np.arange(n)
    return jnp.asarray((i[:, None] <= i[None, :]) & (i[:, None] // seg == i[None, :] // seg), BF16)


_PARAM_NAMES = ('norm1_g', 'w_in', 'fox_bf', 'fox_qn_g', 'fox_kn_g', 'conv_w', 'conv_b', 'conv_ln_g', 'conv_ln_b',
                'nsa_qn_g', 'nsa_kn_g', 'nsa_pe', 'nsa_phi_k', 'nsa_phi_v', 'gmlp_ln_g', 'gmlp_ln_b', 'gmlp_ws',
                'gmlp_bs', 'gnorm_g', 'w_out', 'norm2_g', 'w_ff1', 'w_ff2')


def kernel(x_prompt, x_sample, cache_fox_kv, cache_fox_logf, cache_nsa_kv, state_nsa_win, state_conv, page_table,
           *params):
    P = dict(zip(_PARAM_NAMES, params))
    depth = P['w_in'].shape[0]
    B, S, D = x_prompt.shape
    DB, T, _ = x_sample.shape
    n_pages, page = page_table.shape[1], cache_fox_kv.shape[2]
    past_len = n_pages * page
    assert past_len % SEL_BLOCK == 0 and T <= SEL_BLOCK and past_len >= WINDOW and page == LANES
    tm = 512
    consts_p = dict(_rope_tables(jnp.arange(S)), **_const_tables(), utri=_scan_matrix(tm, tm))
    consts_s = dict(_rope_tables(jnp.tile(past_len + jnp.arange(T), DB)), **_const_tables(),
                    utri=_scan_matrix(DB * T, T))
    caches = _prep_caches(cache_fox_kv, cache_fox_logf, cache_nsa_kv, state_nsa_win, state_conv)
    xp, xs = x_prompt, x_sample
    st_p, st_s = [], []
    for l in range(depth):
        w = _prep_layer(l, P)
        xp, sp = _layer_prompt(xp, w, consts_p, tm=tm, ta=512, tk=512, tf=1024)
        xs, ss = _layer_sample(l, xs, w, caches, consts_s, page_table, past_len=past_len, pg=32, tf=1024)
        st_p.append(sp)
        st_s.append(ss)
    stack_p = lambda k: jnp.stack([s[k] for s in st_p])
    stack_s = lambda k: jnp.stack([s[k] for s in st_s])
    fox_kv_p = jnp.transpose(stack_p('fox_kv').reshape(depth, B, 2, N_HEADS, HEAD_DIM, S), (0, 1, 5, 2, 3, 4))
    fox_logf_p = jnp.transpose(stack_p('fox_logf'), (0, 1, 3, 2))
    nsa_kv_p = jnp.transpose(stack_p('nsa_kv').reshape(depth, B, 4, HEAD_DIM, S), (0, 1, 4, 2, 3))
    nsa_win_p = jnp.transpose(stack_p('nsa_win').reshape(depth, B, 2, HEAD_DIM, -1), (0, 1, 4, 2, 3))
    return (xp, xs, fox_kv_p, stack_s('fox_kv'), fox_logf_p, stack_s('fox_logf'),
            nsa_kv_p, stack_s('nsa_kv'), nsa_win_p, stack_s('nsa_win'),
            stack_p('conv'), stack_s('conv'), stack_s('gmlp_v'))
```

```python
import functools

import jax
import jax.numpy as jnp
import numpy as np
from jax import lax
from jax.experimental import pallas as pl
from jax.experimental.pallas import tpu as pltpu

F32 = jnp.float32
BF16 = jnp.bfloat16

HEAD_DIM = 64
HALF = HEAD_DIM // 2
GROUP_WIDTH = 256
N_HEADS = GROUP_WIDTH // HEAD_DIM
CONV_WIDTH = 31
CMP_BLOCK = 32
SEL_BLOCK = 64
N_SELECT = 16
WINDOW = 512
CHUNK = 128
ROPE_THETA = 10000.0
EPS = 1e-6
FORCED_SCORE = 1e4
Q_SCALE = HEAD_DIM ** -0.5
LOG2E = 1.4426950408889634
NEG_BIG = -1e30
SEL_NEG = -32768.0
LANES = 128
VMEM_LIMIT = 56 * 1024 * 1024


def _cparams(sem):
    return pltpu.CompilerParams(dimension_semantics=sem, vmem_limit_bytes=VMEM_LIMIT)


def _nt(a, b):
    return lax.dot_general(a, b, (((1,), (1,)), ((), ())), preferred_element_type=F32)


def _dot(a, b):
    return jnp.dot(a, b, preferred_element_type=F32)


def _split3(x):
    h = x.astype(BF16).astype(F32)
    r = x - h
    m = r.astype(BF16).astype(F32)
    l = (r - m).astype(BF16).astype(F32)
    return h, m, l


def _log_sigmoid(x):
    return jnp.minimum(x, 0.0) - jnp.log1p(jnp.exp(-jnp.abs(x)))


def _group_mean_sq(x, gsum):
    x2 = x * x
    hi = x2.astype(BF16)
    lo = (x2 - hi.astype(F32)).astype(BF16)
    return (_dot(hi, gsum) + _dot(lo, gsum)) * (1.0 / HEAD_DIM)


def _rope_rows(x, cos, sin_signed):
    lane = lax.broadcasted_iota(jnp.int32, x.shape, 1)
    first_half = (lane % HEAD_DIM) < HALF
    swapped = jnp.where(first_half, pltpu.roll(x, LANES - HALF, 1), pltpu.roll(x, HALF, 1))
    return x * cos + swapped * sin_signed


_R_QA, _R_GLU, _R_QC, _R_UD, _R_VD, _R_SMALL, _R_END = 0, 256, 768, 1024, 1280, 1536, 1664
_C_KA, _C_VA, _C_KVC, _C_FA, _C_END = 0, 256, 512, 896, 912


def _proj_body(x_ref, g1_ref, wrow_ref, wcol_ref, cosr_ref, sinr_ref, cost_ref, sint_ref,
               gsum_ref, gqa_ref, gka_ref, bf_ref, gqc_ref, gkc_ref, lng_ref, lnb_ref,
               wm_ref, bstab_ref, utri_ref,
               qa_ref, foxkv_ref, kaug_ref, vbf_ref, logf_ref, glu_ref, qc_ref, nsakv_ref,
               nsawin_ref, nsabf_ref, cmprow_ref, gc_ref, od_ref, vn_ref, qat_ref, karow_ref, qct_ref, ksrow_ref,
               carry_ref, *, chunk):
    tm = x_ref.shape[1]

    @pl.when(pl.program_id(1) == 0)
    def _():
        carry_ref[...] = jnp.zeros_like(carry_ref)

    x = x_ref[0]
    ms = jnp.mean(x * x, axis=-1, keepdims=True)
    xn = ((x * lax.rsqrt(ms + EPS)) * g1_ref[...]).astype(BF16)
    zr = _dot(xn, wrow_ref[...])
    zc = _nt(wcol_ref[...], xn)
    gsum = gsum_ref[...]

    qa = zr[:, _R_QA:_R_QA + 256]
    qa = qa * lax.rsqrt(_group_mean_sq(qa, gsum) + EPS) * gqa_ref[...] * Q_SCALE
    lane = lax.broadcasted_iota(jnp.int32, (tm, LANES), 1)
    for h in range(N_HEADS):
        src = qa[:, (h // 2) * LANES:(h // 2 + 1) * LANES]
        if h % 2 == 0:
            aug = jnp.where(lane < HEAD_DIM, src, jnp.where(lane < HEAD_DIM + 3, 1.0, 0.0))
        else:
            aug = jnp.where(lane >= HEAD_DIM, src, jnp.where(lane < 3, 1.0, 0.0))
        qa_ref[0, :, h * LANES:(h + 1) * LANES] = aug.astype(BF16)
        is_q = (lane < HEAD_DIM) if h % 2 == 0 else (lane >= HEAD_DIM)
        qat_ref[0, h] = jnp.transpose(aug * jnp.where(is_q, LOG2E, 1.0)).astype(BF16)

    logf = _log_sigmoid(zc[_C_FA:_C_FA + 8] + bf_ref[...])
    logf_ref[0] = logf[0:N_HEADS]
    parts = _split3(logf)
    l3 = jnp.concatenate(parts, axis=0).astype(BF16)
    cs = _dot(l3, utri_ref[...])
    fcum = cs[0:8] + cs[8:16] + cs[16:24] + carry_ref[:, 0:1]
    carry_ref[...] = jnp.broadcast_to(fcum[:, tm - 1:tm], carry_ref.shape)
    nfh, nfm, nfl = _split3(-fcum)
    nf2 = _split3(-fcum * LOG2E)
    row8 = lax.broadcasted_iota(jnp.int32, (8, tm), 0)
    zeros56 = jnp.zeros((HEAD_DIM - 8, tm), F32)
    gka = gka_ref[:, 0:1]
    for h in range(N_HEADS):
        k = zc[_C_KA + h * HEAD_DIM:_C_KA + (h + 1) * HEAD_DIM]
        k = k * lax.rsqrt(jnp.mean(k * k, axis=0, keepdims=True) + EPS) * gka
        v = zc[_C_VA + h * HEAD_DIM:_C_VA + (h + 1) * HEAD_DIM]
        foxkv_ref[0, h * HEAD_DIM:(h + 1) * HEAD_DIM, :] = k
        foxkv_ref[0, GROUP_WIDTH + h * HEAD_DIM:GROUP_WIDTH + (h + 1) * HEAD_DIM, :] = v
        vbf_ref[0, h * HEAD_DIM:(h + 1) * HEAD_DIM, :] = v.astype(BF16)
        extra8 = jnp.where(row8 == 0, nfh[h:h + 1],
                           jnp.where(row8 == 1, nfm[h:h + 1],
                                     jnp.where(row8 == 2, nfl[h:h + 1], 0.0)))
        extra = jnp.concatenate([extra8, zeros56], axis=0)
        pieces = [k, extra] if h % 2 == 0 else [extra, k]
        kaug_ref[0, h] = jnp.concatenate(pieces, axis=0).astype(BF16)
        extra2 = jnp.where(row8 == 0, nf2[0][h:h + 1],
                           jnp.where(row8 == 1, nf2[1][h:h + 1],
                                     jnp.where(row8 == 2, nf2[2][h:h + 1], 0.0)))
        extra2 = jnp.concatenate([extra2, zeros56], axis=0)
        kaug2 = jnp.concatenate([k, extra2] if h % 2 == 0 else [extra2, k], axis=0)
        karow_ref[0, :, h * LANES:(h + 1) * LANES] = jnp.transpose(kaug2).astype(BF16)

    glu_in = zr[:, _R_GLU:_R_GLU + 512]
    glu_ref[0] = glu_in[:, :256] * jax.nn.sigmoid(glu_in[:, 256:])

    qc = zr[:, _R_QC:_R_QC + 256]
    qc = qc * lax.rsqrt(_group_mean_sq(qc, gsum) + EPS) * gqc_ref[...]
    cosr, sinr = cosr_ref[...], sinr_ref[...]
    for p in range(2):
        qh = _rope_rows(qc[:, p * LANES:(p + 1) * LANES], cosr, sinr) * Q_SCALE
        qc_ref[0, :, p * LANES:(p + 1) * LANES] = qh.astype(BF16)
        qct_ref[0, p * LANES:(p + 1) * LANES, :] = jnp.transpose(qh * LOG2E).astype(BF16)
    gc_ref[0] = jax.nn.sigmoid(zr[:, _R_SMALL:_R_SMALL + LANES])
    cost, sint = cost_ref[...], sint_ref[...]
    keys = []
    for b in range(3):
        kb = zc[_C_KVC + 2 * b * HEAD_DIM:_C_KVC + (2 * b + 1) * HEAD_DIM]
        kb = kb * lax.rsqrt(jnp.mean(kb * kb, axis=0, keepdims=True) + EPS) * gkc_ref[:, b:b + 1]
        x1, x2 = kb[:HALF], kb[HALF:]
        keys.append(jnp.concatenate([x1 * cost - x2 * sint, x2 * cost + x1 * sint], axis=0))
    vals = [zc[_C_KVC + (2 * b + 1) * HEAD_DIM:_C_KVC + (2 * b + 2) * HEAD_DIM] for b in range(3)]
    nsakv_ref[0, 0:64, :] = keys[0]
    nsakv_ref[0, 64:128, :] = vals[0]
    nsakv_ref[0, 128:192, :] = keys[1]
    nsakv_ref[0, 192:256, :] = vals[1]
    nsawin_ref[0, 0:64, :] = keys[2]
    nsawin_ref[0, 64:128, :] = vals[2]
    nsabf_ref[0, 0:64, :] = keys[1].astype(BF16)
    nsabf_ref[0, 64:128, :] = vals[1].astype(BF16)
    nsabf_ref[0, 128:192, :] = keys[2].astype(BF16)
    nsabf_ref[0, 192:256, :] = vals[2].astype(BF16)
    cmprow_ref[0] = jnp.transpose(jnp.concatenate([keys[0], vals[0]], axis=0))
    ksrow_ref[0, :, 0:LANES] = jnp.transpose(jnp.concatenate([keys[1], jnp.zeros_like(keys[1])], axis=0)).astype(BF16)
    blk = (pl.program_id(1) * tm + lax.broadcasted_iota(jnp.int32, (tm, LANES), 0)) // SEL_BLOCK
    ksrow_ref[0, :, LANES:] = jnp.where(blk == lane, 1.0, 0.0).astype(BF16)

    ud = zr[:, _R_UD:_R_UD + 256]
    vd = zr[:, _R_VD:_R_VD + 256]
    mu = jnp.mean(vd, axis=-1, keepdims=True)
    var = jnp.mean(jnp.square(vd - mu), axis=-1, keepdims=True)
    vn = (vd - mu) * lax.rsqrt(var + EPS) * lng_ref[...] + lnb_ref[...]
    vn_ref[0] = vn
    grp = lax.broadcasted_iota(jnp.int32, (chunk, GROUP_WIDTH), 1) // HEAD_DIM
    wm = wm_ref[...]
    for c in range(tm // chunk):
        r = _dot(wm, vn[c * chunk:(c + 1) * chunk].astype(BF16))
        mixed = bstab_ref[...]
        for g in range(N_HEADS):
            mixed = mixed + jnp.where(grp == g, r[g * chunk:(g + 1) * chunk], 0.0)
        od_ref[0, c * chunk:(c + 1) * chunk, :] = ud[c * chunk:(c + 1) * chunk] * mixed


def _proj(x, w, tabs, *, tm, chunk):
    B, S, D = x.shape
    ns = S // tm
    row = lambda width: pl.BlockSpec((1, tm, width), lambda b, i: (b, i, 0))
    col = lambda height: pl.BlockSpec((1, height, tm), lambda b, i: (b, 0, i))
    full = lambda a: pl.BlockSpec(a.shape, lambda b, i: (0,) * a.ndim)
    ins = [x, w['norm1_g'], w['w_row'], w['w_col'], tabs['cos_r'], tabs['sin_r'], tabs['cos_t'], tabs['sin_t'],
           tabs['gsum'], w['gqa'], w['gka'], w['bf'], w['gqc'], w['gkc'], w['gmlp_ln_g'], w['gmlp_ln_b'],
           tabs['wm'], tabs['bs_tab'], tabs['utri']]
    in_specs = [row(D), full(ins[1]), full(ins[2]), full(ins[3]),
                pl.BlockSpec((tm, LANES), lambda b, i: (i, 0)), pl.BlockSpec((tm, LANES), lambda b, i: (i, 0)),
                pl.BlockSpec((HALF, tm), lambda b, i: (0, i)), pl.BlockSpec((HALF, tm), lambda b, i: (0, i))]
    in_specs += [full(a) for a in ins[8:]]
    outs = dict(
        qa=(jax.ShapeDtypeStruct((B, S, 4 * LANES), BF16), row(4 * LANES)),
        foxkv=(jax.ShapeDtypeStruct((B, 2 * GROUP_WIDTH, S), F32), col(2 * GROUP_WIDTH)),
        kaug=(jax.ShapeDtypeStruct((B, N_HEADS, LANES, S), BF16),
              pl.BlockSpec((1, N_HEADS, LANES, tm), lambda b, i: (b, 0, 0, i))),
        vbf=(jax.ShapeDtypeStruct((B, GROUP_WIDTH, S), BF16), col(GROUP_WIDTH)),
        logf=(jax.ShapeDtypeStruct((B, N_HEADS, S), F32), col(N_HEADS)),
        glu=(jax.ShapeDtypeStruct((B, S, GROUP_WIDTH), F32), row(GROUP_WIDTH)),
        qc=(jax.ShapeDtypeStruct((B, S, GROUP_WIDTH), BF16), row(GROUP_WIDTH)),
        nsakv=(jax.ShapeDtypeStruct((B, 256, S), F32), col(256)),
        nsawin=(jax.ShapeDtypeStruct((B, 128, S), F32), col(128)),
        nsabf=(jax.ShapeDtypeStruct((B, 256, S), BF16), col(256)),
        cmprow=(jax.ShapeDtypeStruct((B, S, LANES), F32), row(LANES)),
        gc=(jax.ShapeDtypeStruct((B, S, LANES), F32), row(LANES)),
        od=(jax.ShapeDtypeStruct((B, S, GROUP_WIDTH), F32), row(GROUP_WIDTH)),
        vn=(jax.ShapeDtypeStruct((B, S, GROUP_WIDTH), F32), row(GROUP_WIDTH)),
        qat=(jax.ShapeDtypeStruct((B, N_HEADS, LANES, S), BF16),
             pl.BlockSpec((1, N_HEADS, LANES, tm), lambda b, i: (b, 0, 0, i))),
        karow=(jax.ShapeDtypeStruct((B, S, 4 * LANES), BF16), row(4 * LANES)),
        qct=(jax.ShapeDtypeStruct((B, GROUP_WIDTH, S), BF16), col(GROUP_WIDTH)),
        ksrow=(jax.ShapeDtypeStruct((B, S, 2 * LANES), BF16), row(2 * LANES)),
    )
    names = list(outs)
    res = pl.pallas_call(
        functools.partial(_proj_body, chunk=chunk),
        grid=(B, ns),
        in_specs=in_specs,
        out_specs=[outs[n][1] for n in names],
        out_shape=[outs[n][0] for n in names],
        scratch_shapes=[pltpu.VMEM((8, LANES), F32)],
        compiler_params=_cparams(("arbitrary", "arbitrary")),
        name="proj",
    )(*ins)
    return dict(zip(names, res))


def _online_softmax_step(s, v, carry):
    m, l, acc = carry
    m_new = jnp.maximum(m, jnp.max(s, axis=-1, keepdims=True))
    alpha = jnp.exp(m - m_new)
    p = jnp.exp(s - m_new)
    l = alpha * l + jnp.sum(p, axis=-1, keepdims=True)
    acc = alpha * acc + _nt(p.astype(BF16), v)
    return m_new, l, acc


def _softmax_init(rows, dv):
    return (jnp.full((rows, 1), NEG_BIG, F32), jnp.zeros((rows, 1), F32), jnp.zeros((rows, dv), F32))


_SUM_ROWS = 16


def _col_softmax_init(cols):
    return jnp.full((1, cols), NEG_BIG, F32), jnp.zeros((HEAD_DIM + _SUM_ROWS, cols), F32)


def _col_softmax_step(s, v, state):
    m, acc = state
    m_new = jnp.maximum(m, jnp.max(s, axis=0, keepdims=True))
    p = jnp.exp2(s - m_new).astype(BF16)
    v1 = jnp.concatenate([v, jnp.ones((_SUM_ROWS, v.shape[1]), BF16)], axis=0)
    return m_new, jnp.exp2(m - m_new) * acc + _dot(v1, p)


def _col_softmax_result(state):
    acc = state[1]
    return acc[:HEAD_DIM] / acc[HEAD_DIM:HEAD_DIM + 1]


def _fox_attn_body(q_ref, k_ref, v_ref, o_ref):
    t = q_ref.shape[3]
    i = pl.program_id(1)
    key = lax.broadcasted_iota(jnp.int32, (t, t), 0)
    qry = lax.broadcasted_iota(jnp.int32, (t, t), 1)

    def tile(kt, states, masked):
        start = pl.multiple_of(kt * t, t)
        scores = [_dot(k_ref[0, pl.ds(start, t), h * LANES:(h + 1) * LANES], q_ref[0, h])
                  for h in range(N_HEADS)]
        out = []
        for h in range(N_HEADS):
            s = jnp.where(key <= qry, scores[h], NEG_BIG) if masked else scores[h]
            out.append(_col_softmax_step(s, v_ref[0, h * HEAD_DIM:(h + 1) * HEAD_DIM, pl.ds(start, t)], states[h]))
        return tuple(out)

    init = tuple(_col_softmax_init(t) for _ in range(N_HEADS))
    states = tile(i, lax.fori_loop(0, i, functools.partial(tile, masked=False), init), True)
    o_ref[0] = jnp.concatenate([jnp.transpose(_col_softmax_result(st)) for st in states], axis=-1)


def _fox_attn(qat, karow, vbf, *, t):
    B, S, _ = karow.shape
    return pl.pallas_call(
        _fox_attn_body,
        grid=(B, S // t),
        in_specs=[pl.BlockSpec((1, N_HEADS, LANES, t), lambda b, i: (b, 0, 0, i)),
                  pl.BlockSpec((1, S, 4 * LANES), lambda b, i: (b, 0, 0)),
                  pl.BlockSpec((1, GROUP_WIDTH, S), lambda b, i: (b, 0, 0))],
        out_specs=pl.BlockSpec((1, t, GROUP_WIDTH), lambda b, i: (b, i, 0)),
        out_shape=jax.ShapeDtypeStruct((B, S, GROUP_WIDTH), F32),
        compiler_params=_cparams(("arbitrary", "arbitrary")),
        name="fox_attn",
    )(qat, karow, vbf)


_HALO = 32


def _ln_silu(y, g, b):
    mu = jnp.mean(y, axis=-1, keepdims=True)
    var = jnp.mean(jnp.square(y - mu), axis=-1, keepdims=True)
    y = (y - mu) * lax.rsqrt(var + EPS) * g + b
    return y * jax.nn.sigmoid(y)


def _conv_prompt_body(cur_ref, halo_ref, w_ref, cb_ref, lng_ref, lnb_ref, o_ref, xin_ref):
    tm = cur_ref.shape[1]
    first = pl.program_id(1) == 0
    xin_ref[0:_HALO, :] = jnp.where(first, 0.0, halo_ref[0])
    xin_ref[_HALO:, :] = cur_ref[0]
    off = _HALO - (CONV_WIDTH - 1)
    acc = jnp.zeros((tm, GROUP_WIDTH), F32)
    for k in range(CONV_WIDTH):
        acc = acc + xin_ref[pl.ds(off + k, tm), :] * w_ref[k:k + 1, :]
    o_ref[0] = _ln_silu(acc + cb_ref[...], lng_ref[...], lnb_ref[...])


def _conv_prompt(glu, w, *, tm):
    B, S, C = glu.shape
    r = tm // _HALO
    full = lambda a: pl.BlockSpec(a.shape, lambda b, i: (0,) * a.ndim)
    ins = [glu, glu, w['conv_w'], w['conv_b'], w['conv_ln_g'], w['conv_ln_b']]
    return pl.pallas_call(
        _conv_prompt_body,
        grid=(B, S // tm),
        in_specs=[pl.BlockSpec((1, tm, C), lambda b, i: (b, i, 0)),
                  pl.BlockSpec((1, _HALO, C), lambda b, i: (b, jnp.maximum(i * r - 1, 0), 0))]
                 + [full(a) for a in ins[2:]],
        out_specs=pl.BlockSpec((1, tm, C), lambda b, i: (b, i, 0)),
        out_shape=jax.ShapeDtypeStruct((B, S, C), F32),
        scratch_shapes=[pltpu.VMEM((tm + _HALO, C), F32)],
        compiler_params=_cparams(("arbitrary", "arbitrary")),
        name="conv_prompt",
    )(*ins)


def _compress_body(x_ref, pe_ref, phi_ref, o_ref):
    o_ref[0] = _dot((x_ref[0] + pe_ref[...]).astype(BF16), phi_ref[...])


def _compress(blocks, pe_flat, phi):
    B, n, width = blocks.shape
    return pl.pallas_call(
        _compress_body,
        grid=(B,),
        in_specs=[pl.BlockSpec((1, n, width), lambda b: (b, 0, 0)),
                  pl.BlockSpec(pe_flat.shape, lambda b: (0, 0)),
                  pl.BlockSpec(phi.shape, lambda b: (0, 0))],
        out_specs=pl.BlockSpec((1, n, LANES), lambda b: (b, 0, 0)),
        out_shape=jax.ShapeDtypeStruct((B, n, LANES), F32),
        compiler_params=_cparams(("arbitrary",)),
        name="nsa_compress",
    )(blocks, pe_flat, phi)


def _masked_softmax(s, mask):
    s = jnp.where(mask, s, NEG_BIG)
    m = jnp.max(s, axis=-1, keepdims=True)
    e = jnp.where(mask, jnp.exp(s - m), 0.0)
    return e / jnp.maximum(jnp.sum(e, axis=-1, keepdims=True), 1e-30)


def _stack_heads(q):
    return jnp.concatenate([q[:, h * HEAD_DIM:(h + 1) * HEAD_DIM] for h in range(N_HEADS)], axis=0)


def _select_blocks(imp, cur, n_select):
    j = lax.broadcasted_iota(jnp.int32, imp.shape, 1)
    forced = (j == 0) | (j == cur) | (j == cur - 1)
    v = jnp.where(forced, FORCED_SCORE, imp)
    v = jnp.where(j <= cur, v, -1.0)
    jf = j.astype(F32)
    sel = jnp.zeros(imp.shape, jnp.bool_)
    for _ in range(n_select):
        m = jnp.max(v, axis=-1, keepdims=True)
        idx = jnp.min(jnp.where(v == m, jf, float(imp.shape[1])), axis=-1, keepdims=True)
        pick = jf == idx
        sel = sel | (pick & (m >= 0.0))
        v = jnp.where(pick, -2.0, v)
    return sel


def _nsa_local_body(q_ref, g_ref, kc_ref, vc_ref, kv_ref, o_ref, sbt_ref):
    t = q_ref.shape[1]
    nc = kc_ref.shape[2]
    half = nc // 2
    st = pl.program_id(1) * t
    qs = _stack_heads(q_ref[0])
    qpos = st + lax.broadcasted_iota(jnp.int32, (N_HEADS * t, 1), 0) % t

    c = lax.broadcasted_iota(jnp.int32, (1, nc), 1)
    blk = jnp.where(c < half, 2 * c, 2 * (c - half) + 1)
    p_cmp = _masked_softmax(_dot(qs, kc_ref[0]), (blk + 1) * CMP_BLOCK - 1 <= qpos)
    o_cmp = _dot(p_cmp.astype(BF16), vc_ref[0])
    imp = p_cmp[0:t] + p_cmp[t:2 * t] + p_cmp[2 * t:3 * t] + p_cmp[3 * t:4 * t]
    imp = imp[:, :half] + imp[:, half:]
    sel = _select_blocks(imp, qpos[0:t] // SEL_BLOCK, min(N_SELECT, half))
    sb = jnp.where(sel, 0.0, SEL_NEG)
    if half < LANES:
        sb = jnp.concatenate([sb, jnp.zeros((t, LANES - half), F32)], axis=1)
    sbt_ref[0] = jnp.transpose(sb).astype(BF16)

    sub = min(t, LANES)
    span = WINDOW + sub
    o_win = [[] for _ in range(N_HEADS)]
    for sb in range(t // sub):
        st_s = st + sb * sub
        q_s = jnp.concatenate([qs[h * t + sb * sub:h * t + (sb + 1) * sub] for h in range(N_HEADS)], axis=0)
        qpos_s = st_s + lax.broadcasted_iota(jnp.int32, (N_HEADS * sub, 1), 0) % sub
        start = pl.multiple_of(jnp.maximum(st_s - WINDOW, 0), LANES)
        kwpos = start + lax.broadcasted_iota(jnp.int32, (1, span), 1)
        wmask = (kwpos <= qpos_s) & (qpos_s - kwpos < WINDOW)
        s = jnp.where(wmask, _dot(q_s, kv_ref[0, 128:192, pl.ds(start, span)]), NEG_BIG)
        e = jnp.exp(s - jnp.max(s, axis=-1, keepdims=True))
        o_s = _nt(e.astype(BF16), kv_ref[0, 192:256, pl.ds(start, span)]) / jnp.sum(e, axis=-1, keepdims=True)
        for h in range(N_HEADS):
            o_win[h].append(o_s[h * sub:(h + 1) * sub])

    g = g_ref[0]
    outs = [g[:, 3 * h:3 * h + 1] * o_cmp[h * t:(h + 1) * t]
            + g[:, 3 * h + 2:3 * h + 3] * jnp.concatenate(o_win[h], axis=0) for h in range(N_HEADS)]
    o_ref[0] = jnp.concatenate(outs, axis=-1)


def _nsa_local(qc, gc, kc_t, vc, nsabf, *, t):
    B, S, _ = qc.shape
    nc = kc_t.shape[2]
    tile = lambda width: pl.BlockSpec((1, t, width), lambda b, i: (b, i, 0))
    return pl.pallas_call(
        _nsa_local_body,
        grid=(B, S // t),
        in_specs=[tile(GROUP_WIDTH), tile(LANES),
                  pl.BlockSpec((1, HEAD_DIM, nc), lambda b, i: (b, 0, 0)),
                  pl.BlockSpec((1, nc, HEAD_DIM), lambda b, i: (b, 0, 0)),
                  pl.BlockSpec((1, 256, S), lambda b, i: (b, 0, 0))],
        out_specs=[tile(GROUP_WIDTH), pl.BlockSpec((1, max(nc // 2, LANES), t), lambda b, i: (b, 0, i))],
        out_shape=[jax.ShapeDtypeStruct((B, S, GROUP_WIDTH), F32),
                   jax.ShapeDtypeStruct((B, max(nc // 2, LANES), S), BF16)],
        compiler_params=_cparams(("arbitrary", "arbitrary")),
        name="nsa_local",
    )(qc, gc, kc_t, vc, nsabf)


def _nsa_sel_body(qt_ref, sbt_ref, g_ref, part_ref, ks_ref, kv_ref, o_ref, *, tk):
    t = qt_ref.shape[2]
    st = pl.program_id(1) * t
    n_grp, per = N_HEADS, 1
    pad = jnp.zeros((LANES - HEAD_DIM, per * t), BF16)
    bias = jnp.concatenate([sbt_ref[0]] * per, axis=1)
    q2 = [jnp.concatenate([jnp.concatenate([qt_ref[0, h * HEAD_DIM:(h + 1) * HEAD_DIM, :]
                                            for h in range(c * per, (c + 1) * per)], axis=1), pad, bias], axis=0)
          for c in range(n_grp)]
    qpos = st + lax.broadcasted_iota(jnp.int32, (1, per * t), 1) % t

    def tile(kt, states, masked):
        start = pl.multiple_of(kt * tk, tk)
        keys = ks_ref[0, pl.ds(start, tk), :]
        scores = [_dot(keys, q2[c]) for c in range(n_grp)]
        v = kv_ref[0, HEAD_DIM:2 * HEAD_DIM, pl.ds(start, tk)]
        out = []
        for c in range(n_grp):
            s = scores[c]
            if masked:
                kpos = start + lax.broadcasted_iota(jnp.int32, (tk, 1), 0)
                s = jnp.where(kpos <= qpos, s, NEG_BIG)
            out.append(_col_softmax_step(s, v, states[c]))
        return tuple(out)

    last = (st + t - 1) // tk
    init = tuple(_col_softmax_init(per * t) for _ in range(n_grp))
    states = tile(last, lax.fori_loop(0, last, functools.partial(tile, masked=False), init), True)
    g = g_ref[0]
    outs = []
    for h in range(N_HEADS):
        o_sel = jnp.transpose(_col_softmax_result(states[h // per])[:, (h % per) * t:(h % per + 1) * t])
        outs.append(g[:, 3 * h + 1:3 * h + 2] * o_sel)
    o_ref[0] = part_ref[0] + jnp.concatenate(outs, axis=-1)


def _nsa_sel(qct, selbt, gc, part, ksrow, nsabf, *, t, tk):
    B, _, S = qct.shape
    ns = selbt.shape[1]
    tile = lambda width: pl.BlockSpec((1, t, width), lambda b, i: (b, i, 0))
    return pl.pallas_call(
        functools.partial(_nsa_sel_body, tk=tk),
        grid=(B, S // t),
        in_specs=[pl.BlockSpec((1, GROUP_WIDTH, t), lambda b, i: (b, 0, i)),
                  pl.BlockSpec((1, ns, t), lambda b, i: (b, 0, i)),
                  tile(LANES), tile(GROUP_WIDTH),
                  pl.BlockSpec((1, S, 2 * LANES), lambda b, i: (b, 0, 0)),
                  pl.BlockSpec((1, 256, S), lambda b, i: (b, 0, 0))],
        out_specs=tile(GROUP_WIDTH),
        out_shape=jax.ShapeDtypeStruct((B, S, GROUP_WIDTH), F32),
        compiler_params=_cparams(("arbitrary", "arbitrary")),
        name="nsa_sel",
    )(qct, selbt, gc, part, ksrow, nsabf)


def _nsa_prompt(o, w, *, t, tk, tl):
    B, S, _ = o['qc'].shape
    n = S // CMP_BLOCK
    cmp = _compress(o['cmprow'].reshape(B, n, CMP_BLOCK * LANES), w['pe_flat'], w['phi'])
    kc_t, vc = _even_odd(cmp)
    part, selbias = _nsa_local(o['qc'], o['gc'], kc_t, vc, o['nsabf'], t=tl)
    assert S // SEL_BLOCK <= LANES
    return _nsa_sel(o['qct'], selbias, o['gc'], part, o['ksrow'], o['nsabf'], t=t, tk=tk)


def _merge_ffn_body(x_ref, oa_ref, ob_ref, oc_ref, od_ref, gn_ref, wout_ref, g2_ref, w1_ref, w2_ref,
                    y_ref, hn_ref, acc_ref):
    j = pl.program_id(1)

    @pl.when(j == 0)
    def _():
        h = x_ref[...]
        for i, o_ref in enumerate((oa_ref, ob_ref, oc_ref, od_ref)):
            o = o_ref[...]
            o = o * lax.rsqrt(jnp.mean(o * o, axis=-1, keepdims=True) + EPS) * gn_ref[i:i + 1, :]
            h = h + _dot(o.astype(BF16), wout_ref[i * GROUP_WIDTH:(i + 1) * GROUP_WIDTH, :])
        acc_ref[...] = h
        hn = h * lax.rsqrt(jnp.mean(h * h, axis=-1, keepdims=True) + EPS) * g2_ref[...]
        hn_ref[...] = hn.astype(BF16)

    u = jnp.maximum(_dot(hn_ref[...], w1_ref[...]), 0.0)
    acc_ref[...] += _dot((u * u).astype(BF16), w2_ref[...])

    @pl.when(j == pl.num_programs(1) - 1)
    def _():
        y_ref[...] = acc_ref[...]


def _merge_ffn(x, outs, w, *, tm, tf):
    R, D = x.shape
    F = w['w_ff1'].shape[1]
    rows = lambda width: pl.BlockSpec((tm, width), lambda i, j: (i, 0))
    full = lambda a: pl.BlockSpec(a.shape, lambda i, j: (0,) * a.ndim)
    return pl.pallas_call(
        _merge_ffn_body,
        grid=(R // tm, F // tf),
        in_specs=[rows(D)] + [rows(GROUP_WIDTH)] * 4 + [full(w['gnorm_g']), full(w['w_out']), full(w['norm2_g']),
                  pl.BlockSpec((D, tf), lambda i, j: (0, j)), pl.BlockSpec((tf, D), lambda i, j: (j, 0))],
        out_specs=rows(D),
        out_shape=jax.ShapeDtypeStruct((R, D), F32),
        scratch_shapes=[pltpu.VMEM((tm, D), BF16), pltpu.VMEM((tm, D), F32)],
        compiler_params=_cparams(("arbitrary", "arbitrary")),
        name="merge_ffn",
    )(x, *outs, w['gnorm_g'], w['w_out'], w['norm2_g'], w['w_ff1'], w['w_ff2'])


def _dot3(parts, rhs=None, lhs=None):
    if rhs is not None:
        return sum(_dot(p.astype(BF16), rhs) for p in parts)
    return sum(_dot(lhs, p.astype(BF16)) for p in parts)


def _page_scan_matrices(pg):
    r = np.arange(pg * N_HEADS)
    g, h = r // N_HEADS, r % N_HEADS
    s = np.arange(LANES)
    same = h[:, None] == h[None, :]
    return (jnp.asarray(s[:, None] >= s[None, :], BF16), jnp.asarray(same & (g[None, :] > g[:, None]), BF16),
            jnp.asarray(same, BF16))


def _same_seq_causal(b, t_new, n_cols, rows):
    col = lax.broadcasted_iota(jnp.int32, (rows, n_cols), 1)
    t = lax.broadcasted_iota(jnp.int32, (rows, n_cols), 0) % t_new
    return (col // t_new == b) & (col % t_new <= t)


def _fox_sample_body(pt_ref, q_ref, knew_ref, vnew_ref, tri_ref, later_ref, same_ref, kv_hbm, lf_hbm, o_ref,
                     kvbuf, lfbuf, sems, m_ref, l_ref, acc_ref, carry_ref, *, layer, pg):
    b, j = pl.program_id(0), pl.program_id(1)
    nb, nch = pl.num_programs(0), pl.num_programs(1)
    step = b * nch + j
    slot = step % 2
    t_new = q_ref.shape[0]
    rows = N_HEADS * t_new

    def copies(bb, jj, sl):
        first = (nch - 1 - jj) * pg
        out = []
        for g in range(pg):
            pid = pt_ref[bb, first + g]
            out.append(pltpu.make_async_copy(kv_hbm.at[layer, pid], kvbuf.at[sl, g], sems.at[0, sl]))
            out.append(pltpu.make_async_copy(lf_hbm.at[layer, pid], lfbuf.at[sl, pl.ds(g * N_HEADS, N_HEADS)],
                                             sems.at[1, sl]))
        return out

    @pl.when(step == 0)
    def _():
        for c in copies(b, j, slot):
            c.start()

    @pl.when(step + 1 < nb * nch)
    def _():
        nxt = step + 1
        for c in copies(nxt // nch, nxt % nch, 1 - slot):
            c.start()

    @pl.when(j == 0)
    def _():
        m_ref[...] = jnp.full_like(m_ref, NEG_BIG)
        l_ref[...] = jnp.zeros_like(l_ref)
        acc_ref[...] = jnp.zeros_like(acc_ref)
        carry_ref[...] = jnp.zeros_like(carry_ref)

    qv = q_ref[...]
    lane = lax.broadcasted_iota(jnp.int32, (t_new, LANES), 1)
    q4 = jnp.concatenate([jnp.where(lane < HEAD_DIM, qv[:, 0:128], qv[:, 128:256]),
                          jnp.where(lane < HEAD_DIM, qv[:, 256:384], qv[:, 384:512])], axis=-1)
    grp = lax.broadcasted_iota(jnp.int32, (t_new, GROUP_WIDTH), 1) // HEAD_DIM
    wq = jnp.concatenate([jnp.where(grp == h, q4, 0.0) for h in range(N_HEADS)], axis=0).astype(BF16)

    for c in copies(b, j, slot):
        c.wait()

    lf = lfbuf[slot]
    incl = _dot3(_split3(lf), rhs=tri_ref[...])
    tot = _split3(jnp.broadcast_to(incl[:, 0:1], lf.shape))
    suf = incl - lf + _dot3(tot, lhs=later_ref[...]) + carry_ref[...]
    carry_ref[...] += _dot3(tot, lhs=same_ref[...])

    scores = []
    for g in range(pg):
        bias = jnp.concatenate([jnp.broadcast_to(suf[g * N_HEADS + h:g * N_HEADS + h + 1], (t_new, LANES))
                                for h in range(N_HEADS)], axis=0)
        scores.append(_dot(wq, kvbuf[slot, g, 0:GROUP_WIDTH, :].astype(BF16)) + bias)
    m_old = m_ref[...]
    m = jnp.maximum(m_old, jnp.max(functools.reduce(jnp.maximum, scores), axis=-1, keepdims=True))
    alpha = jnp.exp(m_old - m)
    psum = jnp.zeros((rows, LANES), F32)
    acc = alpha * acc_ref[...]
    for g in range(pg):
        p = jnp.exp(scores[g] - m)
        psum = psum + p
        acc = acc + _nt(p.astype(BF16), kvbuf[slot, g, GROUP_WIDTH:, :].astype(BF16))
    l = alpha * l_ref[...] + jnp.sum(psum, axis=-1, keepdims=True)
    m_ref[...], l_ref[...], acc_ref[...] = m, l, acc

    @pl.when(j == nch - 1)
    def _():
        n_cols = knew_ref.shape[2]
        s = jnp.concatenate([_dot(qv[:, h * LANES:(h + 1) * LANES].astype(BF16), knew_ref[h])
                             for h in range(N_HEADS)], axis=0)
        s = jnp.where(_same_seq_causal(b, t_new, n_cols, rows), s, NEG_BIG)
        m2, l2, acc2 = _online_softmax_step(s, vnew_ref[...], (m, l, acc))
        o = acc2 / l2
        o_ref[...] = jnp.concatenate(
            [o[h * t_new:(h + 1) * t_new, h * HEAD_DIM:(h + 1) * HEAD_DIM] for h in range(N_HEADS)], axis=-1)


def _fox_sample(layer, page_table, q_aug, kaug_new, v_new, cache_kv, cache_lf, *, t_new, pg):
    db, n_pages = page_table.shape
    rows = N_HEADS * t_new
    mats = _page_scan_matrices(pg)
    full = lambda a: pl.BlockSpec(a.shape, lambda b, j, pt: (0,) * a.ndim)
    grid_spec = pltpu.PrefetchScalarGridSpec(
        num_scalar_prefetch=1,
        grid=(db, n_pages // pg),
        in_specs=[pl.BlockSpec((t_new, 4 * LANES), lambda b, j, pt: (b, 0)), full(kaug_new), full(v_new)]
                 + [full(a) for a in mats] + [pl.BlockSpec(memory_space=pl.ANY), pl.BlockSpec(memory_space=pl.ANY)],
        out_specs=pl.BlockSpec((t_new, GROUP_WIDTH), lambda b, j, pt: (b, 0)),
        scratch_shapes=[pltpu.VMEM((2, pg) + cache_kv.shape[2:], F32), pltpu.VMEM((2, pg * N_HEADS, LANES), F32),
                        pltpu.SemaphoreType.DMA((2, 2)),
                        pltpu.VMEM((rows, 1), F32), pltpu.VMEM((rows, 1), F32), pltpu.VMEM((rows, GROUP_WIDTH), F32),
                        pltpu.VMEM((pg * N_HEADS, LANES), F32)])
    return pl.pallas_call(
        functools.partial(_fox_sample_body, layer=layer, pg=pg),
        grid_spec=grid_spec,
        out_shape=jax.ShapeDtypeStruct((db * t_new, GROUP_WIDTH), F32),
        compiler_params=_cparams(("arbitrary", "arbitrary")),
        name="fox_sample",
    )(page_table, q_aug, kaug_new, v_new, *mats, cache_kv, cache_lf)


def _conv_sample_body(state_ref, glu_ref, w_ref, cb_ref, lng_ref, lnb_ref, o_ref, new_ref):
    n_state, t_new = state_ref.shape[1], glu_ref.shape[0]
    x = lambda i: state_ref[0, i] if i < n_state else glu_ref[i - n_state]
    for t in range(t_new):
        acc = x(t) * w_ref[0:1, :]
        for k in range(1, CONV_WIDTH):
            acc = acc + x(t + k) * w_ref[k:k + 1, :]
        o_ref[t] = _ln_silu(acc + cb_ref[...], lng_ref[...], lnb_ref[...])
    for i in range(n_state):
        new_ref[i] = x(i + t_new)


def _conv_sample(layer, state_t, glu_t, w):
    _, n_state, db, c = state_t.shape
    t_new = glu_t.shape[0]
    full = lambda a: pl.BlockSpec(a.shape, lambda i: (0,) * a.ndim)
    ins = [state_t, glu_t, w['conv_w'], w['conv_b'], w['conv_ln_g'], w['conv_ln_b']]
    return pl.pallas_call(
        _conv_sample_body,
        grid=(1,),
        in_specs=[pl.BlockSpec((1, n_state, db, c), lambda i: (layer, 0, 0, 0))] + [full(a) for a in ins[1:]],
        out_specs=[pl.BlockSpec((t_new, db, c), lambda i: (0, 0, 0)), pl.BlockSpec((n_state, db, c), lambda i: (0, 0, 0))],
        out_shape=[jax.ShapeDtypeStruct((t_new, db, c), F32), jax.ShapeDtypeStruct((n_state, db, c), F32)],
        compiler_params=_cparams(("arbitrary",)),
        name="conv_sample",
    )(*ins)


_CMP_PER_PAGE = 4


def _compress_sample_body(pt_ref, *refs, n_group):
    x_refs, (pe_ref, perm_ref, phi_ref, o_ref, x_scr) = refs[:n_group], refs[n_group:]
    pe = pe_ref[...]
    for pair in range(n_group // 2):
        z = jnp.concatenate([x_refs[2 * pair][0, 0] + pe, x_refs[2 * pair + 1][0, 0] + pe], axis=1)
        y = _nt(perm_ref[...], z.astype(BF16))
        for r in range(CMP_BLOCK):
            x_scr[r // 2, pair * 8:(pair + 1) * 8, (r % 2) * LANES:(r % 2 + 1) * LANES] = y[r * 8:(r + 1) * 8]
    acc = jnp.zeros((n_group * _CMP_PER_PAGE, LANES), F32)
    for a in range(CMP_BLOCK // 2):
        acc = acc + _dot(x_scr[a].astype(BF16), phi_ref[a * 2 * LANES:(a + 1) * 2 * LANES, :])
    o_ref[0] = acc


def _pair_permutation(page):
    rows = np.arange(2 * page)
    r, g, n = rows // 8, (rows // _CMP_PER_PAGE) % 2, rows % _CMP_PER_PAGE
    src = g * page + n * CMP_BLOCK + r
    return jnp.asarray(src[:, None] == np.arange(2 * page)[None, :], BF16)


def _compress_sample(layer, page_table, cache_nsa, pe_col, phi, *, n_group):
    db, n_pages = page_table.shape
    page = cache_nsa.shape[3]
    page_spec = lambda g: pl.BlockSpec((1, 1, LANES, page), lambda b, j, pt: (layer, pt[b, j * n_group + g], 0, 0))
    rows = n_group * _CMP_PER_PAGE
    perm = _pair_permutation(page)
    full = lambda a: pl.BlockSpec(a.shape, lambda b, j, pt: (0, 0))
    grid_spec = pltpu.PrefetchScalarGridSpec(
        num_scalar_prefetch=1,
        grid=(db, n_pages // n_group),
        in_specs=[page_spec(g) for g in range(n_group)] + [full(pe_col), full(perm), full(phi)],
        out_specs=pl.BlockSpec((1, rows, LANES), lambda b, j, pt: (b, j, 0)),
        scratch_shapes=[pltpu.VMEM((CMP_BLOCK // 2, rows, 2 * LANES), F32)])
    return pl.pallas_call(
        functools.partial(_compress_sample_body, n_group=n_group),
        grid_spec=grid_spec,
        out_shape=jax.ShapeDtypeStruct((db, n_pages * _CMP_PER_PAGE, LANES), F32),
        compiler_params=_cparams(("arbitrary", "arbitrary")),
        name="nsa_compress_sample",
    )(page_table, *([cache_nsa] * n_group), pe_col, perm, phi)


def _nsa_local_sample_body(q_ref, g_ref, kc_ref, vc_ref, win_ref, new_ref, o_ref, flag_ref, *, past_len):
    b = pl.program_id(0)
    t_new = q_ref.shape[0]
    rows = N_HEADS * t_new
    nc = kc_ref.shape[2]
    half = nc // 2
    qs = _stack_heads(q_ref[...]).astype(BF16)
    tq = lax.broadcasted_iota(jnp.int32, (rows, 1), 0) % t_new
    qpos = past_len + tq

    c = lax.broadcasted_iota(jnp.int32, (1, nc), 1)
    blk = jnp.where(c < half, 2 * c, 2 * (c - half) + 1)
    p_cmp = _masked_softmax(_dot(qs, kc_ref[0]), (blk + 1) * CMP_BLOCK - 1 <= qpos)
    o_cmp = _dot(p_cmp.astype(BF16), vc_ref[0])
    imp = p_cmp[0:t_new]
    for h in range(1, N_HEADS):
        imp = imp + p_cmp[h * t_new:(h + 1) * t_new]
    imp = imp[:, :half] + imp[:, half:]
    cur = jnp.full((t_new, 1), past_len // SEL_BLOCK, jnp.int32)
    sel = _select_blocks(imp, cur, min(N_SELECT, half + 1) - 1)
    j = lax.broadcasted_iota(jnp.int32, sel.shape, 1)
    t = lax.broadcasted_iota(jnp.int32, sel.shape, 0)
    weight = jnp.left_shift(1, 2 * t + j % 2).astype(F32)
    colsum = jnp.sum(jnp.where(sel, weight, 0.0), axis=0, keepdims=True)
    flag_ref[0] = colsum + pltpu.roll(colsum, half - 1, 1)

    w = win_ref.shape[3]
    n_cols = new_ref.shape[1]
    s_old = _dot(qs, win_ref[0, 0, 0:HEAD_DIM, :].astype(BF16))
    s_new = _dot(qs, new_ref[128:192, :])
    i_old = lax.broadcasted_iota(jnp.int32, (1, w), 1)
    mask = jnp.concatenate([jnp.broadcast_to(i_old + (WINDOW - w) > tq, (rows, w)),
                            _same_seq_causal(b, t_new, n_cols, rows)], axis=-1)
    p_win = _masked_softmax(jnp.concatenate([s_old, s_new], axis=-1), mask).astype(BF16)
    o_win = _nt(p_win[:, :w], win_ref[0, 0, HEAD_DIM:, :].astype(BF16)) + _nt(p_win[:, w:], new_ref[192:256, :])

    g = g_ref[...]
    outs = [g[:, 3 * h:3 * h + 1] * o_cmp[h * t_new:(h + 1) * t_new]
            + g[:, 3 * h + 2:3 * h + 3] * o_win[h * t_new:(h + 1) * t_new] for h in range(N_HEADS)]
    o_ref[...] = jnp.concatenate(outs, axis=-1)


def _nsa_local_sample(layer, qc, gc, kc_t, vc, win_t, new_bf, *, t_new, past_len):
    db = kc_t.shape[0]
    nc = kc_t.shape[2]
    w = win_t.shape[3]
    rows = lambda width: pl.BlockSpec((t_new, width), lambda b: (b, 0))
    return pl.pallas_call(
        functools.partial(_nsa_local_sample_body, past_len=past_len),
        grid=(db,),
        in_specs=[rows(GROUP_WIDTH), rows(LANES),
                  pl.BlockSpec((1, HEAD_DIM, nc), lambda b: (b, 0, 0)),
                  pl.BlockSpec((1, nc, HEAD_DIM), lambda b: (b, 0, 0)),
                  pl.BlockSpec((1, 1, 2 * HEAD_DIM, w), lambda b: (layer, b, 0, 0)),
                  pl.BlockSpec(new_bf.shape, lambda b: (0, 0))],
        out_specs=[rows(GROUP_WIDTH), pl.BlockSpec((1, 1, nc // 2), lambda b: (b, 0, 0))],
        out_shape=[jax.ShapeDtypeStruct((db * t_new, GROUP_WIDTH), F32),
                   jax.ShapeDtypeStruct((db, 1, nc // 2), F32)],
        compiler_params=_cparams(("arbitrary",)),
        name="nsa_local_sample",
    )(qc, gc, kc_t, vc, win_t, new_bf)


_SEL_GROUP = 16


def _nsa_sel_sample_body(pt_ref, fl_ref, q_ref, g_ref, part_ref, new_ref, kv_hbm, o_ref,
                         buf, sems, cflag, count, *, layer):
    b = pl.program_id(0)
    nb = pl.num_programs(0)
    n_pages = fl_ref.shape[1]
    slot = b % 2
    t_new = q_ref.shape[0]
    rows = N_HEADS * t_new

    def copy(bb, p, sl, k):
        src = kv_hbm.at[layer, pt_ref[bb, p], pl.ds(2 * HEAD_DIM, 2 * HEAD_DIM)]
        return pltpu.make_async_copy(src, buf.at[sl, k], sems.at[sl])

    def start_all(bb, sl):
        def body(p, k):
            flag = fl_ref[bb, p]

            @pl.when(flag != 0)
            def _():
                copy(bb, p, sl, k).start()
                cflag[sl, k] = flag
            return k + (flag != 0).astype(jnp.int32)
        count[sl] = lax.fori_loop(0, n_pages, body, 0)

    @pl.when(b == 0)
    def _():
        start_all(b, slot)

    @pl.when(b + 1 < nb)
    def _():
        start_all(b + 1, 1 - slot)

    qs = _stack_heads(q_ref[...]).astype(BF16)
    shamt = (2 * (lax.broadcasted_iota(jnp.int32, (rows, LANES), 0) % t_new)
             + lax.broadcasted_iota(jnp.int32, (rows, LANES), 1) // SEL_BLOCK)
    n_fetched = count[slot]

    def wait_body(k, c):
        copy(b, 0, slot, k).wait()
        return c
    lax.fori_loop(0, n_fetched, wait_body, 0)

    def group(gi, carry):
        m_old, l_old, acc = carry
        scores, entries = [], []
        for u in range(_SEL_GROUP):
            k = gi * _SEL_GROUP + u
            entry = jnp.minimum(k, n_fetched - 1)
            flag = jnp.where(k < n_fetched, cflag[slot, entry], 0)
            picked = (jnp.right_shift(jnp.full((rows, LANES), flag, jnp.int32), shamt) & 1) == 1
            scores.append(jnp.where(picked, _dot(qs, buf[slot, entry, 0:HEAD_DIM, :].astype(BF16)), NEG_BIG))
            entries.append(entry)
        m = jnp.maximum(m_old, jnp.max(functools.reduce(jnp.maximum, scores), axis=-1, keepdims=True))
        alpha = jnp.exp(m_old - m)
        psum = jnp.zeros((rows, LANES), F32)
        acc = alpha * acc
        for u in range(_SEL_GROUP):
            p = jnp.exp(scores[u] - m)
            psum = psum + p
            acc = acc + _nt(p.astype(BF16), buf[slot, entries[u], HEAD_DIM:, :].astype(BF16))
        return m, alpha * l_old + jnp.sum(psum, axis=-1, keepdims=True), acc

    carry = lax.fori_loop(0, (n_fetched + _SEL_GROUP - 1) // _SEL_GROUP, group, _softmax_init(rows, HEAD_DIM))
    s = jnp.where(_same_seq_causal(b, t_new, new_ref.shape[1], rows), _dot(qs, new_ref[0:64, :]), NEG_BIG)
    m, l, acc = _online_softmax_step(s, new_ref[64:128, :], carry)
    o_sel = acc / l
    g = g_ref[...]
    outs = [g[:, 3 * h + 1:3 * h + 2] * o_sel[h * t_new:(h + 1) * t_new] for h in range(N_HEADS)]
    o_ref[...] = part_ref[...] + jnp.concatenate(outs, axis=-1)


def _nsa_sel_sample(layer, page_table, flags, qc, gc, part, new_bf, cache_nsa, *, t_new):
    db, n_pages = page_table.shape
    rows = N_HEADS * t_new
    tile = lambda width: pl.BlockSpec((t_new, width), lambda b, pt, fl: (b, 0))
    grid_spec = pltpu.PrefetchScalarGridSpec(
        num_scalar_prefetch=2,
        grid=(db,),
        in_specs=[tile(GROUP_WIDTH), tile(LANES), tile(GROUP_WIDTH),
                  pl.BlockSpec(new_bf.shape, lambda b, pt, fl: (0, 0)),
                  pl.BlockSpec(memory_space=pl.ANY)],
        out_specs=tile(GROUP_WIDTH),
        scratch_shapes=[pltpu.VMEM((2, n_pages, 2 * HEAD_DIM, LANES), F32), pltpu.SemaphoreType.DMA((2,)),
                        pltpu.SMEM((2, n_pages), jnp.int32), pltpu.SMEM((2,), jnp.int32)])
    return pl.pallas_call(
        functools.partial(_nsa_sel_sample_body, layer=layer),
        grid_spec=grid_spec,
        out_shape=jax.ShapeDtypeStruct((db * t_new, GROUP_WIDTH), F32),
        compiler_params=_cparams(("arbitrary",)),
        name="nsa_sel_sample",
    )(page_table, flags, qc, gc, part, new_bf, cache_nsa)


_SPLITS = np.cumsum([0, 256, 256, 256, 4, 512, 256, 384, 12, 256, 256])


def _prep_layer(l, P):
    w_in = P['w_in'][l]
    sec = [w_in[:, _SPLITS[i]:_SPLITS[i + 1]] for i in range(10)]
    qa, ka, va, fa, glu, qc, kvc, gc, ud, vd = sec
    w_row = jnp.concatenate([qa, glu, qc, ud, vd, jnp.pad(gc, ((0, 0), (0, LANES - 12)))], axis=1).astype(BF16)
    w_col = jnp.concatenate([ka, va, kvc, jnp.pad(fa, ((0, 0), (0, 12)))], axis=1).T.astype(BF16)
    row = lambda v: v.reshape(1, -1)
    phi_k = P['nsa_phi_k'][l].reshape(CMP_BLOCK, 1, HEAD_DIM, HEAD_DIM)
    phi_v = P['nsa_phi_v'][l].reshape(CMP_BLOCK, 1, HEAD_DIM, HEAD_DIM)
    zero = jnp.zeros_like(phi_k)
    phi = jnp.concatenate([jnp.concatenate([phi_k, zero], axis=-1), jnp.concatenate([zero, phi_v], axis=-1)], axis=1)
    return dict(
        pe_flat=jnp.transpose(P['nsa_pe'][l], (1, 0, 2)).reshape(1, CMP_BLOCK * LANES),
        pe_col=jnp.tile(jnp.transpose(P['nsa_pe'][l], (0, 2, 1)).reshape(LANES, CMP_BLOCK), (1, _CMP_PER_PAGE)),
        phi=phi.reshape(CMP_BLOCK * LANES, LANES).astype(BF16),
        norm1_g=row(P['norm1_g'][l]), w_row=w_row, w_col=w_col,
        gqa=row(jnp.tile(P['fox_qn_g'][l], N_HEADS)), gka=P['fox_kn_g'][l].reshape(HEAD_DIM, 1),
        bf=jnp.pad(P['fox_bf'][l], (0, 4)).reshape(8, 1),
        gqc=row(jnp.tile(P['nsa_qn_g'][l], N_HEADS)), gkc=P['nsa_kn_g'][l].T,
        gmlp_ln_g=row(P['gmlp_ln_g'][l]), gmlp_ln_b=row(P['gmlp_ln_b'][l]),
        gmlp_ws=P['gmlp_ws'][l], gmlp_bs=P['gmlp_bs'][l],
        gnorm_g=P['gnorm_g'][l], w_out=P['w_out'][l].astype(BF16), norm2_g=row(P['norm2_g'][l]),
        w_ff1=P['w_ff1'][l].astype(BF16), w_ff2=P['w_ff2'][l].astype(BF16),
        conv_w=P['conv_w'][l], conv_b=row(P['conv_b'][l]),
        conv_ln_g=row(P['conv_ln_g'][l]), conv_ln_b=row(P['conv_ln_b'][l]),
    )


def _rope_tables(pos):
    inv = ROPE_THETA ** (-jnp.arange(HALF, dtype=F32) / HALF)
    ang = pos.astype(F32)[:, None] * inv
    cos, sin = jnp.cos(ang), jnp.sin(ang)
    return dict(cos_r=jnp.tile(cos, (1, 4)), sin_r=jnp.tile(jnp.concatenate([-sin, sin], axis=1), (1, 2)),
                cos_t=cos.T, sin_t=sin.T)


def _const_tables():
    g = np.arange(GROUP_WIDTH) // HEAD_DIM
    return dict(gsum=jnp.asarray(g[:, None] == g[None, :], BF16))


def _gmlp_tables(w, seq_len, n_seq):
    t = min(seq_len, CHUNK)
    wm = (w['gmlp_ws'] * jnp.tril(jnp.ones((CHUNK, CHUNK), F32)))[:, :t, :t]
    bs = w['gmlp_bs'][:, :t]
    if seq_len < CHUNK:
        eye = jnp.eye(n_seq, dtype=F32)
        wm = jnp.einsum('ab,gts->gatbs', eye, wm).reshape(N_HEADS, n_seq * t, n_seq * t)
        bs = jnp.tile(bs, (1, n_seq))
    c = wm.shape[1]
    return dict(wm=wm.reshape(N_HEADS * c, c).astype(BF16), bs_tab=jnp.repeat(bs.T, HEAD_DIM, axis=1))


def _layer_prompt(x, w, consts, *, tm, ta, tk, tf):
    B, S, D = x.shape
    tabs = dict(consts, **_gmlp_tables(w, S, B))
    o = _proj(x, w, tabs, tm=tm, chunk=CHUNK)
    o_a = _fox_attn(o['qat'], o['karow'], o['vbf'], t=ta)
    o_b = _conv_prompt(o['glu'], w, tm=tm)
    o_c = _nsa_prompt(o, w, t=tk, tk=tk, tl=min(4 * LANES, S))
    flat = lambda a: a.reshape(B * S, a.shape[-1])
    y = _merge_ffn(flat(x), [flat(o_a), flat(o_b), flat(o_c), flat(o['od'])], w, tm=min(2 * tm, B * S), tf=tf)
    wp = min(WINDOW, S)
    states = dict(fox_kv=o['foxkv'], fox_logf=o['logf'], nsa_kv=o['nsakv'],
                  nsa_win=o['nsawin'][:, :, S - wp:], conv=o['glu'][:, S - (CONV_WIDTH - 1):])
    return y.reshape(B, S, D), states


def _even_odd(cmp):
    n = cmp.shape[1]
    order = np.concatenate([np.arange(0, n, 2), np.arange(1, n, 2)])
    cmp = cmp[:, order].astype(BF16)
    return jnp.swapaxes(cmp[:, :, :HEAD_DIM], 1, 2), cmp[:, :, HEAD_DIM:]


def _prep_caches(cache_fox_kv, cache_fox_logf, cache_nsa_kv, state_nsa_win, state_conv):
    L, pool, page = cache_fox_kv.shape[:3]
    db = state_nsa_win.shape[1]
    return dict(
        fox_kv=jnp.transpose(cache_fox_kv, (0, 1, 3, 4, 5, 2)).reshape(L, pool, 2 * GROUP_WIDTH, page),
        fox_lf=jnp.transpose(cache_fox_logf, (0, 1, 3, 2)),
        nsa_kv=jnp.transpose(cache_nsa_kv, (0, 1, 3, 4, 2)).reshape(L, pool, 4 * HEAD_DIM, page),
        win=jnp.transpose(state_nsa_win, (0, 1, 3, 4, 2)).reshape(L, db, 2 * HEAD_DIM, -1),
        conv=jnp.transpose(state_conv, (0, 2, 1, 3)),
    )


def _layer_sample(l, xs, w, caches, consts, page_table, *, past_len, pg, tf):
    db, t_new, D = xs.shape
    R = db * t_new
    tabs = dict(consts, **_gmlp_tables(w, t_new, db))
    o = {k: v[0] for k, v in _proj(xs.reshape(1, R, D), w, tabs, tm=R, chunk=R).items()}
    o_a = _fox_sample(l, page_table, o['qa'].astype(F32), o['kaug'], o['vbf'], caches['fox_kv'], caches['fox_lf'],
                      t_new=t_new, pg=pg)
    glu_t = jnp.swapaxes(o['glu'].reshape(db, t_new, GROUP_WIDTH), 0, 1)
    o_b_t, conv_new = _conv_sample(l, caches['conv'], glu_t, w)
    o_b = jnp.swapaxes(o_b_t, 0, 1).reshape(R, GROUP_WIDTH)
    cmp = _compress_sample(l, page_table, caches['nsa_kv'], w['pe_col'], w['phi'], n_group=pg)
    kc_t, vc = _even_odd(cmp)
    qc = o['qc'].astype(F32)
    part, flags = _nsa_local_sample(l, qc, o['gc'], kc_t, vc, caches['win'], o['nsabf'],
                                    t_new=t_new, past_len=past_len)
    flags = flags[:, 0, ::2].astype(jnp.int32)
    o_c = _nsa_sel_sample(l, page_table, flags, qc, o['gc'], part, o['nsabf'], caches['nsa_kv'], t_new=t_new)
    y = _merge_ffn(xs.reshape(R, D), [o_a, o_b, o_c, o['od']], w, tm=R, tf=tf)
    rows = lambda a, *shape: a.T.reshape(db, t_new, *shape)
    win_new = jnp.swapaxes(o['nsawin'].reshape(2 * HEAD_DIM, db, t_new), 0, 1)
    win = jnp.concatenate([caches['win'][l][:, :, t_new:], win_new], axis=-1)
    states = dict(fox_kv=rows(o['foxkv'], 2, N_HEADS, HEAD_DIM), fox_logf=rows(o['logf'], N_HEADS),
                  nsa_kv=rows(o['nsakv'], 4, HEAD_DIM),
                  nsa_win=jnp.transpose(win.reshape(db, 2, HEAD_DIM, -1), (0, 3, 1, 2)),
                  conv=jnp.swapaxes(conv_new, 0, 1), gmlp_v=o['vn'].reshape(db, t_new, GROUP_WIDTH))
    return y.reshape(db, t_new, D), states


def _scan_matrix(n, seg):
    i = np.arange(n)
    return jnp.asarray((i[:, None] <= i[None, :]) & (i[:, None] // seg == i[None, :] // seg), BF16)


_PARAM_NAMES = ('norm1_g', 'w_in', 'fox_bf', 'fox_qn_g', 'fox_kn_g', 'conv_w', 'conv_b', 'conv_ln_g', 'conv_ln_b',
                'nsa_qn_g', 'nsa_kn_g', 'nsa_pe', 'nsa_phi_k', 'nsa_phi_v', 'gmlp_ln_g', 'gmlp_ln_b', 'gmlp_ws',
                'gmlp_bs', 'gnorm_g', 'w_out', 'norm2_g', 'w_ff1', 'w_ff2')


def kernel(x_prompt, x_sample, cache_fox_kv, cache_fox_logf, cache_nsa_kv, state_nsa_win, state_conv, page_table,
           *params):
    P = dict(zip(_PARAM_NAMES, params))
    depth = P['w_in'].shape[0]
    B, S, D = x_prompt.shape
    DB, T, _ = x_sample.shape
    n_pages, page = page_table.shape[1], cache_fox_kv.shape[2]
    past_len = n_pages * page
    assert past_len % SEL_BLOCK == 0 and T <= SEL_BLOCK and past_len >= WINDOW and page == LANES
    tm = 512
    consts_p = dict(_rope_tables(jnp.arange(S)), **_const_tables(), utri=_scan_matrix(tm, tm))
    consts_s = dict(_rope_tables(jnp.tile(past_len + jnp.arange(T), DB)), **_const_tables(),
                    utri=_scan_matrix(DB * T, T))
    caches = _prep_caches(cache_fox_kv, cache_fox_logf, cache_nsa_kv, state_nsa_win, state_conv)
    xp, xs = x_prompt, x_sample
    st_p, st_s = [], []
    for l in range(depth):
        w = _prep_layer(l, P)
        xp, sp = _layer_prompt(xp, w, consts_p, tm=tm, ta=512, tk=512, tf=1024)
        xs, ss = _layer_sample(l, xs, w, caches, consts_s, page_table, past_len=past_len, pg=32, tf=1024)
        st_p.append(sp)
        st_s.append(ss)
    stack_p = lambda k: jnp.stack([s[k] for s in st_p])
    stack_s = lambda k: jnp.stack([s[k] for s in st_s])
    fox_kv_p = jnp.transpose(stack_p('fox_kv').reshape(depth, B, 2, N_HEADS, HEAD_DIM, S), (0, 1, 5, 2, 3, 4))
    fox_logf_p = jnp.transpose(stack_p('fox_logf'), (0, 1, 3, 2))
    nsa_kv_p = jnp.transpose(stack_p('nsa_kv').reshape(depth, B, 4, HEAD_DIM, S), (0, 1, 4, 2, 3))
    nsa_win_p = jnp.transpose(stack_p('nsa_win').reshape(depth, B, 2, HEAD_DIM, -1), (0, 1, 4, 2, 3))
    return (xp, xs, fox_kv_p, stack_s('fox_kv'), fox_logf_p, stack_s('fox_logf'),
            nsa_kv_p, stack_s('nsa_kv'), nsa_win_p, stack_s('nsa_win'),
            stack_p('conv'), stack_s('conv'), stack_s('gmlp_v'))
```

```python
import functools

import jax
import jax.numpy as jnp
import numpy as np
from jax import lax
from jax.experimental import pallas as pl
from jax.experimental.pallas import tpu as pltpu

F32 = jnp.float32
BF16 = jnp.bfloat16

HEAD_DIM = 64
HALF = HEAD_DIM // 2
GROUP_WIDTH = 256
N_HEADS = GROUP_WIDTH // HEAD_DIM
CONV_WIDTH = 31
CMP_BLOCK = 32
SEL_BLOCK = 64
N_SELECT = 16
WINDOW = 512
CHUNK = 128
ROPE_THETA = 10000.0
EPS = 1e-6
FORCED_SCORE = 1e4
Q_SCALE = HEAD_DIM ** -0.5
LOG2E = 1.4426950408889634
NEG_BIG = -1e30
SEL_NEG = -32768.0
LANES = 128
VMEM_LIMIT = 56 * 1024 * 1024


def _cparams(sem):
    return pltpu.CompilerParams(dimension_semantics=sem, vmem_limit_bytes=VMEM_LIMIT)


def _nt(a, b):
    return lax.dot_general(a, b, (((1,), (1,)), ((), ())), preferred_element_type=F32)


def _dot(a, b):
    return jnp.dot(a, b, preferred_element_type=F32)


def _split3(x):
    h = x.astype(BF16).astype(F32)
    r = x - h
    m = r.astype(BF16).astype(F32)
    l = (r - m).astype(BF16).astype(F32)
    return h, m, l


def _log_sigmoid(x):
    return jnp.minimum(x, 0.0) - jnp.log1p(jnp.exp(-jnp.abs(x)))


def _group_mean_sq(x, gsum):
    x2 = x * x
    hi = x2.astype(BF16)
    lo = (x2 - hi.astype(F32)).astype(BF16)
    return (_dot(hi, gsum) + _dot(lo, gsum)) * (1.0 / HEAD_DIM)


def _rope_rows(x, cos, sin_signed):
    lane = lax.broadcasted_iota(jnp.int32, x.shape, 1)
    first_half = (lane % HEAD_DIM) < HALF
    swapped = jnp.where(first_half, pltpu.roll(x, LANES - HALF, 1), pltpu.roll(x, HALF, 1))
    return x * cos + swapped * sin_signed


_R_QA, _R_GLU, _R_QC, _R_UD, _R_VD, _R_SMALL, _R_END = 0, 256, 768, 1024, 1280, 1536, 1664
_C_KA, _C_VA, _C_KVC, _C_FA, _C_END = 0, 256, 512, 896, 912


def _proj_body(x_ref, g1_ref, wrow_ref, wcol_ref, cosr_ref, sinr_ref, cost_ref, sint_ref,
               gsum_ref, gqa_ref, gka_ref, bf_ref, gqc_ref, gkc_ref, lng_ref, lnb_ref,
               wm_ref, bstab_ref, utri_ref,
               qa_ref, foxkv_ref, kaug_ref, vbf_ref, logf_ref, glu_ref, qc_ref, nsakv_ref,
               nsawin_ref, nsabf_ref, cmprow_ref, gc_ref, od_ref, vn_ref, qat_ref, karow_ref, qct_ref, ksrow_ref,
               carry_ref, *, chunk):
    tm = x_ref.shape[1]

    @pl.when(pl.program_id(1) == 0)
    def _():
        carry_ref[...] = jnp.zeros_like(carry_ref)

    x = x_ref[0]
    ms = jnp.mean(x * x, axis=-1, keepdims=True)
    xn = ((x * lax.rsqrt(ms + EPS)) * g1_ref[...]).astype(BF16)
    zr = _dot(xn, wrow_ref[...])
    zc = _nt(wcol_ref[...], xn)
    gsum = gsum_ref[...]

    qa = zr[:, _R_QA:_R_QA + 256]
    qa = qa * lax.rsqrt(_group_mean_sq(qa, gsum) + EPS) * gqa_ref[...] * Q_SCALE
    lane = lax.broadcasted_iota(jnp.int32, (tm, LANES), 1)
    for h in range(N_HEADS):
        src = qa[:, (h // 2) * LANES:(h // 2 + 1) * LANES]
        if h % 2 == 0:
            aug = jnp.where(lane < HEAD_DIM, src, jnp.where(lane < HEAD_DIM + 3, 1.0, 0.0))
        else:
            aug = jnp.where(lane >= HEAD_DIM, src, jnp.where(lane < 3, 1.0, 0.0))
        qa_ref[0, :, h * LANES:(h + 1) * LANES] = aug.astype(BF16)
        is_q = (lane < HEAD_DIM) if h % 2 == 0 else (lane >= HEAD_DIM)
        qat_ref[0, h] = jnp.transpose(aug * jnp.where(is_q, LOG2E, 1.0)).astype(BF16)

    logf = _log_sigmoid(zc[_C_FA:_C_FA + 8] + bf_ref[...])
    logf_ref[0] = logf[0:N_HEADS]
    parts = _split3(logf)
    l3 = jnp.concatenate(parts, axis=0).astype(BF16)
    cs = _dot(l3, utri_ref[...])
    fcum = cs[0:8] + cs[8:16] + cs[16:24] + carry_ref[:, 0:1]
    carry_ref[...] = jnp.broadcast_to(fcum[:, tm - 1:tm], carry_ref.shape)
    nfh, nfm, nfl = _split3(-fcum)
    nf2 = _split3(-fcum * LOG2E)
    row8 = lax.broadcasted_iota(jnp.int32, (8, tm), 0)
    zeros56 = jnp.zeros((HEAD_DIM - 8, tm), F32)
    gka = gka_ref[:, 0:1]
    for h in range(N_HEADS):
        k = zc[_C_KA + h * HEAD_DIM:_C_KA + (h + 1) * HEAD_DIM]
        k = k * lax.rsqrt(jnp.mean(k * k, axis=0, keepdims=True) + EPS) * gka
        v = zc[_C_VA + h * HEAD_DIM:_C_VA + (h + 1) * HEAD_DIM]
        foxkv_ref[0, h * HEAD_DIM:(h + 1) * HEAD_DIM, :] = k
        foxkv_ref[0, GROUP_WIDTH + h * HEAD_DIM:GROUP_WIDTH + (h + 1) * HEAD_DIM, :] = v
        vbf_ref[0, h * HEAD_DIM:(h + 1) * HEAD_DIM, :] = v.astype(BF16)
        extra8 = jnp.where(row8 == 0, nfh[h:h + 1],
                           jnp.where(row8 == 1, nfm[h:h + 1],
                                     jnp.where(row8 == 2, nfl[h:h + 1], 0.0)))
        extra = jnp.concatenate([extra8, zeros56], axis=0)
        pieces = [k, extra] if h % 2 == 0 else [extra, k]
        kaug_ref[0, h] = jnp.concatenate(pieces, axis=0).astype(BF16)
        extra2 = jnp.where(row8 == 0, nf2[0][h:h + 1],
                           jnp.where(row8 == 1, nf2[1][h:h + 1],
                                     jnp.where(row8 == 2, nf2[2][h:h + 1], 0.0)))
        extra2 = jnp.concatenate([extra2, zeros56], axis=0)
        kaug2 = jnp.concatenate([k, extra2] if h % 2 == 0 else [extra2, k], axis=0)
        karow_ref[0, :, h * LANES:(h + 1) * LANES] = jnp.transpose(kaug2).astype(BF16)

    glu_in = zr[:, _R_GLU:_R_GLU + 512]
    glu_ref[0] = glu_in[:, :256] * jax.nn.sigmoid(glu_in[:, 256:])

    qc = zr[:, _R_QC:_R_QC + 256]
    qc = qc * lax.rsqrt(_group_mean_sq(qc, gsum) + EPS) * gqc_ref[...]
    cosr, sinr = cosr_ref[...], sinr_ref[...]
    for p in range(2):
        qh = _rope_rows(qc[:, p * LANES:(p + 1) * LANES], cosr, sinr) * Q_SCALE
        qc_ref[0, :, p * LANES:(p + 1) * LANES] = qh.astype(BF16)
        qct_ref[0, p * LANES:(p + 1) * LANES, :] = jnp.transpose(qh * LOG2E).astype(BF16)
    gc_ref[0] = jax.nn.sigmoid(zr[:, _R_SMALL:_R_SMALL + LANES])
    cost, sint = cost_ref[...], sint_ref[...]
    keys = []
    for b in range(3):
        kb = zc[_C_KVC + 2 * b * HEAD_DIM:_C_KVC + (2 * b + 1) * HEAD_DIM]
        kb = kb * lax.rsqrt(jnp.mean(kb * kb, axis=0, keepdims=True) + EPS) * gkc_ref[:, b:b + 1]
        x1, x2 = kb[:HALF], kb[HALF:]
        keys.append(jnp.concatenate([x1 * cost - x2 * sint, x2 * cost + x1 * sint], axis=0))
    vals = [zc[_C_KVC + (2 * b + 1) * HEAD_DIM:_C_KVC + (2 * b + 2) * HEAD_DIM] for b in range(3)]
    nsakv_ref[0, 0:64, :] = keys[0]
    nsakv_ref[0, 64:128, :] = vals[0]
    nsakv_ref[0, 128:192, :] = keys[1]
    nsakv_ref[0, 192:256, :] = vals[1]
    nsawin_ref[0, 0:64, :] = keys[2]
    nsawin_ref[0, 64:128, :] = vals[2]
    nsabf_ref[0, 0:64, :] = keys[1].astype(BF16)
    nsabf_ref[0, 64:128, :] = vals[1].astype(BF16)
    nsabf_ref[0, 128:192, :] = keys[2].astype(BF16)
    nsabf_ref[0, 192:256, :] = vals[2].astype(BF16)
    cmprow_ref[0] = jnp.transpose(jnp.concatenate([keys[0], vals[0]], axis=0))
    ksrow_ref[0, :, 0:LANES] = jnp.transpose(jnp.concatenate([keys[1], jnp.zeros_like(keys[1])], axis=0)).astype(BF16)
    blk = (pl.program_id(1) * tm + lax.broadcasted_iota(jnp.int32, (tm, LANES), 0)) // SEL_BLOCK
    ksrow_ref[0, :, LANES:] = jnp.where(blk == lane, 1.0, 0.0).astype(BF16)

    ud = zr[:, _R_UD:_R_UD + 256]
    vd = zr[:, _R_VD:_R_VD + 256]
    mu = jnp.mean(vd, axis=-1, keepdims=True)
    var = jnp.mean(jnp.square(vd - mu), axis=-1, keepdims=True)
    vn = (vd - mu) * lax.rsqrt(var + EPS) * lng_ref[...] + lnb_ref[...]
    vn_ref[0] = vn
    grp = lax.broadcasted_iota(jnp.int32, (chunk, GROUP_WIDTH), 1) // HEAD_DIM
    wm = wm_ref[...]
    for c in range(tm // chunk):
        r = _dot(wm, vn[c * chunk:(c + 1) * chunk].astype(BF16))
        mixed = bstab_ref[...]
        for g in range(N_HEADS):
            mixed = mixed + jnp.where(grp == g, r[g * chunk:(g + 1) * chunk], 0.0)
        od_ref[0, c * chunk:(c + 1) * chunk, :] = ud[c * chunk:(c + 1) * chunk] * mixed


def _proj(x, w, tabs, *, tm, chunk):
    B, S, D = x.shape
    ns = S // tm
    row = lambda width: pl.BlockSpec((1, tm, width), lambda b, i: (b, i, 0))
    col = lambda height: pl.BlockSpec((1, height, tm), lambda b, i: (b, 0, i))
    full = lambda a: pl.BlockSpec(a.shape, lambda b, i: (0,) * a.ndim)
    ins = [x, w['norm1_g'], w['w_row'], w['w_col'], tabs['cos_r'], tabs['sin_r'], tabs['cos_t'], tabs['sin_t'],
           tabs['gsum'], w['gqa'], w['gka'], w['bf'], w['gqc'], w['gkc'], w['gmlp_ln_g'], w['gmlp_ln_b'],
           tabs['wm'], tabs['bs_tab'], tabs['utri']]
    in_specs = [row(D), full(ins[1]), full(ins[2]), full(ins[3]),
                pl.BlockSpec((tm, LANES), lambda b, i: (i, 0)), pl.BlockSpec((tm, LANES), lambda b, i: (i, 0)),
                pl.BlockSpec((HALF, tm), lambda b, i: (0, i)), pl.BlockSpec((HALF, tm), lambda b, i: (0, i))]
    in_specs += [full(a) for a in ins[8:]]
    outs = dict(
        qa=(jax.ShapeDtypeStruct((B, S, 4 * LANES), BF16), row(4 * LANES)),
        foxkv=(jax.ShapeDtypeStruct((B, 2 * GROUP_WIDTH, S), F32), col(2 * GROUP_WIDTH)),
        kaug=(jax.ShapeDtypeStruct((B, N_HEADS, LANES, S), BF16),
              pl.BlockSpec((1, N_HEADS, LANES, tm), lambda b, i: (b, 0, 0, i))),
        vbf=(jax.ShapeDtypeStruct((B, GROUP_WIDTH, S), BF16), col(GROUP_WIDTH)),
        logf=(jax.ShapeDtypeStruct((B, N_HEADS, S), F32), col(N_HEADS)),
        glu=(jax.ShapeDtypeStruct((B, S, GROUP_WIDTH), F32), row(GROUP_WIDTH)),
        qc=(jax.ShapeDtypeStruct((B, S, GROUP_WIDTH), BF16), row(GROUP_WIDTH)),
        nsakv=(jax.ShapeDtypeStruct((B, 256, S), F32), col(256)),
        nsawin=(jax.ShapeDtypeStruct((B, 128, S), F32), col(128)),
        nsabf=(jax.ShapeDtypeStruct((B, 256, S), BF16), col(256)),
        cmprow=(jax.ShapeDtypeStruct((B, S, LANES), F32), row(LANES)),
        gc=(jax.ShapeDtypeStruct((B, S, LANES), F32), row(LANES)),
        od=(jax.ShapeDtypeStruct((B, S, GROUP_WIDTH), F32), row(GROUP_WIDTH)),
        vn=(jax.ShapeDtypeStruct((B, S, GROUP_WIDTH), F32), row(GROUP_WIDTH)),
        qat=(jax.ShapeDtypeStruct((B, N_HEADS, LANES, S), BF16),
             pl.BlockSpec((1, N_HEADS, LANES, tm), lambda b, i: (b, 0, 0, i))),
        karow=(jax.ShapeDtypeStruct((B, S, 4 * LANES), BF16), row(4 * LANES)),
        qct=(jax.ShapeDtypeStruct((B, GROUP_WIDTH, S), BF16), col(GROUP_WIDTH)),
        ksrow=(jax.ShapeDtypeStruct((B, S, 2 * LANES), BF16), row(2 * LANES)),
    )
    names = list(outs)
    res = pl.pallas_call(
        functools.partial(_proj_body, chunk=chunk),
        grid=(B, ns),
        in_specs=in_specs,
        out_specs=[outs[n][1] for n in names],
        out_shape=[outs[n][0] for n in names],
        scratch_shapes=[pltpu.VMEM((8, LANES), F32)],
        compiler_params=_cparams(("arbitrary", "arbitrary")),
        name="proj",
    )(*ins)
    return dict(zip(names, res))


def _online_softmax_step(s, v, carry):
    m, l, acc = carry
    m_new = jnp.maximum(m, jnp.max(s, axis=-1, keepdims=True))
    alpha = jnp.exp(m - m_new)
    p = jnp.exp(s - m_new)
    l = alpha * l + jnp.sum(p, axis=-1, keepdims=True)
    acc = alpha * acc + _nt(p.astype(BF16), v)
    return m_new, l, acc


def _softmax_init(rows, dv):
    return (jnp.full((rows, 1), NEG_BIG, F32), jnp.zeros((rows, 1), F32), jnp.zeros((rows, dv), F32))


_SUM_ROWS = 16


def _col_softmax_init(cols):
    return jnp.full((1, cols), NEG_BIG, F32), jnp.zeros((HEAD_DIM + _SUM_ROWS, cols), F32)


def _col_softmax_step(s, v, state):
    m, acc = state
    m_new = jnp.maximum(m, jnp.max(s, axis=0, keepdims=True))
    p = jnp.exp2(s - m_new).astype(BF16)
    v1 = jnp.concatenate([v, jnp.ones((_SUM_ROWS, v.shape[1]), BF16)], axis=0)
    return m_new, jnp.exp2(m - m_new) * acc + _dot(v1, p)


def _col_softmax_result(state):
    acc = state[1]
    return acc[:HEAD_DIM] / acc[HEAD_DIM:HEAD_DIM + 1]


def _fox_attn_body(q_ref, k_ref, v_ref, o_ref):
    t = q_ref.shape[3]
    i = pl.program_id(1)
    key = lax.broadcasted_iota(jnp.int32, (t, t), 0)
    qry = lax.broadcasted_iota(jnp.int32, (t, t), 1)

    def tile(kt, states, masked):
        start = pl.multiple_of(kt * t, t)
        scores = [_dot(k_ref[0, pl.ds(start, t), h * LANES:(h + 1) * LANES], q_ref[0, h])
                  for h in range(N_HEADS)]
        out = []
        for h in range(N_HEADS):
            s = jnp.where(key <= qry, scores[h], NEG_BIG) if masked else scores[h]
            out.append(_col_softmax_step(s, v_ref[0, h * HEAD_DIM:(h + 1) * HEAD_DIM, pl.ds(start, t)], states[h]))
        return tuple(out)

    init = tuple(_col_softmax_init(t) for _ in range(N_HEADS))
    states = tile(i, lax.fori_loop(0, i, functools.partial(tile, masked=False), init), True)
    o_ref[0] = jnp.concatenate([jnp.transpose(_col_softmax_result(st)) for st in states], axis=-1)


def _fox_attn(qat, karow, vbf, *, t):
    B, S, _ = karow.shape
    return pl.pallas_call(
        _fox_attn_body,
        grid=(B, S // t),
        in_specs=[pl.BlockSpec((1, N_HEADS, LANES, t), lambda b, i: (b, 0, 0, i)),
                  pl.BlockSpec((1, S, 4 * LANES), lambda b, i: (b, 0, 0)),
                  pl.BlockSpec((1, GROUP_WIDTH, S), lambda b, i: (b, 0, 0))],
        out_specs=pl.BlockSpec((1, t, GROUP_WIDTH), lambda b, i: (b, i, 0)),
        out_shape=jax.ShapeDtypeStruct((B, S, GROUP_WIDTH), F32),
        compiler_params=_cparams(("arbitrary", "arbitrary")),
        name="fox_attn",
    )(qat, karow, vbf)


_HALO = 32


def _ln_silu(y, g, b):
    mu = jnp.mean(y, axis=-1, keepdims=True)
    var = jnp.mean(jnp.square(y - mu), axis=-1, keepdims=True)
    y = (y - mu) * lax.rsqrt(var + EPS) * g + b
    return y * jax.nn.sigmoid(y)


def _conv_prompt_body(cur_ref, halo_ref, w_ref, cb_ref, lng_ref, lnb_ref, o_ref, xin_ref):
    tm = cur_ref.shape[1]
    first = pl.program_id(1) == 0
    xin_ref[0:_HALO, :] = jnp.where(first, 0.0, halo_ref[0])
    xin_ref[_HALO:, :] = cur_ref[0]
    off = _HALO - (CONV_WIDTH - 1)
    acc = jnp.zeros((tm, GROUP_WIDTH), F32)
    for k in range(CONV_WIDTH):
        acc = acc + xin_ref[pl.ds(off + k, tm), :] * w_ref[k:k + 1, :]
    o_ref[0] = _ln_silu(acc + cb_ref[...], lng_ref[...], lnb_ref[...])


def _conv_prompt(glu, w, *, tm):
    B, S, C = glu.shape
    r = tm // _HALO
    full = lambda a: pl.BlockSpec(a.shape, lambda b, i: (0,) * a.ndim)
    ins = [glu, glu, w['conv_w'], w['conv_b'], w['conv_ln_g'], w['conv_ln_b']]
    return pl.pallas_call(
        _conv_prompt_body,
        grid=(B, S // tm),
        in_specs=[pl.BlockSpec((1, tm, C), lambda b, i: (b, i, 0)),
                  pl.BlockSpec((1, _HALO, C), lambda b, i: (b, jnp.maximum(i * r - 1, 0), 0))]
                 + [full(a) for a in ins[2:]],
        out_specs=pl.BlockSpec((1, tm, C), lambda b, i: (b, i, 0)),
        out_shape=jax.ShapeDtypeStruct((B, S, C), F32),
        scratch_shapes=[pltpu.VMEM((tm + _HALO, C), F32)],
        compiler_params=_cparams(("arbitrary", "arbitrary")),
        name="conv_prompt",
    )(*ins)


def _compress_body(x_ref, pe_ref, phi_ref, o_ref):
    o_ref[0] = _dot((x_ref[0] + pe_ref[...]).astype(BF16), phi_ref[...])


def _compress(blocks, pe_flat, phi):
    B, n, width = blocks.shape
    return pl.pallas_call(
        _compress_body,
        grid=(B,),
        in_specs=[pl.BlockSpec((1, n, width), lambda b: (b, 0, 0)),
                  pl.BlockSpec(pe_flat.shape, lambda b: (0, 0)),
                  pl.BlockSpec(phi.shape, lambda b: (0, 0))],
        out_specs=pl.BlockSpec((1, n, LANES), lambda b: (b, 0, 0)),
        out_shape=jax.ShapeDtypeStruct((B, n, LANES), F32),
        compiler_params=_cparams(("arbitrary",)),
        name="nsa_compress",
    )(blocks, pe_flat, phi)


def _masked_softmax(s, mask):
    s = jnp.where(mask, s, NEG_BIG)
    m = jnp.max(s, axis=-1, keepdims=True)
    e = jnp.where(mask, jnp.exp(s - m), 0.0)
    return e / jnp.maximum(jnp.sum(e, axis=-1, keepdims=True), 1e-30)


def _stack_heads(q):
    return jnp.concatenate([q[:, h * HEAD_DIM:(h + 1) * HEAD_DIM] for h in range(N_HEADS)], axis=0)


def _select_blocks(imp, cur, n_select):
    j = lax.broadcasted_iota(jnp.int32, imp.shape, 1)
    forced = (j == 0) | (j == cur) | (j == cur - 1)
    v = jnp.where(forced, FORCED_SCORE, imp)
    v = jnp.where(j <= cur, v, -1.0)
    jf = j.astype(F32)
    sel = jnp.zeros(imp.shape, jnp.bool_)
    for _ in range(n_select):
        m = jnp.max(v, axis=-1, keepdims=True)
        idx = jnp.min(jnp.where(v == m, jf, float(imp.shape[1])), axis=-1, keepdims=True)
        pick = jf == idx
        sel = sel | (pick & (m >= 0.0))
        v = jnp.where(pick, -2.0, v)
    return sel


def _nsa_local_body(q_ref, g_ref, kc_ref, vc_ref, kv_ref, o_ref, sbt_ref):
    t = q_ref.shape[1]
    nc = kc_ref.shape[2]
    half = nc // 2
    st = pl.program_id(1) * t
    qs = _stack_heads(q_ref[0])
    qpos = st + lax.broadcasted_iota(jnp.int32, (N_HEADS * t, 1), 0) % t

    c = lax.broadcasted_iota(jnp.int32, (1, nc), 1)
    blk = jnp.where(c < half, 2 * c, 2 * (c - half) + 1)
    p_cmp = _masked_softmax(_dot(qs, kc_ref[0]), (blk + 1) * CMP_BLOCK - 1 <= qpos)
    o_cmp = _dot(p_cmp.astype(BF16), vc_ref[0])
    imp = p_cmp[0:t] + p_cmp[t:2 * t] + p_cmp[2 * t:3 * t] + p_cmp[3 * t:4 * t]
    imp = imp[:, :half] + imp[:, half:]
    sel = _select_blocks(imp, qpos[0:t] // SEL_BLOCK, min(N_SELECT, half))
    sb = jnp.where(sel, 0.0, SEL_NEG)
    if half < LANES:
        sb = jnp.concatenate([sb, jnp.zeros((t, LANES - half), F32)], axis=1)
    sbt_ref[0] = jnp.transpose(sb).astype(BF16)

    sub = min(t, LANES)
    span = WINDOW + sub
    o_win = [[] for _ in range(N_HEADS)]
    for sb in range(t // sub):
        st_s = st + sb * sub
        q_s = jnp.concatenate([qs[h * t + sb * sub:h * t + (sb + 1) * sub] for h in range(N_HEADS)], axis=0)
        qpos_s = st_s + lax.broadcasted_iota(jnp.int32, (N_HEADS * sub, 1), 0) % sub
        start = pl.multiple_of(jnp.maximum(st_s - WINDOW, 0), LANES)
        kwpos = start + lax.broadcasted_iota(jnp.int32, (1, span), 1)
        wmask = (kwpos <= qpos_s) & (qpos_s - kwpos < WINDOW)
        s = jnp.where(wmask, _dot(q_s, kv_ref[0, 128:192, pl.ds(start, span)]), NEG_BIG)
        e = jnp.exp(s - jnp.max(s, axis=-1, keepdims=True))
        o_s = _nt(e.astype(BF16), kv_ref[0, 192:256, pl.ds(start, span)]) / jnp.sum(e, axis=-1, keepdims=True)
        for h in range(N_HEADS):
            o_win[h].append(o_s[h * sub:(h + 1) * sub])

    g = g_ref[0]
    outs = [g[:, 3 * h:3 * h + 1] * o_cmp[h * t:(h + 1) * t]
            + g[:, 3 * h + 2:3 * h + 3] * jnp.concatenate(o_win[h], axis=0) for h in range(N_HEADS)]
    o_ref[0] = jnp.concatenate(outs, axis=-1)


def _nsa_local(qc, gc, kc_t, vc, nsabf, *, t):
    B, S, _ = qc.shape
    nc = kc_t.shape[2]
    tile = lambda width: pl.BlockSpec((1, t, width), lambda b, i: (b, i, 0))
    return pl.pallas_call(
        _nsa_local_body,
        grid=(B, S // t),
        in_specs=[tile(GROUP_WIDTH), tile(LANES),
                  pl.BlockSpec((1, HEAD_DIM, nc), lambda b, i: (b, 0, 0)),
                  pl.BlockSpec((1, nc, HEAD_DIM), lambda b, i: (b, 0, 0)),
                  pl.BlockSpec((1, 256, S), lambda b, i: (b, 0, 0))],
        out_specs=[tile(GROUP_WIDTH), pl.BlockSpec((1, max(nc // 2, LANES), t), lambda b, i: (b, 0, i))],
        out_shape=[jax.ShapeDtypeStruct((B, S, GROUP_WIDTH), F32),
                   jax.ShapeDtypeStruct((B, max(nc // 2, LANES), S), BF16)],
        compiler_params=_cparams(("arbitrary", "arbitrary")),
        name="nsa_local",
    )(qc, gc, kc_t, vc, nsabf)


def _nsa_sel_body(qt_ref, sbt_ref, g_ref, part_ref, ks_ref, kv_ref, o_ref, *, tk):
    t = qt_ref.shape[2]
    st = pl.program_id(1) * t
    n_grp, per = N_HEADS, 1
    pad = jnp.zeros((LANES - HEAD_DIM, per * t), BF16)
    bias = jnp.concatenate([sbt_ref[0]] * per, axis=1)
    q2 = [jnp.concatenate([jnp.concatenate([qt_ref[0, h * HEAD_DIM:(h + 1) * HEAD_DIM, :]
                                            for h in range(c * per, (c + 1) * per)], axis=1), pad, bias], axis=0)
          for c in range(n_grp)]
    qpos = st + lax.broadcasted_iota(jnp.int32, (1, per * t), 1) % t

    def tile(kt, states, masked):
        start = pl.multiple_of(kt * tk, tk)
        keys = ks_ref[0, pl.ds(start, tk), :]
        scores = [_dot(keys, q2[c]) for c in range(n_grp)]
        v = kv_ref[0, HEAD_DIM:2 * HEAD_DIM, pl.ds(start, tk)]
        out = []
        for c in range(n_grp):
            s = scores[c]
            if masked:
                kpos = start + lax.broadcasted_iota(jnp.int32, (tk, 1), 0)
                s = jnp.where(kpos <= qpos, s, NEG_BIG)
            out.append(_col_softmax_step(s, v, states[c]))
        return tuple(out)

    last = (st + t - 1) // tk
    init = tuple(_col_softmax_init(per * t) for _ in range(n_grp))
    states = tile(last, lax.fori_loop(0, last, functools.partial(tile, masked=False), init), True)
    g = g_ref[0]
    outs = []
    for h in range(N_HEADS):
        o_sel = jnp.transpose(_col_softmax_result(states[h // per])[:, (h % per) * t:(h % per + 1) * t])
        outs.append(g[:, 3 * h + 1:3 * h + 2] * o_sel)
    o_ref[0] = part_ref[0] + jnp.concatenate(outs, axis=-1)


def _nsa_sel(qct, selbt, gc, part, ksrow, nsabf, *, t, tk):
    B, _, S = qct.shape
    ns = selbt.shape[1]
    tile = lambda width: pl.BlockSpec((1, t, width), lambda b, i: (b, i, 0))
    return pl.pallas_call(
        functools.partial(_nsa_sel_body, tk=tk),
        grid=(B, S // t),
        in_specs=[pl.BlockSpec((1, GROUP_WIDTH, t), lambda b, i: (b, 0, i)),
                  pl.BlockSpec((1, ns, t), lambda b, i: (b, 0, i)),
                  tile(LANES), tile(GROUP_WIDTH),
                  pl.BlockSpec((1, S, 2 * LANES), lambda b, i: (b, 0, 0)),
                  pl.BlockSpec((1, 256, S), lambda b, i: (b, 0, 0))],
        out_specs=tile(GROUP_WIDTH),
        out_shape=jax.ShapeDtypeStruct((B, S, GROUP_WIDTH), F32),
        compiler_params=_cparams(("arbitrary", "arbitrary")),
        name="nsa_sel",
    )(qct, selbt, gc, part, ksrow, nsabf)


def _nsa_prompt(o, w, *, t, tk, tl):
    B, S, _ = o['qc'].shape
    n = S // CMP_BLOCK
    cmp = _compress(o['cmprow'].reshape(B, n, CMP_BLOCK * LANES), w['pe_flat'], w['phi'])
    kc_t, vc = _even_odd(cmp)
    part, selbias = _nsa_local(o['qc'], o['gc'], kc_t, vc, o['nsabf'], t=tl)
    assert S // SEL_BLOCK <= LANES
    return _nsa_sel(o['qct'], selbias, o['gc'], part, o['ksrow'], o['nsabf'], t=t, tk=tk)


def _merge_ffn_body(x_ref, oa_ref, ob_ref, oc_ref, od_ref, gn_ref, wout_ref, g2_ref, w1_ref, w2_ref,
                    y_ref, hn_ref, acc_ref):
    j = pl.program_id(1)

    @pl.when(j == 0)
    def _():
        h = x_ref[...]
        for i, o_ref in enumerate((oa_ref, ob_ref, oc_ref, od_ref)):
            o = o_ref[...]
            o = o * lax.rsqrt(jnp.mean(o * o, axis=-1, keepdims=True) + EPS) * gn_ref[i:i + 1, :]
            h = h + _dot(o.astype(BF16), wout_ref[i * GROUP_WIDTH:(i + 1) * GROUP_WIDTH, :])
        acc_ref[...] = h
        hn = h * lax.rsqrt(jnp.mean(h * h, axis=-1, keepdims=True) + EPS) * g2_ref[...]
        hn_ref[...] = hn.astype(BF16)

    u = jnp.maximum(_dot(hn_ref[...], w1_ref[...]), 0.0)
    acc_ref[...] += _dot((u * u).astype(BF16), w2_ref[...])

    @pl.when(j == pl.num_programs(1) - 1)
    def _():
        y_ref[...] = acc_ref[...]


def _merge_ffn(x, outs, w, *, tm, tf):
    R, D = x.shape
    F = w['w_ff1'].shape[1]
    rows = lambda width: pl.BlockSpec((tm, width), lambda i, j: (i, 0))
    full = lambda a: pl.BlockSpec(a.shape, lambda i, j: (0,) * a.ndim)
    return pl.pallas_call(
        _merge_ffn_body,
        grid=(R // tm, F // tf),
        in_specs=[rows(D)] + [rows(GROUP_WIDTH)] * 4 + [full(w['gnorm_g']), full(w['w_out']), full(w['norm2_g']),
                  pl.BlockSpec((D, tf), lambda i, j: (0, j)), pl.BlockSpec((tf, D), lambda i, j: (j, 0))],
        out_specs=rows(D),
        out_shape=jax.ShapeDtypeStruct((R, D), F32),
        scratch_shapes=[pltpu.VMEM((tm, D), BF16), pltpu.VMEM((tm, D), F32)],
        compiler_params=_cparams(("arbitrary", "arbitrary")),
        name="merge_ffn",
    )(x, *outs, w['gnorm_g'], w['w_out'], w['norm2_g'], w['w_ff1'], w['w_ff2'])


def _dot3(parts, rhs=None, lhs=None):
    if rhs is not None:
        return sum(_dot(p.astype(BF16), rhs) for p in parts)
    return sum(_dot(lhs, p.astype(BF16)) for p in parts)


def _page_scan_matrices(pg):
    r = np.arange(pg * N_HEADS)
    g, h = r // N_HEADS, r % N_HEADS
    s = np.arange(LANES)
    same = h[:, None] == h[None, :]
    return (jnp.asarray(s[:, None] >= s[None, :], BF16), jnp.asarray(same & (g[None, :] > g[:, None]), BF16),
            jnp.asarray(same, BF16))


def _same_seq_causal(b, t_new, n_cols, rows):
    col = lax.broadcasted_iota(jnp.int32, (rows, n_cols), 1)
    t = lax.broadcasted_iota(jnp.int32, (rows, n_cols), 0) % t_new
    return (col // t_new == b) & (col % t_new <= t)


def _fox_sample_body(pt_ref, q_ref, knew_ref, vnew_ref, tri_ref, later_ref, same_ref, kv_hbm, lf_hbm, o_ref,
                     kvbuf, lfbuf, sems, m_ref, l_ref, acc_ref, carry_ref, *, layer, pg):
    b, j = pl.program_id(0), pl.program_id(1)
    nb, nch = pl.num_programs(0), pl.num_programs(1)
    step = b * nch + j
    slot = step % 2
    t_new = q_ref.shape[0]
    rows = N_HEADS * t_new

    def copies(bb, jj, sl):
        first = (nch - 1 - jj) * pg
        out = []
        for g in range(pg):
            pid = pt_ref[bb, first + g]
            out.append(pltpu.make_async_copy(kv_hbm.at[layer, pid], kvbuf.at[sl, g], sems.at[0, sl]))
            out.append(pltpu.make_async_copy(lf_hbm.at[layer, pid], lfbuf.at[sl, pl.ds(g * N_HEADS, N_HEADS)],
                                             sems.at[1, sl]))
        return out

    @pl.when(step == 0)
    def _():
        for c in copies(b, j, slot):
            c.start()

    @pl.when(step + 1 < nb * nch)
    def _():
        nxt = step + 1
        for c in copies(nxt // nch, nxt % nch, 1 - slot):
            c.start()

    @pl.when(j == 0)
    def _():
        m_ref[...] = jnp.full_like(m_ref, NEG_BIG)
        l_ref[...] = jnp.zeros_like(l_ref)
        acc_ref[...] = jnp.zeros_like(acc_ref)
        carry_ref[...] = jnp.zeros_like(carry_ref)

    qv = q_ref[...]
    lane = lax.broadcasted_iota(jnp.int32, (t_new, LANES), 1)
    q4 = jnp.concatenate([jnp.where(lane < HEAD_DIM, qv[:, 0:128], qv[:, 128:256]),
                          jnp.where(lane < HEAD_DIM, qv[:, 256:384], qv[:, 384:512])], axis=-1)
    grp = lax.broadcasted_iota(jnp.int32, (t_new, GROUP_WIDTH), 1) // HEAD_DIM
    wq = jnp.concatenate([jnp.where(grp == h, q4, 0.0) for h in range(N_HEADS)], axis=0).astype(BF16)

    for c in copies(b, j, slot):
        c.wait()

    lf = lfbuf[slot]
    incl = _dot3(_split3(lf), rhs=tri_ref[...])
    tot = _split3(jnp.broadcast_to(incl[:, 0:1], lf.shape))
    suf = incl - lf + _dot3(tot, lhs=later_ref[...]) + carry_ref[...]
    carry_ref[...] += _dot3(tot, lhs=same_ref[...])

    scores = []
    for g in range(pg):
        bias = jnp.concatenate([jnp.broadcast_to(suf[g * N_HEADS + h:g * N_HEADS + h + 1], (t_new, LANES))
                                for h in range(N_HEADS)], axis=0)
        scores.append(_dot(wq, kvbuf[slot, g, 0:GROUP_WIDTH, :].astype(BF16)) + bias)
    m_old = m_ref[...]
    m = jnp.maximum(m_old, jnp.max(functools.reduce(jnp.maximum, scores), axis=-1, keepdims=True))
    alpha = jnp.exp(m_old - m)
    psum = jnp.zeros((rows, LANES), F32)
    acc = alpha * acc_ref[...]
    for g in range(pg):
        p = jnp.exp(scores[g] - m)
        psum = psum + p
        acc = acc + _nt(p.astype(BF16), kvbuf[slot, g, GROUP_WIDTH:, :].astype(BF16))
    l = alpha * l_ref[...] + jnp.sum(psum, axis=-1, keepdims=True)
    m_ref[...], l_ref[...], acc_ref[...] = m, l, acc

    @pl.when(j == nch - 1)
    def _():
        n_cols = knew_ref.shape[2]
        s = jnp.concatenate([_dot(qv[:, h * LANES:(h + 1) * LANES].astype(BF16), knew_ref[h])
                             for h in range(N_HEADS)], axis=0)
        s = jnp.where(_same_seq_causal(b, t_new, n_cols, rows), s, NEG_BIG)
        m2, l2, acc2 = _online_softmax_step(s, vnew_ref[...], (m, l, acc))
        o = acc2 / l2
        o_ref[...] = jnp.concatenate(
            [o[h * t_new:(h + 1) * t_new, h * HEAD_DIM:(h + 1) * HEAD_DIM] for h in range(N_HEADS)], axis=-1)


def _fox_sample(layer, page_table, q_aug, kaug_new, v_new, cache_kv, cache_lf, *, t_new, pg):
    db, n_pages = page_table.shape
    rows = N_HEADS * t_new
    mats = _page_scan_matrices(pg)
    full = lambda a: pl.BlockSpec(a.shape, lambda b, j, pt: (0,) * a.ndim)
    grid_spec = pltpu.PrefetchScalarGridSpec(
        num_scalar_prefetch=1,
        grid=(db, n_pages // pg),
        in_specs=[pl.BlockSpec((t_new, 4 * LANES), lambda b, j, pt: (b, 0)), full(kaug_new), full(v_new)]
                 + [full(a) for a in mats] + [pl.BlockSpec(memory_space=pl.ANY), pl.BlockSpec(memory_space=pl.ANY)],
        out_specs=pl.BlockSpec((t_new, GROUP_WIDTH), lambda b, j, pt: (b, 0)),
        scratch_shapes=[pltpu.VMEM((2, pg) + cache_kv.shape[2:], F32), pltpu.VMEM((2, pg * N_HEADS, LANES), F32),
                        pltpu.SemaphoreType.DMA((2, 2)),
                        pltpu.VMEM((rows, 1), F32), pltpu.VMEM((rows, 1), F32), pltpu.VMEM((rows, GROUP_WIDTH), F32),
                        pltpu.VMEM((pg * N_HEADS, LANES), F32)])
    return pl.pallas_call(
        functools.partial(_fox_sample_body, layer=layer, pg=pg),
        grid_spec=grid_spec,
        out_shape=jax.ShapeDtypeStruct((db * t_new, GROUP_WIDTH), F32),
        compiler_params=_cparams(("arbitrary", "arbitrary")),
        name="fox_sample",
    )(page_table, q_aug, kaug_new, v_new, *mats, cache_kv, cache_lf)


def _conv_sample_body(state_ref, glu_ref, w_ref, cb_ref, lng_ref, lnb_ref, o_ref, new_ref):
    n_state, t_new = state_ref.shape[1], glu_ref.shape[0]
    x = lambda i: state_ref[0, i] if i < n_state else glu_ref[i - n_state]
    for t in range(t_new):
        acc = x(t) * w_ref[0:1, :]
        for k in range(1, CONV_WIDTH):
            acc = acc + x(t + k) * w_ref[k:k + 1, :]
        o_ref[t] = _ln_silu(acc + cb_ref[...], lng_ref[...], lnb_ref[...])
    for i in range(n_state):
        new_ref[i] = x(i + t_new)


def _conv_sample(layer, state_t, glu_t, w):
    _, n_state, db, c = state_t.shape
    t_new = glu_t.shape[0]
    full = lambda a: pl.BlockSpec(a.shape, lambda i: (0,) * a.ndim)
    ins = [state_t, glu_t, w['conv_w'], w['conv_b'], w['conv_ln_g'], w['conv_ln_b']]
    return pl.pallas_call(
        _conv_sample_body,
        grid=(1,),
        in_specs=[pl.BlockSpec((1, n_state, db, c), lambda i: (layer, 0, 0, 0))] + [full(a) for a in ins[1:]],
        out_specs=[pl.BlockSpec((t_new, db, c), lambda i: (0, 0, 0)), pl.BlockSpec((n_state, db, c), lambda i: (0, 0, 0))],
        out_shape=[jax.ShapeDtypeStruct((t_new, db, c), F32), jax.ShapeDtypeStruct((n_state, db, c), F32)],
        compiler_params=_cparams(("arbitrary",)),
        name="conv_sample",
    )(*ins)


_CMP_PER_PAGE = 4


def _compress_sample_body(pt_ref, *refs, n_group):
    x_refs, (pe_ref, perm_ref, phi_ref, o_ref, x_scr) = refs[:n_group], refs[n_group:]
    pe = pe_ref[...]
    for pair in range(n_group // 2):
        z = jnp.concatenate([x_refs[2 * pair][0, 0] + pe, x_refs[2 * pair + 1][0, 0] + pe], axis=1)
        y = _nt(perm_ref[...], z.astype(BF16))
        for r in range(CMP_BLOCK):
            x_scr[r // 2, pair * 8:(pair + 1) * 8, (r % 2) * LANES:(r % 2 + 1) * LANES] = y[r * 8:(r + 1) * 8]
    acc = jnp.zeros((n_group * _CMP_PER_PAGE, LANES), F32)
    for a in range(CMP_BLOCK // 2):
        acc = acc + _dot(x_scr[a].astype(BF16), phi_ref[a * 2 * LANES:(a + 1) * 2 * LANES, :])
    o_ref[0] = acc


def _pair_permutation(page):
    rows = np.arange(2 * page)
    r, g, n = rows // 8, (rows // _CMP_PER_PAGE) % 2, rows % _CMP_PER_PAGE
    src = g * page + n * CMP_BLOCK + r
    return jnp.asarray(src[:, None] == np.arange(2 * page)[None, :], BF16)


def _compress_sample(layer, page_table, cache_nsa, pe_col, phi, *, n_group):
    db, n_pages = page_table.shape
    page = cache_nsa.shape[3]
    page_spec = lambda g: pl.BlockSpec((1, 1, LANES, page), lambda b, j, pt: (layer, pt[b, j * n_group + g], 0, 0))
    rows = n_group * _CMP_PER_PAGE
    perm = _pair_permutation(page)
    full = lambda a: pl.BlockSpec(a.shape, lambda b, j, pt: (0, 0))
    grid_spec = pltpu.PrefetchScalarGridSpec(
        num_scalar_prefetch=1,
        grid=(db, n_pages // n_group),
        in_specs=[page_spec(g) for g in range(n_group)] + [full(pe_col), full(perm), full(phi)],
        out_specs=pl.BlockSpec((1, rows, LANES), lambda b, j, pt: (b, j, 0)),
        scratch_shapes=[pltpu.VMEM((CMP_BLOCK // 2, rows, 2 * LANES), F32)])
    return pl.pallas_call(
        functools.partial(_compress_sample_body, n_group=n_group),
        grid_spec=grid_spec,
        out_shape=jax.ShapeDtypeStruct((db, n_pages * _CMP_PER_PAGE, LANES), F32),
        compiler_params=_cparams(("arbitrary", "arbitrary")),
        name="nsa_compress_sample",
    )(page_table, *([cache_nsa] * n_group), pe_col, perm, phi)


def _nsa_local_sample_one(b, q, g, kc, vc, win_k, win_v, new_ref, past_len):
    t_new = q.shape[0]
    rows = N_HEADS * t_new
    nc = kc.shape[1]
    half = nc // 2
    qs = _stack_heads(q).astype(BF16)
    tq = lax.broadcasted_iota(jnp.int32, (rows, 1), 0) % t_new
    qpos = past_len + tq

    c = lax.broadcasted_iota(jnp.int32, (1, nc), 1)
    blk = jnp.where(c < half, 2 * c, 2 * (c - half) + 1)
    p_cmp = _masked_softmax(_dot(qs, kc), (blk + 1) * CMP_BLOCK - 1 <= qpos)
    o_cmp = _dot(p_cmp.astype(BF16), vc)
    imp = p_cmp[0:t_new]
    for h in range(1, N_HEADS):
        imp = imp + p_cmp[h * t_new:(h + 1) * t_new]
    imp = imp[:, :half] + imp[:, half:]
    cur = jnp.full((t_new, 1), past_len // SEL_BLOCK, jnp.int32)
    sel = _select_blocks(imp, cur, min(N_SELECT, half + 1) - 1)
    j = lax.broadcasted_iota(jnp.int32, sel.shape, 1)
    t = lax.broadcasted_iota(jnp.int32, sel.shape, 0)
    weight = jnp.left_shift(1, 2 * t + j % 2).astype(F32)
    colsum = jnp.sum(jnp.where(sel, weight, 0.0), axis=0, keepdims=True)
    flags = colsum + pltpu.roll(colsum, half - 1, 1)

    w = win_k.shape[1]
    n_cols = new_ref.shape[1]
    s_old = _dot(qs, win_k.astype(BF16))
    s_new = _dot(qs, new_ref[128:192, :])
    i_old = lax.broadcasted_iota(jnp.int32, (1, w), 1)
    mask = jnp.concatenate([jnp.broadcast_to(i_old + (WINDOW - w) > tq, (rows, w)),
                            _same_seq_causal(b, t_new, n_cols, rows)], axis=-1)
    p_win = _masked_softmax(jnp.concatenate([s_old, s_new], axis=-1), mask).astype(BF16)
    o_win = _nt(p_win[:, :w], win_v.astype(BF16)) + _nt(p_win[:, w:], new_ref[192:256, :])

    outs = [g[:, 3 * h:3 * h + 1] * o_cmp[h * t_new:(h + 1) * t_new]
            + g[:, 3 * h + 2:3 * h + 3] * o_win[h * t_new:(h + 1) * t_new] for h in range(N_HEADS)]
    return jnp.concatenate(outs, axis=-1), flags


def _nsa_local_sample_body(q_ref, g_ref, kc_ref, vc_ref, win_ref, new_ref, o_ref, flag_ref, *, past_len, t_new):
    n_seq = kc_ref.shape[0]
    for u in range(n_seq):
        rows = slice(u * t_new, (u + 1) * t_new)
        o, flags = _nsa_local_sample_one(pl.program_id(0) * n_seq + u, q_ref[rows, :], g_ref[rows, :], kc_ref[u],
                                         vc_ref[u], win_ref[0, u, 0:HEAD_DIM, :], win_ref[0, u, HEAD_DIM:, :],
                                         new_ref, past_len)
        o_ref[rows, :] = o
        flag_ref[u] = flags


def _nsa_local_sample(layer, qc, gc, kc_t, vc, win_t, new_bf, *, t_new, past_len):
    db = kc_t.shape[0]
    nc = kc_t.shape[2]
    w = win_t.shape[3]
    n_seq = 4 if db % 4 == 0 else 1
    rows = lambda width: pl.BlockSpec((n_seq * t_new, width), lambda b: (b, 0))
    return pl.pallas_call(
        functools.partial(_nsa_local_sample_body, past_len=past_len, t_new=t_new),
        grid=(db // n_seq,),
        in_specs=[rows(GROUP_WIDTH), rows(LANES),
                  pl.BlockSpec((n_seq, HEAD_DIM, nc), lambda b: (b, 0, 0)),
                  pl.BlockSpec((n_seq, nc, HEAD_DIM), lambda b: (b, 0, 0)),
                  pl.BlockSpec((1, n_seq, 2 * HEAD_DIM, w), lambda b: (layer, b, 0, 0)),
                  pl.BlockSpec(new_bf.shape, lambda b: (0, 0))],
        out_specs=[rows(GROUP_WIDTH), pl.BlockSpec((n_seq, 1, nc // 2), lambda b: (b, 0, 0))],
        out_shape=[jax.ShapeDtypeStruct((db * t_new, GROUP_WIDTH), F32),
                   jax.ShapeDtypeStruct((db, 1, nc // 2), F32)],
        compiler_params=_cparams(("arbitrary",)),
        name="nsa_local_sample",
    )(qc, gc, kc_t, vc, win_t, new_bf)


_SEL_GROUP = 16


def _nsa_sel_sample_body(pt_ref, fl_ref, q_ref, g_ref, part_ref, new_ref, kv_hbm, o_ref,
                         buf, sems, cflag, count, *, layer):
    b = pl.program_id(0)
    nb = pl.num_programs(0)
    n_pages = fl_ref.shape[1]
    slot = b % 2
    t_new = q_ref.shape[0]
    rows = N_HEADS * t_new

    def copy(bb, p, sl, k):
        src = kv_hbm.at[layer, pt_ref[bb, p], pl.ds(2 * HEAD_DIM, 2 * HEAD_DIM)]
        return pltpu.make_async_copy(src, buf.at[sl, k], sems.at[sl])

    def start_all(bb, sl):
        def body(p, k):
            flag = fl_ref[bb, p]

            @pl.when(flag != 0)
            def _():
                copy(bb, p, sl, k).start()
                cflag[sl, k] = flag
            return k + (flag != 0).astype(jnp.int32)
        count[sl] = lax.fori_loop(0, n_pages, body, 0)

    @pl.when(b == 0)
    def _():
        start_all(b, slot)

    @pl.when(b + 1 < nb)
    def _():
        start_all(b + 1, 1 - slot)

    qs = _stack_heads(q_ref[...]).astype(BF16)
    shamt = (2 * (lax.broadcasted_iota(jnp.int32, (rows, LANES), 0) % t_new)
             + lax.broadcasted_iota(jnp.int32, (rows, LANES), 1) // SEL_BLOCK)
    n_fetched = count[slot]

    def wait_body(k, c):
        copy(b, 0, slot, k).wait()
        return c
    lax.fori_loop(0, n_fetched, wait_body, 0)

    def group(gi, carry):
        m_old, l_old, acc = carry
        scores, entries = [], []
        for u in range(_SEL_GROUP):
            k = gi * _SEL_GROUP + u
            entry = jnp.minimum(k, n_fetched - 1)
            flag = jnp.where(k < n_fetched, cflag[slot, entry], 0)
            picked = (jnp.right_shift(jnp.full((rows, LANES), flag, jnp.int32), shamt) & 1) == 1
            scores.append(jnp.where(picked, _dot(qs, buf[slot, entry, 0:HEAD_DIM, :].astype(BF16)), NEG_BIG))
            entries.append(entry)
        m = jnp.maximum(m_old, jnp.max(functools.reduce(jnp.maximum, scores), axis=-1, keepdims=True))
        alpha = jnp.exp(m_old - m)
        psum = jnp.zeros((rows, LANES), F32)
        acc = alpha * acc
        for u in range(_SEL_GROUP):
            p = jnp.exp(scores[u] - m)
            psum = psum + p
            acc = acc + _nt(p.astype(BF16), buf[slot, entries[u], HEAD_DIM:, :].astype(BF16))
        return m, alpha * l_old + jnp.sum(psum, axis=-1, keepdims=True), acc

    carry = lax.fori_loop(0, (n_fetched + _SEL_GROUP - 1) // _SEL_GROUP, group, _softmax_init(rows, HEAD_DIM))
    s = jnp.where(_same_seq_causal(b, t_new, new_ref.shape[1], rows), _dot(qs, new_ref[0:64, :]), NEG_BIG)
    m, l, acc = _online_softmax_step(s, new_ref[64:128, :], carry)
    o_sel = acc / l
    g = g_ref[...]
    outs = [g[:, 3 * h + 1:3 * h + 2] * o_sel[h * t_new:(h + 1) * t_new] for h in range(N_HEADS)]
    o_ref[...] = part_ref[...] + jnp.concatenate(outs, axis=-1)


def _nsa_sel_sample(layer, page_table, flags, qc, gc, part, new_bf, cache_nsa, *, t_new):
    db, n_pages = page_table.shape
    rows = N_HEADS * t_new
    tile = lambda width: pl.BlockSpec((t_new, width), lambda b, pt, fl: (b, 0))
    grid_spec = pltpu.PrefetchScalarGridSpec(
        num_scalar_prefetch=2,
        grid=(db,),
        in_specs=[tile(GROUP_WIDTH), tile(LANES), tile(GROUP_WIDTH),
                  pl.BlockSpec(new_bf.shape, lambda b, pt, fl: (0, 0)),
                  pl.BlockSpec(memory_space=pl.ANY)],
        out_specs=tile(GROUP_WIDTH),
        scratch_shapes=[pltpu.VMEM((2, n_pages, 2 * HEAD_DIM, LANES), F32), pltpu.SemaphoreType.DMA((2,)),
                        pltpu.SMEM((2, n_pages), jnp.int32), pltpu.SMEM((2,), jnp.int32)])
    return pl.pallas_call(
        functools.partial(_nsa_sel_sample_body, layer=layer),
        grid_spec=grid_spec,
        out_shape=jax.ShapeDtypeStruct((db * t_new, GROUP_WIDTH), F32),
        compiler_params=_cparams(("arbitrary",)),
        name="nsa_sel_sample",
    )(page_table, flags, qc, gc, part, new_bf, cache_nsa)


_SPLITS = np.cumsum([0, 256, 256, 256, 4, 512, 256, 384, 12, 256, 256])


def _prep_layer(l, P):
    w_in = P['w_in'][l]
    sec = [w_in[:, _SPLITS[i]:_SPLITS[i + 1]] for i in range(10)]
    qa, ka, va, fa, glu, qc, kvc, gc, ud, vd = sec
    w_row = jnp.concatenate([qa, glu, qc, ud, vd, jnp.pad(gc, ((0, 0), (0, LANES - 12)))], axis=1).astype(BF16)
    w_col = jnp.concatenate([ka, va, kvc, jnp.pad(fa, ((0, 0), (0, 12)))], axis=1).T.astype(BF16)
    row = lambda v: v.reshape(1, -1)
    phi_k = P['nsa_phi_k'][l].reshape(CMP_BLOCK, 1, HEAD_DIM, HEAD_DIM)
    phi_v = P['nsa_phi_v'][l].reshape(CMP_BLOCK, 1, HEAD_DIM, HEAD_DIM)
    zero = jnp.zeros_like(phi_k)
    phi = jnp.concatenate([jnp.concatenate([phi_k, zero], axis=-1), jnp.concatenate([zero, phi_v], axis=-1)], axis=1)
    return dict(
        pe_flat=jnp.transpose(P['nsa_pe'][l], (1, 0, 2)).reshape(1, CMP_BLOCK * LANES),
        pe_col=jnp.tile(jnp.transpose(P['nsa_pe'][l], (0, 2, 1)).reshape(LANES, CMP_BLOCK), (1, _CMP_PER_PAGE)),
        phi=phi.reshape(CMP_BLOCK * LANES, LANES).astype(BF16),
        norm1_g=row(P['norm1_g'][l]), w_row=w_row, w_col=w_col,
        gqa=row(jnp.tile(P['fox_qn_g'][l], N_HEADS)), gka=P['fox_kn_g'][l].reshape(HEAD_DIM, 1),
        bf=jnp.pad(P['fox_bf'][l], (0, 4)).reshape(8, 1),
        gqc=row(jnp.tile(P['nsa_qn_g'][l], N_HEADS)), gkc=P['nsa_kn_g'][l].T,
        gmlp_ln_g=row(P['gmlp_ln_g'][l]), gmlp_ln_b=row(P['gmlp_ln_b'][l]),
        gmlp_ws=P['gmlp_ws'][l], gmlp_bs=P['gmlp_bs'][l],
        gnorm_g=P['gnorm_g'][l], w_out=P['w_out'][l].astype(BF16), norm2_g=row(P['norm2_g'][l]),
        w_ff1=P['w_ff1'][l].astype(BF16), w_ff2=P['w_ff2'][l].astype(BF16),
        conv_w=P['conv_w'][l], conv_b=row(P['conv_b'][l]),
        conv_ln_g=row(P['conv_ln_g'][l]), conv_ln_b=row(P['conv_ln_b'][l]),
    )


def _rope_tables(pos):
    inv = ROPE_THETA ** (-jnp.arange(HALF, dtype=F32) / HALF)
    ang = pos.astype(F32)[:, None] * inv
    cos, sin = jnp.cos(ang), jnp.sin(ang)
    return dict(cos_r=jnp.tile(cos, (1, 4)), sin_r=jnp.tile(jnp.concatenate([-sin, sin], axis=1), (1, 2)),
                cos_t=cos.T, sin_t=sin.T)


def _const_tables():
    g = np.arange(GROUP_WIDTH) // HEAD_DIM
    return dict(gsum=jnp.asarray(g[:, None] == g[None, :], BF16))


def _gmlp_tables(w, seq_len, n_seq):
    t = min(seq_len, CHUNK)
    wm = (w['gmlp_ws'] * jnp.tril(jnp.ones((CHUNK, CHUNK), F32)))[:, :t, :t]
    bs = w['gmlp_bs'][:, :t]
    if seq_len < CHUNK:
        eye = jnp.eye(n_seq, dtype=F32)
        wm = jnp.einsum('ab,gts->gatbs', eye, wm).reshape(N_HEADS, n_seq * t, n_seq * t)
        bs = jnp.tile(bs, (1, n_seq))
    c = wm.shape[1]
    return dict(wm=wm.reshape(N_HEADS * c, c).astype(BF16), bs_tab=jnp.repeat(bs.T, HEAD_DIM, axis=1))


def _layer_prompt(x, w, consts, *, tm, ta, tk, tf):
    B, S, D = x.shape
    tabs = dict(consts, **_gmlp_tables(w, S, B))
    o = _proj(x, w, tabs, tm=tm, chunk=CHUNK)
    o_a = _fox_attn(o['qat'], o['karow'], o['vbf'], t=ta)
    o_b = _conv_prompt(o['glu'], w, tm=tm)
    o_c = _nsa_prompt(o, w, t=tk, tk=tk, tl=min(4 * LANES, S))
    flat = lambda a: a.reshape(B * S, a.shape[-1])
    y = _merge_ffn(flat(x), [flat(o_a), flat(o_b), flat(o_c), flat(o['od'])], w, tm=min(2 * tm, B * S), tf=tf)
    wp = min(WINDOW, S)
    states = dict(fox_kv=o['foxkv'], fox_logf=o['logf'], nsa_kv=o['nsakv'],
                  nsa_win=o['nsawin'][:, :, S - wp:], conv=o['glu'][:, S - (CONV_WIDTH - 1):])
    return y.reshape(B, S, D), states


def _even_odd(cmp):
    n = cmp.shape[1]
    order = np.concatenate([np.arange(0, n, 2), np.arange(1, n, 2)])
    cmp = cmp[:, order].astype(BF16)
    return jnp.swapaxes(cmp[:, :, :HEAD_DIM], 1, 2), cmp[:, :, HEAD_DIM:]


def _prep_caches(cache_fox_kv, cache_fox_logf, cache_nsa_kv, state_nsa_win, state_conv):
    L, pool, page = cache_fox_kv.shape[:3]
    db = state_nsa_win.shape[1]
    return dict(
        fox_kv=jnp.transpose(cache_fox_kv, (0, 1, 3, 4, 5, 2)).reshape(L, pool, 2 * GROUP_WIDTH, page),
        fox_lf=jnp.transpose(cache_fox_logf, (0, 1, 3, 2)),
        nsa_kv=jnp.transpose(cache_nsa_kv, (0, 1, 3, 4, 2)).reshape(L, pool, 4 * HEAD_DIM, page),
        win=jnp.transpose(state_nsa_win, (0, 1, 3, 4, 2)).reshape(L, db, 2 * HEAD_DIM, -1),
        conv=jnp.transpose(state_conv, (0, 2, 1, 3)),
    )


def _layer_sample(l, xs, w, caches, consts, page_table, *, past_len, pg, tf):
    db, t_new, D = xs.shape
    R = db * t_new
    tabs = dict(consts, **_gmlp_tables(w, t_new, db))
    o = {k: v[0] for k, v in _proj(xs.reshape(1, R, D), w, tabs, tm=R, chunk=R).items()}
    o_a = _fox_sample(l, page_table, o['qa'].astype(F32), o['kaug'], o['vbf'], caches['fox_kv'], caches['fox_lf'],
                      t_new=t_new, pg=pg)
    glu_t = jnp.swapaxes(o['glu'].reshape(db, t_new, GROUP_WIDTH), 0, 1)
    o_b_t, conv_new = _conv_sample(l, caches['conv'], glu_t, w)
    o_b = jnp.swapaxes(o_b_t, 0, 1).reshape(R, GROUP_WIDTH)
    cmp = _compress_sample(l, page_table, caches['nsa_kv'], w['pe_col'], w['phi'], n_group=pg)
    kc_t, vc = _even_odd(cmp)
    qc = o['qc'].astype(F32)
    part, flags = _nsa_local_sample(l, qc, o['gc'], kc_t, vc, caches['win'], o['nsabf'],
                                    t_new=t_new, past_len=past_len)
    flags = flags[:, 0, ::2].astype(jnp.int32)
    o_c = _nsa_sel_sample(l, page_table, flags, qc, o['gc'], part, o['nsabf'], caches['nsa_kv'], t_new=t_new)
    y = _merge_ffn(xs.reshape(R, D), [o_a, o_b, o_c, o['od']], w, tm=R, tf=tf)
    rows = lambda a, *shape: a.T.reshape(db, t_new, *shape)
    win_new = jnp.swapaxes(o['nsawin'].reshape(2 * HEAD_DIM, db, t_new), 0, 1)
    win = jnp.concatenate([caches['win'][l][:, :, t_new:], win_new], axis=-1)
    states = dict(fox_kv=rows(o['foxkv'], 2, N_HEADS, HEAD_DIM), fox_logf=rows(o['logf'], N_HEADS),
                  nsa_kv=rows(o['nsakv'], 4, HEAD_DIM),
                  nsa_win=jnp.transpose(win.reshape(db, 2, HEAD_DIM, -1), (0, 3, 1, 2)),
                  conv=jnp.swapaxes(conv_new, 0, 1), gmlp_v=o['vn'].reshape(db, t_new, GROUP_WIDTH))
    return y.reshape(db, t_new, D), states


def _scan_matrix(n, seg):
    i = np.arange(n)
    return jnp.asarray((i[:, None] <= i[None, :]) & (i[:, None] // seg == i[None, :] // seg), BF16)


_PARAM_NAMES = ('norm1_g', 'w_in', 'fox_bf', 'fox_qn_g', 'fox_kn_g', 'conv_w', 'conv_b', 'conv_ln_g', 'conv_ln_b',
                'nsa_qn_g', 'nsa_kn_g', 'nsa_pe', 'nsa_phi_k', 'nsa_phi_v', 'gmlp_ln_g', 'gmlp_ln_b', 'gmlp_ws',
                'gmlp_bs', 'gnorm_g', 'w_out', 'norm2_g', 'w_ff1', 'w_ff2')


def kernel(x_prompt, x_sample, cache_fox_kv, cache_fox_logf, cache_nsa_kv, state_nsa_win, state_conv, page_table,
           *params):
    P = dict(zip(_PARAM_NAMES, params))
    depth = P['w_in'].shape[0]
    B, S, D = x_prompt.shape
    DB, T, _ = x_sample.shape
    n_pages, page = page_table.shape[1], cache_fox_kv.shape[2]
    past_len = n_pages * page
    assert past_len % SEL_BLOCK == 0 and T <= SEL_BLOCK and past_len >= WINDOW and page == LANES
    tm = 512
    consts_p = dict(_rope_tables(jnp.arange(S)), **_const_tables(), utri=_scan_matrix(tm, tm))
    consts_s = dict(_rope_tables(jnp.tile(past_len + jnp.arange(T), DB)), **_const_tables(),
                    utri=_scan_matrix(DB * T, T))
    caches = _prep_caches(cache_fox_kv, cache_fox_logf, cache_nsa_kv, state_nsa_win, state_conv)
    xp, xs = x_prompt, x_sample
    st_p, st_s = [], []
    for l in range(depth):
        w = _prep_layer(l, P)
        xp, sp = _layer_prompt(xp, w, consts_p, tm=tm, ta=512, tk=512, tf=1024)
        xs, ss = _layer_sample(l, xs, w, caches, consts_s, page_table, past_len=past_len, pg=32, tf=1024)
        st_p.append(sp)
        st_s.append(ss)
    stack_p = lambda k: jnp.stack([s[k] for s in st_p])
    stack_s = lambda k: jnp.stack([s[k] for s in st_s])
    fox_kv_p = jnp.transpose(stack_p('fox_kv').reshape(depth, B, 2, N_HEADS, HEAD_DIM, S), (0, 1, 5, 2, 3, 4))
    fox_logf_p = jnp.transpose(stack_p('fox_logf'), (0, 1, 3, 2))
    nsa_kv_p = jnp.transpose(stack_p('nsa_kv').reshape(depth, B, 4, HEAD_DIM, S), (0, 1, 4, 2, 3))
    nsa_win_p = jnp.transpose(stack_p('nsa_win').reshape(depth, B, 2, HEAD_DIM, -1), (0, 1, 4, 2, 3))
    return (xp, xs, fox_kv_p, stack_s('fox_kv'), fox_logf_p, stack_s('fox_logf'),
            nsa_kv_p, stack_s('nsa_kv'), nsa_win_p, stack_s('nsa_win'),
            stack_p('conv'), stack_s('conv'), stack_s('gmlp_v'))
```

```python
import functools

import jax
import jax.numpy as jnp
import numpy as np
from jax import lax
from jax.experimental import pallas as pl
from jax.experimental.pallas import tpu as pltpu

F32 = jnp.float32
BF16 = jnp.bfloat16

HEAD_DIM = 64
HALF = HEAD_DIM // 2
GROUP_WIDTH = 256
N_HEADS = GROUP_WIDTH // HEAD_DIM
CONV_WIDTH = 31
CMP_BLOCK = 32
SEL_BLOCK = 64
N_SELECT = 16
WINDOW = 512
CHUNK = 128
ROPE_THETA = 10000.0
EPS = 1e-6
FORCED_SCORE = 1e4
Q_SCALE = HEAD_DIM ** -0.5
LOG2E = 1.4426950408889634
NEG_BIG = -1e30
SEL_NEG = -32768.0
LANES = 128
VMEM_LIMIT = 56 * 1024 * 1024


def _cparams(sem):
    return pltpu.CompilerParams(dimension_semantics=sem, vmem_limit_bytes=VMEM_LIMIT)


def _nt(a, b):
    return lax.dot_general(a, b, (((1,), (1,)), ((), ())), preferred_element_type=F32)


def _dot(a, b):
    return jnp.dot(a, b, preferred_element_type=F32)


def _split3(x):
    h = x.astype(BF16).astype(F32)
    r = x - h
    m = r.astype(BF16).astype(F32)
    l = (r - m).astype(BF16).astype(F32)
    return h, m, l


def _log_sigmoid(x):
    return jnp.minimum(x, 0.0) - jnp.log1p(jnp.exp(-jnp.abs(x)))


def _group_mean_sq(x, gsum):
    x2 = x * x
    hi = x2.astype(BF16)
    lo = (x2 - hi.astype(F32)).astype(BF16)
    return (_dot(hi, gsum) + _dot(lo, gsum)) * (1.0 / HEAD_DIM)


def _rope_rows(x, cos, sin_signed):
    lane = lax.broadcasted_iota(jnp.int32, x.shape, 1)
    first_half = (lane % HEAD_DIM) < HALF
    swapped = jnp.where(first_half, pltpu.roll(x, LANES - HALF, 1), pltpu.roll(x, HALF, 1))
    return x * cos + swapped * sin_signed


_R_QA, _R_GLU, _R_QC, _R_UD, _R_VD, _R_SMALL, _R_END = 0, 256, 768, 1024, 1280, 1536, 1664
_C_KA, _C_VA, _C_KVC, _C_FA, _C_END = 0, 256, 512, 896, 912


def _proj_body(x_ref, g1_ref, wrow_ref, wcol_ref, cosr_ref, sinr_ref, cost_ref, sint_ref,
               gsum_ref, gqa_ref, gka_ref, bf_ref, gqc_ref, gkc_ref, lng_ref, lnb_ref,
               wm_ref, bstab_ref, utri_ref,
               qa_ref, foxkv_ref, kaug_ref, vbf_ref, logf_ref, glu_ref, qc_ref, nsakv_ref,
               nsawin_ref, nsabf_ref, cmprow_ref, gc_ref, od_ref, vn_ref, qat_ref, karow_ref, qct_ref, ksrow_ref,
               carry_ref, *, chunk):
    tm = x_ref.shape[1]

    @pl.when(pl.program_id(1) == 0)
    def _():
        carry_ref[...] = jnp.zeros_like(carry_ref)

    x = x_ref[0]
    ms = jnp.mean(x * x, axis=-1, keepdims=True)
    xn = ((x * lax.rsqrt(ms + EPS)) * g1_ref[...]).astype(BF16)
    zr = _dot(xn, wrow_ref[...])
    zc = _nt(wcol_ref[...], xn)
    gsum = gsum_ref[...]

    qa = zr[:, _R_QA:_R_QA + 256]
    qa = qa * lax.rsqrt(_group_mean_sq(qa, gsum) + EPS) * gqa_ref[...] * Q_SCALE
    lane = lax.broadcasted_iota(jnp.int32, (tm, LANES), 1)
    for h in range(N_HEADS):
        src = qa[:, (h // 2) * LANES:(h // 2 + 1) * LANES]
        if h % 2 == 0:
            aug = jnp.where(lane < HEAD_DIM, src, jnp.where(lane < HEAD_DIM + 3, 1.0, 0.0))
        else:
            aug = jnp.where(lane >= HEAD_DIM, src, jnp.where(lane < 3, 1.0, 0.0))
        qa_ref[0, :, h * LANES:(h + 1) * LANES] = aug.astype(BF16)
        is_q = (lane < HEAD_DIM) if h % 2 == 0 else (lane >= HEAD_DIM)
        qat_ref[0, h] = jnp.transpose(aug * jnp.where(is_q, LOG2E, 1.0)).astype(BF16)

    logf = _log_sigmoid(zc[_C_FA:_C_FA + 8] + bf_ref[...])
    logf_ref[0] = logf[0:N_HEADS]
    parts = _split3(logf)
    l3 = jnp.concatenate(parts, axis=0).astype(BF16)
    cs = _dot(l3, utri_ref[...])
    fcum = cs[0:8] + cs[8:16] + cs[16:24] + carry_ref[:, 0:1]
    carry_ref[...] = jnp.broadcast_to(fcum[:, tm - 1:tm], carry_ref.shape)
    nfh, nfm, nfl = _split3(-fcum)
    nf2 = _split3(-fcum * LOG2E)
    row8 = lax.broadcasted_iota(jnp.int32, (8, tm), 0)
    zeros56 = jnp.zeros((HEAD_DIM - 8, tm), F32)
    gka = gka_ref[:, 0:1]
    for h in range(N_HEADS):
        k = zc[_C_KA + h * HEAD_DIM:_C_KA + (h + 1) * HEAD_DIM]
        k = k * lax.rsqrt(jnp.mean(k * k, axis=0, keepdims=True) + EPS) * gka
        v = zc[_C_VA + h * HEAD_DIM:_C_VA + (h + 1) * HEAD_DIM]
        foxkv_ref[0, h * HEAD_DIM:(h + 1) * HEAD_DIM, :] = k
        foxkv_ref[0, GROUP_WIDTH + h * HEAD_DIM:GROUP_WIDTH + (h + 1) * HEAD_DIM, :] = v
        vbf_ref[0, h * HEAD_DIM:(h + 1) * HEAD_DIM, :] = v.astype(BF16)
        extra8 = jnp.where(row8 == 0, nfh[h:h + 1],
                           jnp.where(row8 == 1, nfm[h:h + 1],
                                     jnp.where(row8 == 2, nfl[h:h + 1], 0.0)))
        extra = jnp.concatenate([extra8, zeros56], axis=0)
        pieces = [k, extra] if h % 2 == 0 else [extra, k]
        kaug_ref[0, h] = jnp.concatenate(pieces, axis=0).astype(BF16)
        extra2 = jnp.where(row8 == 0, nf2[0][h:h + 1],
                           jnp.where(row8 == 1, nf2[1][h:h + 1],
                                     jnp.where(row8 == 2, nf2[2][h:h + 1], 0.0)))
        extra2 = jnp.concatenate([extra2, zeros56], axis=0)
        kaug2 = jnp.concatenate([k, extra2] if h % 2 == 0 else [extra2, k], axis=0)
        karow_ref[0, :, h * LANES:(h + 1) * LANES] = jnp.transpose(kaug2).astype(BF16)

    glu_in = zr[:, _R_GLU:_R_GLU + 512]
    glu_ref[0] = glu_in[:, :256] * jax.nn.sigmoid(glu_in[:, 256:])

    qc = zr[:, _R_QC:_R_QC + 256]
    qc = qc * lax.rsqrt(_group_mean_sq(qc, gsum) + EPS) * gqc_ref[...]
    cosr, sinr = cosr_ref[...], sinr_ref[...]
    for p in range(2):
        qh = _rope_rows(qc[:, p * LANES:(p + 1) * LANES], cosr, sinr) * Q_SCALE
        qc_ref[0, :, p * LANES:(p + 1) * LANES] = qh.astype(BF16)
        qct_ref[0, p * LANES:(p + 1) * LANES, :] = jnp.transpose(qh * LOG2E).astype(BF16)
    gc_ref[0] = jax.nn.sigmoid(zr[:, _R_SMALL:_R_SMALL + LANES])
    cost, sint = cost_ref[...], sint_ref[...]
    keys = []
    for b in range(3):
        kb = zc[_C_KVC + 2 * b * HEAD_DIM:_C_KVC + (2 * b + 1) * HEAD_DIM]
        kb = kb * lax.rsqrt(jnp.mean(kb * kb, axis=0, keepdims=True) + EPS) * gkc_ref[:, b:b + 1]
        x1, x2 = kb[:HALF], kb[HALF:]
        keys.append(jnp.concatenate([x1 * cost - x2 * sint, x2 * cost + x1 * sint], axis=0))
    vals = [zc[_C_KVC + (2 * b + 1) * HEAD_DIM:_C_KVC + (2 * b + 2) * HEAD_DIM] for b in range(3)]
    nsakv_ref[0, 0:64, :] = keys[0]
    nsakv_ref[0, 64:128, :] = vals[0]
    nsakv_ref[0, 128:192, :] = keys[1]
    nsakv_ref[0, 192:256, :] = vals[1]
    nsawin_ref[0, 0:64, :] = keys[2]
    nsawin_ref[0, 64:128, :] = vals[2]
    nsabf_ref[0, 0:64, :] = keys[1].astype(BF16)
    nsabf_ref[0, 64:128, :] = vals[1].astype(BF16)
    nsabf_ref[0, 128:192, :] = keys[2].astype(BF16)
    nsabf_ref[0, 192:256, :] = vals[2].astype(BF16)
    cmprow_ref[0] = jnp.transpose(jnp.concatenate([keys[0], vals[0]], axis=0))
    ksrow_ref[0, :, 0:LANES] = jnp.transpose(jnp.concatenate([keys[1], jnp.zeros_like(keys[1])], axis=0)).astype(BF16)
    blk = (pl.program_id(1) * tm + lax.broadcasted_iota(jnp.int32, (tm, LANES), 0)) // SEL_BLOCK
    ksrow_ref[0, :, LANES:] = jnp.where(blk == lane, 1.0, 0.0).astype(BF16)

    ud = zr[:, _R_UD:_R_UD + 256]
    vd = zr[:, _R_VD:_R_VD + 256]
    mu = jnp.mean(vd, axis=-1, keepdims=True)
    var = jnp.mean(jnp.square(vd - mu), axis=-1, keepdims=True)
    vn = (vd - mu) * lax.rsqrt(var + EPS) * lng_ref[...] + lnb_ref[...]
    vn_ref[0] = vn
    grp = lax.broadcasted_iota(jnp.int32, (chunk, GROUP_WIDTH), 1) // HEAD_DIM
    wm = wm_ref[...]
    for c in range(tm // chunk):
        r = _dot(wm, vn[c * chunk:(c + 1) * chunk].astype(BF16))
        mixed = bstab_ref[...]
        for g in range(N_HEADS):
            mixed = mixed + jnp.where(grp == g, r[g * chunk:(g + 1) * chunk], 0.0)
        od_ref[0, c * chunk:(c + 1) * chunk, :] = ud[c * chunk:(c + 1) * chunk] * mixed


def _proj(x, w, tabs, *, tm, chunk):
    B, S, D = x.shape
    ns = S // tm
    row = lambda width: pl.BlockSpec((1, tm, width), lambda b, i: (b, i, 0))
    col = lambda height: pl.BlockSpec((1, height, tm), lambda b, i: (b, 0, i))
    full = lambda a: pl.BlockSpec(a.shape, lambda b, i: (0,) * a.ndim)
    ins = [x, w['norm1_g'], w['w_row'], w['w_col'], tabs['cos_r'], tabs['sin_r'], tabs['cos_t'], tabs['sin_t'],
           tabs['gsum'], w['gqa'], w['gka'], w['bf'], w['gqc'], w['gkc'], w['gmlp_ln_g'], w['gmlp_ln_b'],
           tabs['wm'], tabs['bs_tab'], tabs['utri']]
    in_specs = [row(D), full(ins[1]), full(ins[2]), full(ins[3]),
                pl.BlockSpec((tm, LANES), lambda b, i: (i, 0)), pl.BlockSpec((tm, LANES), lambda b, i: (i, 0)),
                pl.BlockSpec((HALF, tm), lambda b, i: (0, i)), pl.BlockSpec((HALF, tm), lambda b, i: (0, i))]
    in_specs += [full(a) for a in ins[8:]]
    outs = dict(
        qa=(jax.ShapeDtypeStruct((B, S, 4 * LANES), BF16), row(4 * LANES)),
        foxkv=(jax.ShapeDtypeStruct((B, 2 * GROUP_WIDTH, S), F32), col(2 * GROUP_WIDTH)),
        kaug=(jax.ShapeDtypeStruct((B, N_HEADS, LANES, S), BF16),
              pl.BlockSpec((1, N_HEADS, LANES, tm), lambda b, i: (b, 0, 0, i))),
        vbf=(jax.ShapeDtypeStruct((B, GROUP_WIDTH, S), BF16), col(GROUP_WIDTH)),
        logf=(jax.ShapeDtypeStruct((B, N_HEADS, S), F32), col(N_HEADS)),
        glu=(jax.ShapeDtypeStruct((B, S, GROUP_WIDTH), F32), row(GROUP_WIDTH)),
        qc=(jax.ShapeDtypeStruct((B, S, GROUP_WIDTH), BF16), row(GROUP_WIDTH)),
        nsakv=(jax.ShapeDtypeStruct((B, 256, S), F32), col(256)),
        nsawin=(jax.ShapeDtypeStruct((B, 128, S), F32), col(128)),
        nsabf=(jax.ShapeDtypeStruct((B, 256, S), BF16), col(256)),
        cmprow=(jax.ShapeDtypeStruct((B, S, LANES), F32), row(LANES)),
        gc=(jax.ShapeDtypeStruct((B, S, LANES), F32), row(LANES)),
        od=(jax.ShapeDtypeStruct((B, S, GROUP_WIDTH), F32), row(GROUP_WIDTH)),
        vn=(jax.ShapeDtypeStruct((B, S, GROUP_WIDTH), F32), row(GROUP_WIDTH)),
        qat=(jax.ShapeDtypeStruct((B, N_HEADS, LANES, S), BF16),
             pl.BlockSpec((1, N_HEADS, LANES, tm), lambda b, i: (b, 0, 0, i))),
        karow=(jax.ShapeDtypeStruct((B, S, 4 * LANES), BF16), row(4 * LANES)),
        qct=(jax.ShapeDtypeStruct((B, GROUP_WIDTH, S), BF16), col(GROUP_WIDTH)),
        ksrow=(jax.ShapeDtypeStruct((B, S, 2 * LANES), BF16), row(2 * LANES)),
    )
    names = list(outs)
    res = pl.pallas_call(
        functools.partial(_proj_body, chunk=chunk),
        grid=(B, ns),
        in_specs=in_specs,
        out_specs=[outs[n][1] for n in names],
        out_shape=[outs[n][0] for n in names],
        scratch_shapes=[pltpu.VMEM((8, LANES), F32)],
        compiler_params=_cparams(("arbitrary", "arbitrary")),
        name="proj",
    )(*ins)
    return dict(zip(names, res))


def _online_softmax_step(s, v, carry):
    m, l, acc = carry
    m_new = jnp.maximum(m, jnp.max(s, axis=-1, keepdims=True))
    alpha = jnp.exp(m - m_new)
    p = jnp.exp(s - m_new)
    l = alpha * l + jnp.sum(p, axis=-1, keepdims=True)
    acc = alpha * acc + _nt(p.astype(BF16), v)
    return m_new, l, acc


def _softmax_init(rows, dv):
    return (jnp.full((rows, 1), NEG_BIG, F32), jnp.zeros((rows, 1), F32), jnp.zeros((rows, dv), F32))


_SUM_ROWS = 16


def _col_softmax_init(cols):
    return jnp.full((1, cols), NEG_BIG, F32), jnp.zeros((HEAD_DIM + _SUM_ROWS, cols), F32)


def _col_softmax_step(s, v, state):
    m, acc = state
    m_new = jnp.maximum(m, jnp.max(s, axis=0, keepdims=True))
    p = jnp.exp2(s - m_new).astype(BF16)
    v1 = jnp.concatenate([v, jnp.ones((_SUM_ROWS, v.shape[1]), BF16)], axis=0)
    return m_new, jnp.exp2(m - m_new) * acc + _dot(v1, p)


def _col_softmax_result(state):
    acc = state[1]
    return acc[:HEAD_DIM] / acc[HEAD_DIM:HEAD_DIM + 1]


def _fox_attn_body(q_ref, k_ref, v_ref, o_ref):
    t = q_ref.shape[3]
    i = pl.program_id(1)
    key = lax.broadcasted_iota(jnp.int32, (t, t), 0)
    qry = lax.broadcasted_iota(jnp.int32, (t, t), 1)

    def tile(kt, states, masked):
        start = pl.multiple_of(kt * t, t)
        scores = [_dot(k_ref[0, pl.ds(start, t), h * LANES:(h + 1) * LANES], q_ref[0, h])
                  for h in range(N_HEADS)]
        out = []
        for h in range(N_HEADS):
            s = jnp.where(key <= qry, scores[h], NEG_BIG) if masked else scores[h]
            out.append(_col_softmax_step(s, v_ref[0, h * HEAD_DIM:(h + 1) * HEAD_DIM, pl.ds(start, t)], states[h]))
        return tuple(out)

    init = tuple(_col_softmax_init(t) for _ in range(N_HEADS))
    states = tile(i, lax.fori_loop(0, i, functools.partial(tile, masked=False), init), True)
    o_ref[0] = jnp.concatenate([jnp.transpose(_col_softmax_result(st)) for st in states], axis=-1)


def _fox_attn(qat, karow, vbf, *, t):
    B, S, _ = karow.shape
    return pl.pallas_call(
        _fox_attn_body,
        grid=(B, S // t),
        in_specs=[pl.BlockSpec((1, N_HEADS, LANES, t), lambda b, i: (b, 0, 0, i)),
                  pl.BlockSpec((1, S, 4 * LANES), lambda b, i: (b, 0, 0)),
                  pl.BlockSpec((1, GROUP_WIDTH, S), lambda b, i: (b, 0, 0))],
        out_specs=pl.BlockSpec((1, t, GROUP_WIDTH), lambda b, i: (b, i, 0)),
        out_shape=jax.ShapeDtypeStruct((B, S, GROUP_WIDTH), F32),
        compiler_params=_cparams(("arbitrary", "arbitrary")),
        name="fox_attn",
    )(qat, karow, vbf)


_HALO = 32


def _ln_silu(y, g, b):
    mu = jnp.mean(y, axis=-1, keepdims=True)
    var = jnp.mean(jnp.square(y - mu), axis=-1, keepdims=True)
    y = (y - mu) * lax.rsqrt(var + EPS) * g + b
    return y * jax.nn.sigmoid(y)


def _conv_prompt_body(cur_ref, halo_ref, w_ref, cb_ref, lng_ref, lnb_ref, o_ref, xin_ref):
    tm = cur_ref.shape[1]
    first = pl.program_id(1) == 0
    xin_ref[0:_HALO, :] = jnp.where(first, 0.0, halo_ref[0])
    xin_ref[_HALO:, :] = cur_ref[0]
    off = _HALO - (CONV_WIDTH - 1)
    acc = jnp.zeros((tm, GROUP_WIDTH), F32)
    for k in range(CONV_WIDTH):
        acc = acc + xin_ref[pl.ds(off + k, tm), :] * w_ref[k:k + 1, :]
    o_ref[0] = _ln_silu(acc + cb_ref[...], lng_ref[...], lnb_ref[...])


def _conv_prompt(glu, w, *, tm):
    B, S, C = glu.shape
    r = tm // _HALO
    full = lambda a: pl.BlockSpec(a.shape, lambda b, i: (0,) * a.ndim)
    ins = [glu, glu, w['conv_w'], w['conv_b'], w['conv_ln_g'], w['conv_ln_b']]
    return pl.pallas_call(
        _conv_prompt_body,
        grid=(B, S // tm),
        in_specs=[pl.BlockSpec((1, tm, C), lambda b, i: (b, i, 0)),
                  pl.BlockSpec((1, _HALO, C), lambda b, i: (b, jnp.maximum(i * r - 1, 0), 0))]
                 + [full(a) for a in ins[2:]],
        out_specs=pl.BlockSpec((1, tm, C), lambda b, i: (b, i, 0)),
        out_shape=jax.ShapeDtypeStruct((B, S, C), F32),
        scratch_shapes=[pltpu.VMEM((tm + _HALO, C), F32)],
        compiler_params=_cparams(("arbitrary", "arbitrary")),
        name="conv_prompt",
    )(*ins)


def _compress_body(x_ref, pe_ref, phi_ref, o_ref):
    o_ref[0] = _dot((x_ref[0] + pe_ref[...]).astype(BF16), phi_ref[...])


def _compress(blocks, pe_flat, phi):
    B, n, width = blocks.shape
    return pl.pallas_call(
        _compress_body,
        grid=(B,),
        in_specs=[pl.BlockSpec((1, n, width), lambda b: (b, 0, 0)),
                  pl.BlockSpec(pe_flat.shape, lambda b: (0, 0)),
                  pl.BlockSpec(phi.shape, lambda b: (0, 0))],
        out_specs=pl.BlockSpec((1, n, LANES), lambda b: (b, 0, 0)),
        out_shape=jax.ShapeDtypeStruct((B, n, LANES), F32),
        compiler_params=_cparams(("arbitrary",)),
        name="nsa_compress",
    )(blocks, pe_flat, phi)


def _masked_softmax(s, mask):
    s = jnp.where(mask, s, NEG_BIG)
    m = jnp.max(s, axis=-1, keepdims=True)
    e = jnp.where(mask, jnp.exp(s - m), 0.0)
    return e / jnp.maximum(jnp.sum(e, axis=-1, keepdims=True), 1e-30)


def _stack_heads(q):
    return jnp.concatenate([q[:, h * HEAD_DIM:(h + 1) * HEAD_DIM] for h in range(N_HEADS)], axis=0)


def _select_blocks(imp, cur, n_select):
    j = lax.broadcasted_iota(jnp.int32, imp.shape, 1)
    forced = (j == 0) | (j == cur) | (j == cur - 1)
    v = jnp.where(forced, FORCED_SCORE, imp)
    v = jnp.where(j <= cur, v, -1.0)
    jf = j.astype(F32)
    sel = jnp.zeros(imp.shape, jnp.bool_)
    for _ in range(n_select):
        m = jnp.max(v, axis=-1, keepdims=True)
        idx = jnp.min(jnp.where(v == m, jf, float(imp.shape[1])), axis=-1, keepdims=True)
        pick = jf == idx
        sel = sel | (pick & (m >= 0.0))
        v = jnp.where(pick, -2.0, v)
    return sel


def _nsa_local_body(q_ref, g_ref, kc_ref, vc_ref, kv_ref, o_ref, sbt_ref):
    t = q_ref.shape[1]
    nc = kc_ref.shape[2]
    half = nc // 2
    st = pl.program_id(1) * t
    qs = _stack_heads(q_ref[0])
    qpos = st + lax.broadcasted_iota(jnp.int32, (N_HEADS * t, 1), 0) % t

    c = lax.broadcasted_iota(jnp.int32, (1, nc), 1)
    blk = jnp.where(c < half, 2 * c, 2 * (c - half) + 1)
    p_cmp = _masked_softmax(_dot(qs, kc_ref[0]), (blk + 1) * CMP_BLOCK - 1 <= qpos)
    o_cmp = _dot(p_cmp.astype(BF16), vc_ref[0])
    imp = p_cmp[0:t] + p_cmp[t:2 * t] + p_cmp[2 * t:3 * t] + p_cmp[3 * t:4 * t]
    imp = imp[:, :half] + imp[:, half:]
    sel = _select_blocks(imp, qpos[0:t] // SEL_BLOCK, min(N_SELECT, half))
    sb = jnp.where(sel, 0.0, SEL_NEG)
    if half < LANES:
        sb = jnp.concatenate([sb, jnp.zeros((t, LANES - half), F32)], axis=1)
    sbt_ref[0] = jnp.transpose(sb).astype(BF16)

    sub = min(t, LANES)
    span = WINDOW + sub
    o_win = [[] for _ in range(N_HEADS)]
    for sb in range(t // sub):
        st_s = st + sb * sub
        q_s = jnp.concatenate([qs[h * t + sb * sub:h * t + (sb + 1) * sub] for h in range(N_HEADS)], axis=0)
        qpos_s = st_s + lax.broadcasted_iota(jnp.int32, (N_HEADS * sub, 1), 0) % sub
        start = pl.multiple_of(jnp.maximum(st_s - WINDOW, 0), LANES)
        kwpos = start + lax.broadcasted_iota(jnp.int32, (1, span), 1)
        wmask = (kwpos <= qpos_s) & (qpos_s - kwpos < WINDOW)
        s = jnp.where(wmask, _dot(q_s, kv_ref[0, 128:192, pl.ds(start, span)]), NEG_BIG)
        e = jnp.exp(s - jnp.max(s, axis=-1, keepdims=True))
        o_s = _nt(e.astype(BF16), kv_ref[0, 192:256, pl.ds(start, span)]) / jnp.sum(e, axis=-1, keepdims=True)
        for h in range(N_HEADS):
            o_win[h].append(o_s[h * sub:(h + 1) * sub])

    g = g_ref[0]
    outs = [g[:, 3 * h:3 * h + 1] * o_cmp[h * t:(h + 1) * t]
            + g[:, 3 * h + 2:3 * h + 3] * jnp.concatenate(o_win[h], axis=0) for h in range(N_HEADS)]
    o_ref[0] = jnp.concatenate(outs, axis=-1)


def _nsa_local(qc, gc, kc_t, vc, nsabf, *, t):
    B, S, _ = qc.shape
    nc = kc_t.shape[2]
    tile = lambda width: pl.BlockSpec((1, t, width), lambda b, i: (b, i, 0))
    return pl.pallas_call(
        _nsa_local_body,
        grid=(B, S // t),
        in_specs=[tile(GROUP_WIDTH), tile(LANES),
                  pl.BlockSpec((1, HEAD_DIM, nc), lambda b, i: (b, 0, 0)),
                  pl.BlockSpec((1, nc, HEAD_DIM), lambda b, i: (b, 0, 0)),
                  pl.BlockSpec((1, 256, S), lambda b, i: (b, 0, 0))],
        out_specs=[tile(GROUP_WIDTH), pl.BlockSpec((1, max(nc // 2, LANES), t), lambda b, i: (b, 0, i))],
        out_shape=[jax.ShapeDtypeStruct((B, S, GROUP_WIDTH), F32),
                   jax.ShapeDtypeStruct((B, max(nc // 2, LANES), S), BF16)],
        compiler_params=_cparams(("arbitrary", "arbitrary")),
        name="nsa_local",
    )(qc, gc, kc_t, vc, nsabf)


def _nsa_sel_body(qt_ref, sbt_ref, g_ref, part_ref, ks_ref, kv_ref, o_ref, *, tk):
    t = qt_ref.shape[2]
    st = pl.program_id(1) * t
    n_grp, per = N_HEADS, 1
    pad = jnp.zeros((LANES - HEAD_DIM, per * t), BF16)
    bias = jnp.concatenate([sbt_ref[0]] * per, axis=1)
    q2 = [jnp.concatenate([jnp.concatenate([qt_ref[0, h * HEAD_DIM:(h + 1) * HEAD_DIM, :]
                                            for h in range(c * per, (c + 1) * per)], axis=1), pad, bias], axis=0)
          for c in range(n_grp)]
    qpos = st + lax.broadcasted_iota(jnp.int32, (1, per * t), 1) % t

    def tile(kt, states, masked):
        start = pl.multiple_of(kt * tk, tk)
        keys = ks_ref[0, pl.ds(start, tk), :]
        scores = [_dot(keys, q2[c]) for c in range(n_grp)]
        v = kv_ref[0, HEAD_DIM:2 * HEAD_DIM, pl.ds(start, tk)]
        out = []
        for c in range(n_grp):
            s = scores[c]
            if masked:
                kpos = start + lax.broadcasted_iota(jnp.int32, (tk, 1), 0)
                s = jnp.where(kpos <= qpos, s, NEG_BIG)
            out.append(_col_softmax_step(s, v, states[c]))
        return tuple(out)

    last = (st + t - 1) // tk
    init = tuple(_col_softmax_init(per * t) for _ in range(n_grp))
    states = tile(last, lax.fori_loop(0, last, functools.partial(tile, masked=False), init), True)
    g = g_ref[0]
    outs = []
    for h in range(N_HEADS):
        o_sel = jnp.transpose(_col_softmax_result(states[h // per])[:, (h % per) * t:(h % per + 1) * t])
        outs.append(g[:, 3 * h + 1:3 * h + 2] * o_sel)
    o_ref[0] = part_ref[0] + jnp.concatenate(outs, axis=-1)


def _nsa_sel(qct, selbt, gc, part, ksrow, nsabf, *, t, tk):
    B, _, S = qct.shape
    ns = selbt.shape[1]
    tile = lambda width: pl.BlockSpec((1, t, width), lambda b, i: (b, i, 0))
    return pl.pallas_call(
        functools.partial(_nsa_sel_body, tk=tk),
        grid=(B, S // t),
        in_specs=[pl.BlockSpec((1, GROUP_WIDTH, t), lambda b, i: (b, 0, i)),
                  pl.BlockSpec((1, ns, t), lambda b, i: (b, 0, i)),
                  tile(LANES), tile(GROUP_WIDTH),
                  pl.BlockSpec((1, S, 2 * LANES), lambda b, i: (b, 0, 0)),
                  pl.BlockSpec((1, 256, S), lambda b, i: (b, 0, 0))],
        out_specs=tile(GROUP_WIDTH),
        out_shape=jax.ShapeDtypeStruct((B, S, GROUP_WIDTH), F32),
        compiler_params=_cparams(("arbitrary", "arbitrary")),
        name="nsa_sel",
    )(qct, selbt, gc, part, ksrow, nsabf)


def _nsa_prompt(o, w, *, t, tk, tl):
    B, S, _ = o['qc'].shape
    n = S // CMP_BLOCK
    cmp = _compress(o['cmprow'].reshape(B, n, CMP_BLOCK * LANES), w['pe_flat'], w['phi'])
    kc_t, vc = _even_odd(cmp)
    part, selbias = _nsa_local(o['qc'], o['gc'], kc_t, vc, o['nsabf'], t=tl)
    assert S // SEL_BLOCK <= LANES
    return _nsa_sel(o['qct'], selbias, o['gc'], part, o['ksrow'], o['nsabf'], t=t, tk=tk)


def _merge_ffn_body(x_ref, oa_ref, ob_ref, oc_ref, od_ref, gn_ref, wout_ref, g2_ref, w1_ref, w2_ref,
                    y_ref, hn_ref, acc_ref):
    j = pl.program_id(1)

    @pl.when(j == 0)
    def _():
        h = x_ref[...]
        for i, o_ref in enumerate((oa_ref, ob_ref, oc_ref, od_ref)):
            o = o_ref[...]
            o = o * lax.rsqrt(jnp.mean(o * o, axis=-1, keepdims=True) + EPS) * gn_ref[i:i + 1, :]
            h = h + _dot(o.astype(BF16), wout_ref[i * GROUP_WIDTH:(i + 1) * GROUP_WIDTH, :])
        acc_ref[...] = h
        hn = h * lax.rsqrt(jnp.mean(h * h, axis=-1, keepdims=True) + EPS) * g2_ref[...]
        hn_ref[...] = hn.astype(BF16)

    u = jnp.maximum(_dot(hn_ref[...], w1_ref[...]), 0.0)
    acc_ref[...] += _dot((u * u).astype(BF16), w2_ref[...])

    @pl.when(j == pl.num_programs(1) - 1)
    def _():
        y_ref[...] = acc_ref[...]


def _merge_ffn(x, outs, w, *, tm, tf):
    R, D = x.shape
    F = w['w_ff1'].shape[1]
    rows = lambda width: pl.BlockSpec((tm, width), lambda i, j: (i, 0))
    full = lambda a: pl.BlockSpec(a.shape, lambda i, j: (0,) * a.ndim)
    return pl.pallas_call(
        _merge_ffn_body,
        grid=(R // tm, F // tf),
        in_specs=[rows(D)] + [rows(GROUP_WIDTH)] * 4 + [full(w['gnorm_g']), full(w['w_out']), full(w['norm2_g']),
                  pl.BlockSpec((D, tf), lambda i, j: (0, j)), pl.BlockSpec((tf, D), lambda i, j: (j, 0))],
        out_specs=rows(D),
        out_shape=jax.ShapeDtypeStruct((R, D), F32),
        scratch_shapes=[pltpu.VMEM((tm, D), BF16), pltpu.VMEM((tm, D), F32)],
        compiler_params=_cparams(("arbitrary", "arbitrary")),
        name="merge_ffn",
    )(x, *outs, w['gnorm_g'], w['w_out'], w['norm2_g'], w['w_ff1'], w['w_ff2'])


def _dot3(parts, rhs=None, lhs=None):
    if rhs is not None:
        return sum(_dot(p.astype(BF16), rhs) for p in parts)
    return sum(_dot(lhs, p.astype(BF16)) for p in parts)


def _page_scan_matrices(pg):
    r = np.arange(pg * N_HEADS)
    g, h = r // N_HEADS, r % N_HEADS
    s = np.arange(LANES)
    same = h[:, None] == h[None, :]
    return (jnp.asarray(s[:, None] >= s[None, :], BF16), jnp.asarray(same & (g[None, :] > g[:, None]), BF16),
            jnp.asarray(same, BF16))


def _same_seq_causal(b, t_new, n_cols, rows):
    col = lax.broadcasted_iota(jnp.int32, (rows, n_cols), 1)
    t = lax.broadcasted_iota(jnp.int32, (rows, n_cols), 0) % t_new
    return (col // t_new == b) & (col % t_new <= t)


def _fox_sample_body(pt_ref, q_ref, knew_ref, vnew_ref, tri_ref, later_ref, same_ref, kv_hbm, lf_hbm, o_ref,
                     kvbuf, lfbuf, sems, m_ref, l_ref, acc_ref, carry_ref, *, layer, pg):
    b, j = pl.program_id(0), pl.program_id(1)
    nb, nch = pl.num_programs(0), pl.num_programs(1)
    step = b * nch + j
    slot = step % 2
    t_new = q_ref.shape[0]
    rows = N_HEADS * t_new

    def copies(bb, jj, sl):
        first = (nch - 1 - jj) * pg
        out = []
        for g in range(pg):
            pid = pt_ref[bb, first + g]
            out.append(pltpu.make_async_copy(kv_hbm.at[layer, pid], kvbuf.at[sl, g], sems.at[0, sl]))
            out.append(pltpu.make_async_copy(lf_hbm.at[layer, pid], lfbuf.at[sl, pl.ds(g * N_HEADS, N_HEADS)],
                                             sems.at[1, sl]))
        return out

    @pl.when(step == 0)
    def _():
        for c in copies(b, j, slot):
            c.start()

    @pl.when(step + 1 < nb * nch)
    def _():
        nxt = step + 1
        for n, c in enumerate(copies(nxt // nch, nxt % nch, 1 - slot)):
            c.start(priority=1 - n % 2)

    @pl.when(j == 0)
    def _():
        m_ref[...] = jnp.full_like(m_ref, NEG_BIG)
        l_ref[...] = jnp.zeros_like(l_ref)
        acc_ref[...] = jnp.zeros_like(acc_ref)
        carry_ref[...] = jnp.zeros_like(carry_ref)

    qv = q_ref[...]
    lane = lax.broadcasted_iota(jnp.int32, (t_new, LANES), 1)
    q4 = jnp.concatenate([jnp.where(lane < HEAD_DIM, qv[:, 0:128], qv[:, 128:256]),
                          jnp.where(lane < HEAD_DIM, qv[:, 256:384], qv[:, 384:512])], axis=-1)
    grp = lax.broadcasted_iota(jnp.int32, (t_new, GROUP_WIDTH), 1) // HEAD_DIM
    wq = jnp.concatenate([jnp.where(grp == h, q4, 0.0) for h in range(N_HEADS)], axis=0).astype(BF16)

    for c in copies(b, j, slot):
        c.wait()

    lf = lfbuf[slot]
    incl = _dot3(_split3(lf), rhs=tri_ref[...])
    tot = _split3(jnp.broadcast_to(incl[:, 0:1], lf.shape))
    suf = incl - lf + _dot3(tot, lhs=later_ref[...]) + carry_ref[...]
    carry_ref[...] += _dot3(tot, lhs=same_ref[...])

    scores = []
    for g in range(pg):
        bias = jnp.concatenate([jnp.broadcast_to(suf[g * N_HEADS + h:g * N_HEADS + h + 1], (t_new, LANES))
                                for h in range(N_HEADS)], axis=0)
        scores.append(_dot(wq, kvbuf[slot, g, 0:GROUP_WIDTH, :].astype(BF16)) + bias)
    m_old = m_ref[...]
    m = jnp.maximum(m_old, jnp.max(functools.reduce(jnp.maximum, scores), axis=-1, keepdims=True))
    alpha = jnp.exp(m_old - m)
    psum = jnp.zeros((rows, LANES), F32)
    acc = alpha * acc_ref[...]
    for g in range(pg):
        p = jnp.exp(scores[g] - m)
        psum = psum + p
        acc = acc + _nt(p.astype(BF16), kvbuf[slot, g, GROUP_WIDTH:, :].astype(BF16))
    l = alpha * l_ref[...] + jnp.sum(psum, axis=-1, keepdims=True)
    m_ref[...], l_ref[...], acc_ref[...] = m, l, acc

    @pl.when(j == nch - 1)
    def _():
        n_cols = knew_ref.shape[2]
        s = jnp.concatenate([_dot(qv[:, h * LANES:(h + 1) * LANES].astype(BF16), knew_ref[h])
                             for h in range(N_HEADS)], axis=0)
        s = jnp.where(_same_seq_causal(b, t_new, n_cols, rows), s, NEG_BIG)
        m2, l2, acc2 = _online_softmax_step(s, vnew_ref[...], (m, l, acc))
        o = acc2 / l2
        o_ref[...] = jnp.concatenate(
            [o[h * t_new:(h + 1) * t_new, h * HEAD_DIM:(h + 1) * HEAD_DIM] for h in range(N_HEADS)], axis=-1)


def _fox_sample(layer, page_table, q_aug, kaug_new, v_new, cache_kv, cache_lf, *, t_new, pg):
    db, n_pages = page_table.shape
    rows = N_HEADS * t_new
    mats = _page_scan_matrices(pg)
    full = lambda a: pl.BlockSpec(a.shape, lambda b, j, pt: (0,) * a.ndim)
    grid_spec = pltpu.PrefetchScalarGridSpec(
        num_scalar_prefetch=1,
        grid=(db, n_pages // pg),
        in_specs=[pl.BlockSpec((t_new, 4 * LANES), lambda b, j, pt: (b, 0)), full(kaug_new), full(v_new)]
                 + [full(a) for a in mats] + [pl.BlockSpec(memory_space=pl.ANY), pl.BlockSpec(memory_space=pl.ANY)],
        out_specs=pl.BlockSpec((t_new, GROUP_WIDTH), lambda b, j, pt: (b, 0)),
        scratch_shapes=[pltpu.VMEM((2, pg) + cache_kv.shape[2:], F32), pltpu.VMEM((2, pg * N_HEADS, LANES), F32),
                        pltpu.SemaphoreType.DMA((2, 2)),
                        pltpu.VMEM((rows, 1), F32), pltpu.VMEM((rows, 1), F32), pltpu.VMEM((rows, GROUP_WIDTH), F32),
                        pltpu.VMEM((pg * N_HEADS, LANES), F32)])
    return pl.pallas_call(
        functools.partial(_fox_sample_body, layer=layer, pg=pg),
        grid_spec=grid_spec,
        out_shape=jax.ShapeDtypeStruct((db * t_new, GROUP_WIDTH), F32),
        compiler_params=_cparams(("arbitrary", "arbitrary")),
        name="fox_sample",
    )(page_table, q_aug, kaug_new, v_new, *mats, cache_kv, cache_lf)


def _conv_sample_body(state_ref, glu_ref, w_ref, cb_ref, lng_ref, lnb_ref, o_ref, new_ref):
    n_state, t_new = state_ref.shape[1], glu_ref.shape[0]
    x = lambda i: state_ref[0, i] if i < n_state else glu_ref[i - n_state]
    for t in range(t_new):
        acc = x(t) * w_ref[0:1, :]
        for k in range(1, CONV_WIDTH):
            acc = acc + x(t + k) * w_ref[k:k + 1, :]
        o_ref[t] = _ln_silu(acc + cb_ref[...], lng_ref[...], lnb_ref[...])
    for i in range(n_state):
        new_ref[i] = x(i + t_new)


def _conv_sample(layer, state_t, glu_t, w):
    _, n_state, db, c = state_t.shape
    t_new = glu_t.shape[0]
    full = lambda a: pl.BlockSpec(a.shape, lambda i: (0,) * a.ndim)
    ins = [state_t, glu_t, w['conv_w'], w['conv_b'], w['conv_ln_g'], w['conv_ln_b']]
    return pl.pallas_call(
        _conv_sample_body,
        grid=(1,),
        in_specs=[pl.BlockSpec((1, n_state, db, c), lambda i: (layer, 0, 0, 0))] + [full(a) for a in ins[1:]],
        out_specs=[pl.BlockSpec((t_new, db, c), lambda i: (0, 0, 0)), pl.BlockSpec((n_state, db, c), lambda i: (0, 0, 0))],
        out_shape=[jax.ShapeDtypeStruct((t_new, db, c), F32), jax.ShapeDtypeStruct((n_state, db, c), F32)],
        compiler_params=_cparams(("arbitrary",)),
        name="conv_sample",
    )(*ins)


_CMP_PER_PAGE = 4


def _compress_sample_body(pt_ref, *refs, n_group):
    x_refs, (pe_ref, perm_ref, phi_ref, o_ref, x_scr) = refs[:n_group], refs[n_group:]
    pe = pe_ref[...]
    for pair in range(n_group // 2):
        z = jnp.concatenate([x_refs[2 * pair][0, 0] + pe, x_refs[2 * pair + 1][0, 0] + pe], axis=1)
        y = _nt(perm_ref[...], z.astype(BF16))
        for r in range(CMP_BLOCK):
            x_scr[r // 2, pair * 8:(pair + 1) * 8, (r % 2) * LANES:(r % 2 + 1) * LANES] = y[r * 8:(r + 1) * 8]
    acc = jnp.zeros((n_group * _CMP_PER_PAGE, LANES), F32)
    for a in range(CMP_BLOCK // 2):
        acc = acc + _dot(x_scr[a].astype(BF16), phi_ref[a * 2 * LANES:(a + 1) * 2 * LANES, :])
    o_ref[0] = acc


def _pair_permutation(page):
    rows = np.arange(2 * page)
    r, g, n = rows // 8, (rows // _CMP_PER_PAGE) % 2, rows % _CMP_PER_PAGE
    src = g * page + n * CMP_BLOCK + r
    return jnp.asarray(src[:, None] == np.arange(2 * page)[None, :], BF16)


def _compress_sample(layer, page_table, cache_nsa, pe_col, phi, *, n_group):
    db, n_pages = page_table.shape
    page = cache_nsa.shape[3]
    page_spec = lambda g: pl.BlockSpec((1, 1, LANES, page), lambda b, j, pt: (layer, pt[b, j * n_group + g], 0, 0))
    rows = n_group * _CMP_PER_PAGE
    perm = _pair_permutation(page)
    full = lambda a: pl.BlockSpec(a.shape, lambda b, j, pt: (0, 0))
    grid_spec = pltpu.PrefetchScalarGridSpec(
        num_scalar_prefetch=1,
        grid=(db, n_pages // n_group),
        in_specs=[page_spec(g) for g in range(n_group)] + [full(pe_col), full(perm), full(phi)],
        out_specs=pl.BlockSpec((1, rows, LANES), lambda b, j, pt: (b, j, 0)),
        scratch_shapes=[pltpu.VMEM((CMP_BLOCK // 2, rows, 2 * LANES), F32)])
    return pl.pallas_call(
        functools.partial(_compress_sample_body, n_group=n_group),
        grid_spec=grid_spec,
        out_shape=jax.ShapeDtypeStruct((db, n_pages * _CMP_PER_PAGE, LANES), F32),
        compiler_params=_cparams(("arbitrary", "arbitrary")),
        name="nsa_compress_sample",
    )(page_table, *([cache_nsa] * n_group), pe_col, perm, phi)


def _nsa_local_sample_one(b, q, g, kc, vc, win_k, win_v, new_ref, past_len):
    t_new = q.shape[0]
    rows = N_HEADS * t_new
    nc = kc.shape[1]
    half = nc // 2
    qs = _stack_heads(q).astype(BF16)
    tq = lax.broadcasted_iota(jnp.int32, (rows, 1), 0) % t_new
    qpos = past_len + tq

    c = lax.broadcasted_iota(jnp.int32, (1, nc), 1)
    blk = jnp.where(c < half, 2 * c, 2 * (c - half) + 1)
    p_cmp = _masked_softmax(_dot(qs, kc), (blk + 1) * CMP_BLOCK - 1 <= qpos)
    o_cmp = _dot(p_cmp.astype(BF16), vc)
    imp = p_cmp[0:t_new]
    for h in range(1, N_HEADS):
        imp = imp + p_cmp[h * t_new:(h + 1) * t_new]
    imp = imp[:, :half] + imp[:, half:]
    cur = jnp.full((t_new, 1), past_len // SEL_BLOCK, jnp.int32)
    sel = _select_blocks(imp, cur, min(N_SELECT, half + 1) - 1)
    j = lax.broadcasted_iota(jnp.int32, sel.shape, 1)
    t = lax.broadcasted_iota(jnp.int32, sel.shape, 0)
    weight = jnp.left_shift(1, 2 * t + j % 2).astype(F32)
    colsum = jnp.sum(jnp.where(sel, weight, 0.0), axis=0, keepdims=True)
    flags = colsum + pltpu.roll(colsum, half - 1, 1)

    w = win_k.shape[1]
    n_cols = new_ref.shape[1]
    s_old = _dot(qs, win_k.astype(BF16))
    s_new = _dot(qs, new_ref[128:192, :])
    i_old = lax.broadcasted_iota(jnp.int32, (1, w), 1)
    mask = jnp.concatenate([jnp.broadcast_to(i_old + (WINDOW - w) > tq, (rows, w)),
                            _same_seq_causal(b, t_new, n_cols, rows)], axis=-1)
    p_win = _masked_softmax(jnp.concatenate([s_old, s_new], axis=-1), mask).astype(BF16)
    o_win = _nt(p_win[:, :w], win_v.astype(BF16)) + _nt(p_win[:, w:], new_ref[192:256, :])

    outs = [g[:, 3 * h:3 * h + 1] * o_cmp[h * t_new:(h + 1) * t_new]
            + g[:, 3 * h + 2:3 * h + 3] * o_win[h * t_new:(h + 1) * t_new] for h in range(N_HEADS)]
    return jnp.concatenate(outs, axis=-1), flags


def _nsa_local_sample_body(q_ref, g_ref, kc_ref, vc_ref, win_ref, new_ref, o_ref, flag_ref, *, past_len, t_new):
    n_seq = kc_ref.shape[0]
    for u in range(n_seq):
        rows = slice(u * t_new, (u + 1) * t_new)
        o, flags = _nsa_local_sample_one(pl.program_id(0) * n_seq + u, q_ref[rows, :], g_ref[rows, :], kc_ref[u],
                                         vc_ref[u], win_ref[0, u, 0:HEAD_DIM, :], win_ref[0, u, HEAD_DIM:, :],
                                         new_ref, past_len)
        o_ref[rows, :] = o
        flag_ref[u] = flags


def _nsa_local_sample(layer, qc, gc, kc_t, vc, win_t, new_bf, *, t_new, past_len):
    db = kc_t.shape[0]
    nc = kc_t.shape[2]
    w = win_t.shape[3]
    n_seq = 4 if db % 4 == 0 else 1
    rows = lambda width: pl.BlockSpec((n_seq * t_new, width), lambda b: (b, 0))
    return pl.pallas_call(
        functools.partial(_nsa_local_sample_body, past_len=past_len, t_new=t_new),
        grid=(db // n_seq,),
        in_specs=[rows(GROUP_WIDTH), rows(LANES),
                  pl.BlockSpec((n_seq, HEAD_DIM, nc), lambda b: (b, 0, 0)),
                  pl.BlockSpec((n_seq, nc, HEAD_DIM), lambda b: (b, 0, 0)),
                  pl.BlockSpec((1, n_seq, 2 * HEAD_DIM, w), lambda b: (layer, b, 0, 0)),
                  pl.BlockSpec(new_bf.shape, lambda b: (0, 0))],
        out_specs=[rows(GROUP_WIDTH), pl.BlockSpec((n_seq, 1, nc // 2), lambda b: (b, 0, 0))],
        out_shape=[jax.ShapeDtypeStruct((db * t_new, GROUP_WIDTH), F32),
                   jax.ShapeDtypeStruct((db, 1, nc // 2), F32)],
        compiler_params=_cparams(("arbitrary",)),
        name="nsa_local_sample",
    )(qc, gc, kc_t, vc, win_t, new_bf)


_SEL_GROUP = 16


def _nsa_sel_sample_body(pt_ref, fl_ref, q_ref, g_ref, part_ref, new_ref, kv_hbm, o_ref,
                         buf, sems, cflag, count, *, layer):
    b = pl.program_id(0)
    nb = pl.num_programs(0)
    n_pages = fl_ref.shape[1]
    slot = b % 2
    t_new = q_ref.shape[0]
    rows = N_HEADS * t_new

    def copy(bb, p, sl, k):
        src = kv_hbm.at[layer, pt_ref[bb, p], pl.ds(2 * HEAD_DIM, 2 * HEAD_DIM)]
        return pltpu.make_async_copy(src, buf.at[sl, k], sems.at[sl])

    def start_all(bb, sl):
        def body(p, k):
            flag = fl_ref[bb, p]

            @pl.when(flag != 0)
            def _():
                copy(bb, p, sl, k).start()
                cflag[sl, k] = flag
            return k + (flag != 0).astype(jnp.int32)
        count[sl] = lax.fori_loop(0, n_pages, body, 0)

    @pl.when(b == 0)
    def _():
        start_all(b, slot)

    @pl.when(b + 1 < nb)
    def _():
        start_all(b + 1, 1 - slot)

    qs = _stack_heads(q_ref[...]).astype(BF16)
    shamt = (2 * (lax.broadcasted_iota(jnp.int32, (rows, LANES), 0) % t_new)
             + lax.broadcasted_iota(jnp.int32, (rows, LANES), 1) // SEL_BLOCK)
    n_fetched = count[slot]

    def wait_body(k, c):
        copy(b, 0, slot, k).wait()
        return c
    lax.fori_loop(0, n_fetched, wait_body, 0)

    def group(gi, carry):
        m_old, l_old, acc = carry
        scores, entries = [], []
        for u in range(_SEL_GROUP):
            k = gi * _SEL_GROUP + u
            entry = jnp.minimum(k, n_fetched - 1)
            flag = jnp.where(k < n_fetched, cflag[slot, entry], 0)
            picked = (jnp.right_shift(jnp.full((rows, LANES), flag, jnp.int32), shamt) & 1) == 1
            scores.append(jnp.where(picked, _dot(qs, buf[slot, entry, 0:HEAD_DIM, :].astype(BF16)), NEG_BIG))
            entries.append(entry)
        m = jnp.maximum(m_old, jnp.max(functools.reduce(jnp.maximum, scores), axis=-1, keepdims=True))
        alpha = jnp.exp(m_old - m)
        psum = jnp.zeros((rows, LANES), F32)
        acc = alpha * acc
        for u in range(_SEL_GROUP):
            p = jnp.exp(scores[u] - m)
            psum = psum + p
            acc = acc + _nt(p.astype(BF16), buf[slot, entries[u], HEAD_DIM:, :].astype(BF16))
        return m, alpha * l_old + jnp.sum(psum, axis=-1, keepdims=True), acc

    carry = lax.fori_loop(0, (n_fetched + _SEL_GROUP - 1) // _SEL_GROUP, group, _softmax_init(rows, HEAD_DIM))
    s = jnp.where(_same_seq_causal(b, t_new, new_ref.shape[1], rows), _dot(qs, new_ref[0:64, :]), NEG_BIG)
    m, l, acc = _online_softmax_step(s, new_ref[64:128, :], carry)
    o_sel = acc / l
    g = g_ref[...]
    outs = [g[:, 3 * h + 1:3 * h + 2] * o_sel[h * t_new:(h + 1) * t_new] for h in range(N_HEADS)]
    o_ref[...] = part_ref[...] + jnp.concatenate(outs, axis=-1)


def _nsa_sel_sample(layer, page_table, flags, qc, gc, part, new_bf, cache_nsa, *, t_new):
    db, n_pages = page_table.shape
    rows = N_HEADS * t_new
    tile = lambda width: pl.BlockSpec((t_new, width), lambda b, pt, fl: (b, 0))
    grid_spec = pltpu.PrefetchScalarGridSpec(
        num_scalar_prefetch=2,
        grid=(db,),
        in_specs=[tile(GROUP_WIDTH), tile(LANES), tile(GROUP_WIDTH),
                  pl.BlockSpec(new_bf.shape, lambda b, pt, fl: (0, 0)),
                  pl.BlockSpec(memory_space=pl.ANY)],
        out_specs=tile(GROUP_WIDTH),
        scratch_shapes=[pltpu.VMEM((2, n_pages, 2 * HEAD_DIM, LANES), F32), pltpu.SemaphoreType.DMA((2,)),
                        pltpu.SMEM((2, n_pages), jnp.int32), pltpu.SMEM((2,), jnp.int32)])
    return pl.pallas_call(
        functools.partial(_nsa_sel_sample_body, layer=layer),
        grid_spec=grid_spec,
        out_shape=jax.ShapeDtypeStruct((db * t_new, GROUP_WIDTH), F32),
        compiler_params=_cparams(("arbitrary",)),
        name="nsa_sel_sample",
    )(page_table, flags, qc, gc, part, new_bf, cache_nsa)


_SPLITS = np.cumsum([0, 256, 256, 256, 4, 512, 256, 384, 12, 256, 256])


def _prep_layer(l, P):
    w_in = P['w_in'][l]
    sec = [w_in[:, _SPLITS[i]:_SPLITS[i + 1]] for i in range(10)]
    qa, ka, va, fa, glu, qc, kvc, gc, ud, vd = sec
    w_row = jnp.concatenate([qa, glu, qc, ud, vd, jnp.pad(gc, ((0, 0), (0, LANES - 12)))], axis=1).astype(BF16)
    w_col = jnp.concatenate([ka, va, kvc, jnp.pad(fa, ((0, 0), (0, 12)))], axis=1).T.astype(BF16)
    row = lambda v: v.reshape(1, -1)
    phi_k = P['nsa_phi_k'][l].reshape(CMP_BLOCK, 1, HEAD_DIM, HEAD_DIM)
    phi_v = P['nsa_phi_v'][l].reshape(CMP_BLOCK, 1, HEAD_DIM, HEAD_DIM)
    zero = jnp.zeros_like(phi_k)
    phi = jnp.concatenate([jnp.concatenate([phi_k, zero], axis=-1), jnp.concatenate([zero, phi_v], axis=-1)], axis=1)
    return dict(
        pe_flat=jnp.transpose(P['nsa_pe'][l], (1, 0, 2)).reshape(1, CMP_BLOCK * LANES),
        pe_col=jnp.tile(jnp.transpose(P['nsa_pe'][l], (0, 2, 1)).reshape(LANES, CMP_BLOCK), (1, _CMP_PER_PAGE)),
        phi=phi.reshape(CMP_BLOCK * LANES, LANES).astype(BF16),
        norm1_g=row(P['norm1_g'][l]), w_row=w_row, w_col=w_col,
        gqa=row(jnp.tile(P['fox_qn_g'][l], N_HEADS)), gka=P['fox_kn_g'][l].reshape(HEAD_DIM, 1),
        bf=jnp.pad(P['fox_bf'][l], (0, 4)).reshape(8, 1),
        gqc=row(jnp.tile(P['nsa_qn_g'][l], N_HEADS)), gkc=P['nsa_kn_g'][l].T,
        gmlp_ln_g=row(P['gmlp_ln_g'][l]), gmlp_ln_b=row(P['gmlp_ln_b'][l]),
        gmlp_ws=P['gmlp_ws'][l], gmlp_bs=P['gmlp_bs'][l],
        gnorm_g=P['gnorm_g'][l], w_out=P['w_out'][l].astype(BF16), norm2_g=row(P['norm2_g'][l]),
        w_ff1=P['w_ff1'][l].astype(BF16), w_ff2=P['w_ff2'][l].astype(BF16),
        conv_w=P['conv_w'][l], conv_b=row(P['conv_b'][l]),
        conv_ln_g=row(P['conv_ln_g'][l]), conv_ln_b=row(P['conv_ln_b'][l]),
    )


def _rope_tables(pos):
    inv = ROPE_THETA ** (-jnp.arange(HALF, dtype=F32) / HALF)
    ang = pos.astype(F32)[:, None] * inv
    cos, sin = jnp.cos(ang), jnp.sin(ang)
    return dict(cos_r=jnp.tile(cos, (1, 4)), sin_r=jnp.tile(jnp.concatenate([-sin, sin], axis=1), (1, 2)),
                cos_t=cos.T, sin_t=sin.T)


def _const_tables():
    g = np.arange(GROUP_WIDTH) // HEAD_DIM
    return dict(gsum=jnp.asarray(g[:, None] == g[None, :], BF16))


def _gmlp_tables(w, seq_len, n_seq):
    t = min(seq_len, CHUNK)
    wm = (w['gmlp_ws'] * jnp.tril(jnp.ones((CHUNK, CHUNK), F32)))[:, :t, :t]
    bs = w['gmlp_bs'][:, :t]
    if seq_len < CHUNK:
        eye = jnp.eye(n_seq, dtype=F32)
        wm = jnp.einsum('ab,gts->gatbs', eye, wm).reshape(N_HEADS, n_seq * t, n_seq * t)
        bs = jnp.tile(bs, (1, n_seq))
    c = wm.shape[1]
    return dict(wm=wm.reshape(N_HEADS * c, c).astype(BF16), bs_tab=jnp.repeat(bs.T, HEAD_DIM, axis=1))


def _layer_prompt(x, w, consts, *, tm, ta, tk, tf):
    B, S, D = x.shape
    tabs = dict(consts, **_gmlp_tables(w, S, B))
    o = _proj(x, w, tabs, tm=tm, chunk=CHUNK)
    o_a = _fox_attn(o['qat'], o['karow'], o['vbf'], t=ta)
    o_b = _conv_prompt(o['glu'], w, tm=tm)
    o_c = _nsa_prompt(o, w, t=tk, tk=tk, tl=min(4 * LANES, S))
    flat = lambda a: a.reshape(B * S, a.shape[-1])
    y = _merge_ffn(flat(x), [flat(o_a), flat(o_b), flat(o_c), flat(o['od'])], w, tm=min(2 * tm, B * S), tf=tf)
    wp = min(WINDOW, S)
    states = dict(fox_kv=o['foxkv'], fox_logf=o['logf'], nsa_kv=o['nsakv'],
                  nsa_win=o['nsawin'][:, :, S - wp:], conv=o['glu'][:, S - (CONV_WIDTH - 1):])
    return y.reshape(B, S, D), states


def _even_odd(cmp):
    n = cmp.shape[1]
    order = np.concatenate([np.arange(0, n, 2), np.arange(1, n, 2)])
    cmp = cmp[:, order].astype(BF16)
    return jnp.swapaxes(cmp[:, :, :HEAD_DIM], 1, 2), cmp[:, :, HEAD_DIM:]


def _prep_caches(cache_fox_kv, cache_fox_logf, cache_nsa_kv, state_nsa_win, state_conv):
    L, pool, page = cache_fox_kv.shape[:3]
    db = state_nsa_win.shape[1]
    return dict(
        fox_kv=jnp.transpose(cache_fox_kv, (0, 1, 3, 4, 5, 2)).reshape(L, pool, 2 * GROUP_WIDTH, page),
        fox_lf=jnp.transpose(cache_fox_logf, (0, 1, 3, 2)),
        nsa_kv=jnp.transpose(cache_nsa_kv, (0, 1, 3, 4, 2)).reshape(L, pool, 4 * HEAD_DIM, page),
        win=jnp.transpose(state_nsa_win, (0, 1, 3, 4, 2)).reshape(L, db, 2 * HEAD_DIM, -1),
        conv=jnp.transpose(state_conv, (0, 2, 1, 3)),
    )


def _layer_sample(l, xs, w, caches, consts, page_table, *, past_len, pg, tf):
    db, t_new, D = xs.shape
    R = db * t_new
    tabs = dict(consts, **_gmlp_tables(w, t_new, db))
    o = {k: v[0] for k, v in _proj(xs.reshape(1, R, D), w, tabs, tm=R, chunk=R).items()}
    o_a = _fox_sample(l, page_table, o['qa'].astype(F32), o['kaug'], o['vbf'], caches['fox_kv'], caches['fox_lf'],
                      t_new=t_new, pg=pg)
    glu_t = jnp.swapaxes(o['glu'].reshape(db, t_new, GROUP_WIDTH), 0, 1)
    o_b_t, conv_new = _conv_sample(l, caches['conv'], glu_t, w)
    o_b = jnp.swapaxes(o_b_t, 0, 1).reshape(R, GROUP_WIDTH)
    cmp = _compress_sample(l, page_table, caches['nsa_kv'], w['pe_col'], w['phi'], n_group=pg)
    kc_t, vc = _even_odd(cmp)
    qc = o['qc'].astype(F32)
    part, flags = _nsa_local_sample(l, qc, o['gc'], kc_t, vc, caches['win'], o['nsabf'],
                                    t_new=t_new, past_len=past_len)
    flags = flags[:, 0, ::2].astype(jnp.int32)
    o_c = _nsa_sel_sample(l, page_table, flags, qc, o['gc'], part, o['nsabf'], caches['nsa_kv'], t_new=t_new)
    y = _merge_ffn(xs.reshape(R, D), [o_a, o_b, o_c, o['od']], w, tm=R, tf=tf)
    rows = lambda a, *shape: a.T.reshape(db, t_new, *shape)
    win_new = jnp.swapaxes(o['nsawin'].reshape(2 * HEAD_DIM, db, t_new), 0, 1)
    win = jnp.concatenate([caches['win'][l][:, :, t_new:], win_new], axis=-1)
    states = dict(fox_kv=rows(o['foxkv'], 2, N_HEADS, HEAD_DIM), fox_logf=rows(o['logf'], N_HEADS),
                  nsa_kv=rows(o['nsakv'], 4, HEAD_DIM),
                  nsa_win=jnp.transpose(win.reshape(db, 2, HEAD_DIM, -1), (0, 3, 1, 2)),
                  conv=jnp.swapaxes(conv_new, 0, 1), gmlp_v=o['vn'].reshape(db, t_new, GROUP_WIDTH))
    return y.reshape(db, t_new, D), states


def _scan_matrix(n, seg):
    i = np.arange(n)
    return jnp.asarray((i[:, None] <= i[None, :]) & (i[:, None] // seg == i[None, :] // seg), BF16)


_PARAM_NAMES = ('norm1_g', 'w_in', 'fox_bf', 'fox_qn_g', 'fox_kn_g', 'conv_w', 'conv_b', 'conv_ln_g', 'conv_ln_b',
                'nsa_qn_g', 'nsa_kn_g', 'nsa_pe', 'nsa_phi_k', 'nsa_phi_v', 'gmlp_ln_g', 'gmlp_ln_b', 'gmlp_ws',
                'gmlp_bs', 'gnorm_g', 'w_out', 'norm2_g', 'w_ff1', 'w_ff2')


def kernel(x_prompt, x_sample, cache_fox_kv, cache_fox_logf, cache_nsa_kv, state_nsa_win, state_conv, page_table,
           *params):
    P = dict(zip(_PARAM_NAMES, params))
    depth = P['w_in'].shape[0]
    B, S, D = x_prompt.shape
    DB, T, _ = x_sample.shape
    n_pages, page = page_table.shape[1], cache_fox_kv.shape[2]
    past_len = n_pages * page
    assert past_len % SEL_BLOCK == 0 and T <= SEL_BLOCK and past_len >= WINDOW and page == LANES
    tm = 512
    consts_p = dict(_rope_tables(jnp.arange(S)), **_const_tables(), utri=_scan_matrix(tm, tm))
    consts_s = dict(_rope_tables(jnp.tile(past_len + jnp.arange(T), DB)), **_const_tables(),
                    utri=_scan_matrix(DB * T, T))
    caches = _prep_caches(cache_fox_kv, cache_fox_logf, cache_nsa_kv, state_nsa_win, state_conv)
    xp, xs = x_prompt, x_sample
    st_p, st_s = [], []
    for l in range(depth):
        w = _prep_layer(l, P)
        xp, sp = _layer_prompt(xp, w, consts_p, tm=tm, ta=512, tk=512, tf=1024)
        xs, ss = _layer_sample(l, xs, w, caches, consts_s, page_table, past_len=past_len, pg=32, tf=1024)
        st_p.append(sp)
        st_s.append(ss)
    stack_p = lambda k: jnp.stack([s[k] for s in st_p])
    stack_s = lambda k: jnp.stack([s[k] for s in st_s])
    fox_kv_p = jnp.transpose(stack_p('fox_kv').reshape(depth, B, 2, N_HEADS, HEAD_DIM, S), (0, 1, 5, 2, 3, 4))
    fox_logf_p = jnp.transpose(stack_p('fox_logf'), (0, 1, 3, 2))
    nsa_kv_p = jnp.transpose(stack_p('nsa_kv').reshape(depth, B, 4, HEAD_DIM, S), (0, 1, 4, 2, 3))
    nsa_win_p = jnp.transpose(stack_p('nsa_win').reshape(depth, B, 2, HEAD_DIM, -1), (0, 1, 4, 2, 3))
    return (xp, xs, fox_kv_p, stack_s('fox_kv'), fox_logf_p, stack_s('fox_logf'),
            nsa_kv_p, stack_s('nsa_kv'), nsa_win_p, stack_s('nsa_win'),
            stack_p('conv'), stack_s('conv'), stack_s('gmlp_v'))
```
